```python
import jax, jax.numpy as jnp
from jax import lax
import numpy as np

D_MODEL = 1024
BATCH = 8
SEQ = 2048
DEPTH = 4
DEC_BATCH = 32
DEC_SEQ = 4
PAST_LEN = 8192
PAGE_SIZE = 128

D_S5 = D_MODEL // 2
S5_GROUP = 16
S5_GROUPS = D_S5 // S5_GROUP
S5_STATE = 64
D_NSA = D_MODEL // 2
HEAD_DIM = 64
N_HEADS = D_NSA // HEAD_DIM
N_KV = 2
GROUP_SIZE = N_HEADS // N_KV
BLOCK = 64
N_SELECT = 16
WINDOW = 512
QUERY_BLOCK = 64
ROT_DIM = HEAD_DIM // 4
ROT_HALF = ROT_DIM // 2
ROPE_THETA = 500000.0
RMS_EPS = 1e-6
NEG_INF = -1e30
FORCED_SCORE = 1e4
KV_W = N_KV * HEAD_DIM
IN_SIZES = (D_S5, D_S5, D_NSA, 6 * KV_W, 3 * N_HEADS, D_NSA, 2 * D_MODEL)
N_IN = 2 * D_S5 + 2 * D_NSA + 6 * KV_W + 3 * N_HEADS + 2 * D_MODEL

kernel_name = 'hybrid_s5_nsa_adaln_step'


def rmsnorm(x, g):
    xf = x.astype(jnp.float32)
    y = xf * lax.rsqrt(jnp.mean(xf * xf, axis=-1, keepdims=True) + RMS_EPS) * g
    return y.astype(x.dtype)


def rope(x, pos):
    inv = ROPE_THETA ** (-jnp.arange(ROT_HALF, dtype=jnp.float32) / ROT_HALF)
    ang = pos.astype(jnp.float32)[:, None] * inv[None, :]
    cos = jnp.cos(ang)[:, None, :]
    sin = jnp.sin(ang)[:, None, :]
    x1 = x[..., :ROT_HALF].astype(jnp.float32)
    x2 = x[..., ROT_HALF:ROT_DIM].astype(jnp.float32)
    rot = jnp.concatenate([x1 * cos - x2 * sin, x1 * sin + x2 * cos], axis=-1).astype(x.dtype)
    return jnp.concatenate([rot, x[..., ROT_DIM:]], axis=-1)


def masked_softmax(s, mask):
    s = jnp.where(mask, s.astype(jnp.float32), NEG_INF)
    return jnp.where(mask, jax.nn.softmax(s, axis=-1), 0.0)


def s5_scan(u, h0, a_re, a_im, log_dt, b_re, b_im, c_re, c_im, d):
    bsz, t, _ = u.shape
    uf = u.astype(jnp.float32).reshape(bsz, t, S5_GROUPS, S5_GROUP)
    dt = jnp.exp(log_dt.astype(jnp.float32))[:, None]
    a_re = a_re.astype(jnp.float32)
    a_im = a_im.astype(jnp.float32)
    mag = jnp.exp(a_re * dt)
    ab_re = mag * jnp.cos(a_im * dt)
    ab_im = mag * jnp.sin(a_im * dt)
    den = a_re * a_re + a_im * a_im
    nr = ab_re - 1.0
    e_re = (nr * a_re + ab_im * a_im) / den
    e_im = (ab_im * a_re - nr * a_im) / den
    b_re = b_re.astype(jnp.float32)
    b_im = b_im.astype(jnp.float32)
    bb_re = e_re[..., None] * b_re - e_im[..., None] * b_im
    bb_im = e_re[..., None] * b_im + e_im[..., None] * b_re
    bu_re = jnp.einsum('gpc,btgc->btgp', bb_re, uf)
    bu_im = jnp.einsum('gpc,btgc->btgp', bb_im, uf)
    at_re = jnp.broadcast_to(ab_re, bu_re.shape)
    at_im = jnp.broadcast_to(ab_im, bu_im.shape)

    def combine(l, r):
        lar, lai, lbr, lbi = l
        rar, rai, rbr, rbi = r
        return (lar * rar - lai * rai, lar * rai + lai * rar,
                rar * lbr - rai * lbi + rbr, rar * lbi + rai * lbr + rbi)

    acr, aci, hr, hi = lax.associative_scan(combine, (at_re, at_im, bu_re, bu_im), axis=1)
    if h0 is not None:
        h0f = h0.astype(jnp.float32)
        h0r = h0f[:, None, 0]
        h0i = h0f[:, None, 1]
        hr, hi = hr + acr * h0r - aci * h0i, hi + acr * h0i + aci * h0r
    y = (jnp.einsum('gcp,btgp->btgc', c_re.astype(jnp.float32), hr)
         - jnp.einsum('gcp,btgp->btgc', c_im.astype(jnp.float32), hi)
         + d.astype(jnp.float32) * uf)
    h_last = jnp.stack([hr[:, -1], hi[:, -1]], axis=1)
    return y.reshape(bsz, t, D_S5), h_last


def compress_rows(rows, pe, w1, w2):
    bsz, tk = rows.shape[:2]
    nb = tk // BLOCK
    blk = rows.reshape(bsz, nb, BLOCK, N_KV, HEAD_DIM) + pe[None, None, :, None, :]
    blk = blk.transpose(0, 1, 3, 2, 4).reshape(bsz, nb, N_KV, BLOCK * HEAD_DIM)
    return jax.nn.silu(blk @ w1) @ w2


def nsa_attention(q, qpos, kv_all, kw, vw, kwpos, gate_logits, cmp_pe, cmp_w1, cmp_w2, window_chunked):
    bsz, tq = q.shape[:2]
    nb = kv_all.shape[1] // BLOCK
    qg = q.reshape(bsz, tq, N_KV, GROUP_SIZE, HEAD_DIM) * (HEAD_DIM ** -0.5)

    kc = compress_rows(kv_all[:, :, 0], cmp_pe[0], cmp_w1[0], cmp_w2[0])
    vc = compress_rows(kv_all[:, :, 1], cmp_pe[1], cmp_w1[1], cmp_w2[1])
    cpos = jnp.arange(nb) * BLOCK + (BLOCK - 1)
    kc = rope(kc, cpos)
    s_c = jnp.einsum('btgrd,bngd->btgrn', qg, kc)
    mask_c = (cpos[None, :] <= qpos[:, None])[None, :, None, None, :]
    p_c = masked_softmax(s_c, mask_c)
    o_c = jnp.einsum('btgrn,bngd->btgrd', p_c.astype(vc.dtype), vc)

    imp = p_c.sum(axis=3)
    blk = jnp.arange(nb)[None, :]
    qblk = (qpos // BLOCK)[:, None]
    forced = (blk == 0) | (blk == qblk) | (blk == qblk - 1)
    imp = jnp.where(forced[None, :, None, :], FORCED_SCORE, imp)
    imp = jnp.where((blk > qblk)[None, :, None, :], -1.0, imp)
    n_sel = min(N_SELECT, nb)
    top_val, top_idx = lax.top_k(imp, n_sel)
    top_ok = top_val > -0.5
    kb = kv_all[:, :, 2].reshape(bsz, nb, BLOCK, N_KV, HEAD_DIM).transpose(0, 3, 1, 2, 4)
    vb = kv_all[:, :, 3].reshape(bsz, nb, BLOCK, N_KV, HEAD_DIM).transpose(0, 3, 1, 2, 4)

    qb = QUERY_BLOCK if tq % QUERY_BLOCK == 0 else tq
    n_chunks = tq // qb
    wlen = WINDOW + qb
    bi = jnp.arange(bsz)[:, None, None, None]
    gi = jnp.arange(N_KV)[None, None, :, None]

    def chunk(i):
        s0 = i * qb
        qc = lax.dynamic_slice_in_dim(qg, s0, qb, 1)
        pc = lax.dynamic_slice_in_dim(qpos, s0, qb, 0)
        ic = lax.dynamic_slice_in_dim(top_idx, s0, qb, 1)
        okc = lax.dynamic_slice_in_dim(top_ok, s0, qb, 1)
        ks = kb[bi, gi, ic]
        vs = vb[bi, gi, ic]
        s_s = jnp.einsum('bqgrd,bqgkpd->bqgrkp', qc, ks)
        kpos = ic[..., None] * BLOCK + jnp.arange(BLOCK)
        mask_s = okc[..., None] & (kpos <= pc[None, :, None, None, None])
        p_s = masked_softmax(s_s.reshape(bsz, qb, N_KV, GROUP_SIZE, n_sel * BLOCK),
                             mask_s.reshape(bsz, qb, N_KV, 1, n_sel * BLOCK))
        p_s = p_s.reshape(s_s.shape).astype(vs.dtype)
        o_s = jnp.einsum('bqgrkp,bqgkpd->bqgrd', p_s, vs)
        if window_chunked:
            kwc = lax.dynamic_slice_in_dim(kw, s0, wlen, 1)
            vwc = lax.dynamic_slice_in_dim(vw, s0, wlen, 1)
            kwpc = lax.dynamic_slice_in_dim(kwpos, s0, wlen, 0)
        else:
            kwc, vwc, kwpc = kw, vw, kwpos
        s_w = jnp.einsum('bqgrd,bkgd->bqgrk', qc, kwc)
        mask_w = ((kwpc[None, :] <= pc[:, None]) & (kwpc[None, :] > pc[:, None] - WINDOW)
                  & (kwpc[None, :] >= 0))[None, :, None, None, :]
        p_w = masked_softmax(s_w, mask_w).astype(vwc.dtype)
        o_w = jnp.einsum('bqgrk,bkgd->bqgrd', p_w, vwc)
        return o_s, o_w

    o_s, o_w = lax.map(chunk, jnp.arange(n_chunks))
    o_s = o_s.transpose(1, 0, 2, 3, 4, 5).reshape(bsz, tq, N_KV, GROUP_SIZE, HEAD_DIM)
    o_w = o_w.transpose(1, 0, 2, 3, 4, 5).reshape(bsz, tq, N_KV, GROUP_SIZE, HEAD_DIM)
    g = jax.nn.sigmoid(gate_logits.astype(jnp.float32)).reshape(bsz, tq, 3, N_KV, GROUP_SIZE, 1)
    o = g[:, :, 0] * o_c + g[:, :, 1] * o_s + g[:, :, 2] * o_w
    return o.reshape(bsz, tq, D_NSA).astype(q.dtype)


def layer(x, c, qpos, kv_past, win_past, ssm_h0, lp):
    bsz, t = x.shape[:2]
    mod = jax.nn.silu(c) @ lp['ada_w'] + lp['ada_b']
    shift, scale, gate = jnp.split(mod, 3, axis=-1)
    h = rmsnorm(x, lp['norm_g']) * (1.0 + scale[:, None]) + shift[:, None]
    proj = h @ lp['w_in']
    splits = [int(s) for s in np.cumsum(IN_SIZES)[:-1]]
    u, z_s5, q, kvx, gate_logits, z_nsa, merge_logits = jnp.split(proj, splits, axis=-1)

    y5, h5 = s5_scan(u, ssm_h0, lp['s5_a_re'], lp['s5_a_im'], lp['s5_log_dt'], lp['s5_b_re'],
                     lp['s5_b_im'], lp['s5_c_re'], lp['s5_c_im'], lp['s5_d'])
    y5 = jax.nn.gelu(y5)
    y5 = y5 * jax.nn.sigmoid(y5 @ lp['s5_glu_w'] + lp['s5_glu_b'])
    y5 = (y5 * jax.nn.silu(z_s5.astype(jnp.float32))).astype(x.dtype)
    b_s5 = y5 @ lp['w_s5_out']

    q = rope(q.reshape(bsz, t, N_HEADS, HEAD_DIM), qpos)
    kvx = kvx.reshape(bsz, t, 6, N_KV, HEAD_DIM)
    k_sel = rope(kvx[:, :, 2], qpos)
    k_win = rope(kvx[:, :, 4], qpos)
    kv_new = jnp.stack([kvx[:, :, 0], kvx[:, :, 1], k_sel, kvx[:, :, 3]], axis=2)
    win_rows = jnp.stack([k_win, kvx[:, :, 5]], axis=2)
    if kv_past is None:
        kv_all = kv_new
        pad = ((0, 0), (WINDOW, 0), (0, 0), (0, 0))
        kw = jnp.pad(win_rows[:, :, 0], pad)
        vw = jnp.pad(win_rows[:, :, 1], pad)
        kwpos = jnp.arange(t + WINDOW) - WINDOW
        keep = min(WINDOW, t)
        win_state = win_rows[:, t - keep:]
        chunked = True
    else:
        kv_all = jnp.concatenate([kv_past.astype(kv_new.dtype), kv_new], axis=1)
        extra = (-kv_all.shape[1]) % BLOCK
        kv_all = jnp.pad(kv_all, ((0, 0), (0, extra), (0, 0), (0, 0), (0, 0)))
        wbuf = win_past.shape[1]
        win_all = jnp.concatenate([win_past.astype(win_rows.dtype), win_rows], axis=1)
        kw = win_all[:, :, 0]
        vw = win_all[:, :, 1]
        kwpos = qpos[0] - wbuf + jnp.arange(wbuf + t)
        win_state = win_all[:, t:]
        chunked = False
    o = nsa_attention(q, qpos, kv_all, kw, vw, kwpos, gate_logits,
                      lp['cmp_pe'], lp['cmp_w1'], lp['cmp_w2'], chunked)
    b_nsa = (o * jax.nn.silu(z_nsa)) @ lp['w_nsa_out']

    m_s5, m_nsa = jnp.split(jax.nn.sigmoid(merge_logits), 2, axis=-1)
    out = (m_s5 * b_s5 + m_nsa * b_nsa) @ lp['w_o']
    x = x + gate[:, None] * out
    return x, kv_new, win_state, h5


def setup_inputs(seed: int = 0) -> dict:
    key = jax.random.key(seed)
    ks = jax.random.split(key, 32)
    f32 = jnp.float32

    def nrm(k, shape, s):
        return jax.random.normal(k, shape, f32) * s

    n_pages = PAST_LEN // PAGE_SIZE
    n_used = DEC_BATCH * n_pages
    n_pool = (n_used * 5) // 4
    win_buf = min(WINDOW, PAST_LEN)
    x_prompt = nrm(ks[0], (BATCH, SEQ, D_MODEL), 1.0)
    x_sample = nrm(ks[1], (DEC_BATCH, DEC_SEQ, D_MODEL), 1.0)
    c_prompt = nrm(ks[2], (BATCH, D_MODEL), 1.0)
    c_sample = nrm(ks[3], (DEC_BATCH, D_MODEL), 1.0)
    cache_kv = nrm(ks[4], (n_pool, DEPTH, PAGE_SIZE, 4, N_KV, HEAD_DIM), 1.0)
    page_table = jax.random.permutation(ks[5], n_pool)[:n_used].reshape(DEC_BATCH, n_pages).astype(jnp.int32)
    state_win = nrm(ks[6], (DEC_BATCH, DEPTH, win_buf, 2, N_KV, HEAD_DIM), 1.0)
    state_ssm = nrm(ks[7], (DEC_BATCH, DEPTH, 2, S5_GROUPS, S5_STATE), 0.5)
    ada_w = nrm(ks[8], (DEPTH, D_MODEL, 3 * D_MODEL), 0.5 * D_MODEL ** -0.5)
    ada_b = nrm(ks[9], (DEPTH, 3 * D_MODEL), 0.01)
    norm_g = 1.0 + nrm(ks[10], (DEPTH, D_MODEL), 0.02)
    w_in = nrm(ks[11], (DEPTH, D_MODEL, N_IN), D_MODEL ** -0.5)
    s5_a_re = -0.5 + nrm(ks[12], (DEPTH, S5_GROUPS, S5_STATE), 0.01)
    s5_a_im = jnp.pi * jnp.arange(S5_STATE, dtype=f32) + nrm(ks[13], (DEPTH, S5_GROUPS, S5_STATE), 0.01)
    s5_log_dt = jax.random.uniform(ks[14], (DEPTH, S5_GROUPS), f32, float(np.log(1e-3)), float(np.log(1e-1)))
    s5_b_re = nrm(ks[15], (DEPTH, S5_GROUPS, S5_STATE, S5_GROUP), (2 * S5_GROUP) ** -0.5)
    s5_b_im = nrm(ks[16], (DEPTH, S5_GROUPS, S5_STATE, S5_GROUP), (2 * S5_GROUP) ** -0.5)
    s5_c_re = nrm(ks[17], (DEPTH, S5_GROUPS, S5_GROUP, S5_STATE), (2 * S5_STATE) ** -0.5)
    s5_c_im = nrm(ks[18], (DEPTH, S5_GROUPS, S5_GROUP, S5_STATE), (2 * S5_STATE) ** -0.5)
    s5_d = nrm(ks[19], (DEPTH, S5_GROUPS, S5_GROUP), 1.0)
    s5_glu_w = nrm(ks[20], (DEPTH, D_S5, D_S5), D_S5 ** -0.5)
    s5_glu_b = nrm(ks[21], (DEPTH, D_S5), 0.01)
    cmp_pe = nrm(ks[22], (DEPTH, 2, BLOCK, HEAD_DIM), 0.02)
    cmp_w1 = nrm(ks[23], (DEPTH, 2, BLOCK * HEAD_DIM, HEAD_DIM), (BLOCK * HEAD_DIM) ** -0.5)
    cmp_w2 = nrm(ks[24], (DEPTH, 2, HEAD_DIM, HEAD_DIM), HEAD_DIM ** -0.5)
    w_s5_out = nrm(ks[25], (DEPTH, D_S5, D_MODEL), D_S5 ** -0.5)
    w_nsa_out = nrm(ks[26], (DEPTH, D_NSA, D_MODEL), D_NSA ** -0.5)
    w_o = nrm(ks[27], (DEPTH, D_MODEL, D_MODEL), D_MODEL ** -0.5)
    final_g = 1.0 + nrm(ks[28], (D_MODEL,), 0.02)
    return {'x_prompt': x_prompt, 'x_sample': x_sample, 'c_prompt': c_prompt, 'c_sample': c_sample,
            'cache_kv': cache_kv, 'page_table': page_table, 'state_win': state_win, 'state_ssm': state_ssm,
            'ada_w': ada_w, 'ada_b': ada_b, 'norm_g': norm_g, 'w_in': w_in,
            's5_a_re': s5_a_re, 's5_a_im': s5_a_im, 's5_log_dt': s5_log_dt,
            's5_b_re': s5_b_re, 's5_b_im': s5_b_im, 's5_c_re': s5_c_re, 's5_c_im': s5_c_im, 's5_d': s5_d,
            's5_glu_w': s5_glu_w, 's5_glu_b': s5_glu_b, 'cmp_pe': cmp_pe, 'cmp_w1': cmp_w1, 'cmp_w2': cmp_w2,
            'w_s5_out': w_s5_out, 'w_nsa_out': w_nsa_out, 'w_o': w_o, 'final_g': final_g}


def reference(x_prompt, x_sample, c_prompt, c_sample, cache_kv, page_table, state_win, state_ssm,
              ada_w, ada_b, norm_g, w_in, s5_a_re, s5_a_im, s5_log_dt, s5_b_re, s5_b_im,
              s5_c_re, s5_c_im, s5_d, s5_glu_w, s5_glu_b, cmp_pe, cmp_w1, cmp_w2,
              w_s5_out, w_nsa_out, w_o, final_g):
    n_pages = page_table.shape[1]
    page = cache_kv.shape[2]
    past_len = n_pages * page
    qpos_p = jnp.arange(x_prompt.shape[1])
    qpos_s = past_len + jnp.arange(x_sample.shape[1])
    xp, xs = x_prompt, x_sample
    kv_p, kv_s, win_p, win_s, ssm_p, ssm_s = [], [], [], [], [], []
    for l in range(DEPTH):
        lp = {'ada_w': ada_w[l], 'ada_b': ada_b[l], 'norm_g': norm_g[l], 'w_in': w_in[l],
              's5_a_re': s5_a_re[l], 's5_a_im': s5_a_im[l], 's5_log_dt': s5_log_dt[l],
              's5_b_re': s5_b_re[l], 's5_b_im': s5_b_im[l], 's5_c_re': s5_c_re[l], 's5_c_im': s5_c_im[l],
              's5_d': s5_d[l], 's5_glu_w': s5_glu_w[l], 's5_glu_b': s5_glu_b[l],
              'cmp_pe': cmp_pe[l], 'cmp_w1': cmp_w1[l], 'cmp_w2': cmp_w2[l],
              'w_s5_out': w_s5_out[l], 'w_nsa_out': w_nsa_out[l], 'w_o': w_o[l]}
        xp, kvn, wn, hn = layer(xp, c_prompt, qpos_p, None, None, None, lp)
        kv_p.append(kvn)
        win_p.append(wn)
        ssm_p.append(hn)
        kv_past = cache_kv[page_table, l]
        kv_past = kv_past.reshape(kv_past.shape[0], past_len, 4, N_KV, HEAD_DIM)
        xs, kvn, wn, hn = layer(xs, c_sample, qpos_s, kv_past, state_win[:, l], state_ssm[:, l], lp)
        kv_s.append(kvn)
        win_s.append(wn)
        ssm_s.append(hn)
    y_prompt = rmsnorm(xp, final_g)
    y_sample = rmsnorm(xs, final_g)
    kv_prompt = jnp.stack(kv_p, axis=1)
    kv_sample = jnp.stack(kv_s, axis=1)
    win_prompt = jnp.stack(win_p, axis=1)
    win_sample = jnp.stack(win_s, axis=1)
    ssm_prompt = jnp.stack(ssm_p, axis=1)
    ssm_sample = jnp.stack(ssm_s, axis=1)
    return (y_prompt, y_sample, kv_prompt, kv_sample, win_prompt, win_sample, ssm_prompt, ssm_sample)
```

```python
import functools
import math

import numpy as np
import jax
import jax.numpy as jnp
from jax import lax
from jax.experimental import pallas as pl
from jax.experimental.pallas import tpu as pltpu

F32 = jnp.float32
BF16 = jnp.bfloat16

D_MODEL = 1024
DEPTH = 4
D_S5 = 512
S5_GROUP = 16
S5_GROUPS = 32
S5_STATE = 64
D_NSA = 512
HEAD_DIM = 64
N_HEADS = 8
N_KV = 2
GROUP_SIZE = 4
BLOCK = 64
N_SELECT = 16
WINDOW = 512
ROT_HALF = 8
ROPE_THETA = 500000.0
RMS_EPS = 1e-6
NEG_INF = -1e30
FORCED_SCORE = 1e4

LANES = 128
S5_CHUNK = 16
KV_W = N_KV * HEAD_DIM
N_PROJ = 4992
VMEM_LIMIT = 56 * 1024 * 1024


def _dot(a, b):
    return jnp.dot(a, b, preferred_element_type=F32)


def _dot_nt(a, b):
    return lax.dot_general(a, b, (((1,), (1,)), ((), ())), preferred_element_type=F32)


def _silu(x):
    return x * jax.nn.sigmoid(x)


def _rope128(x, c, sa, sb):
    return x * c + pltpu.roll(x, LANES - ROT_HALF, 1) * sa + pltpu.roll(x, ROT_HALF, 1) * sb


def _masked_exp(s, mask):
    s = jnp.where(mask, s, NEG_INF)
    m = jnp.max(s, axis=-1, keepdims=True)
    p = jnp.where(mask, jnp.exp(s - m), 0.0)
    return p, jnp.sum(p, axis=-1, keepdims=True)


def _safe_inv(l):
    return jnp.where(l > 0.0, 1.0 / l, 0.0)


def _cparams(sem):
    return pltpu.CompilerParams(dimension_semantics=sem, vmem_limit_bytes=VMEM_LIMIT)


def _ada_kernel(c_ref, w_ref, b_ref, o_ref):
    c = c_ref[...]
    o_ref[...] = _dot(_silu(c).astype(BF16), w_ref[...].astype(BF16)) + b_ref[...]


def _ada_mod(c_all, ada_w, ada_b):
    nc = c_all.shape[0]
    tn = 1024
    return pl.pallas_call(
        _ada_kernel,
        grid=(DEPTH, 3 * D_MODEL // tn),
        in_specs=[pl.BlockSpec((nc, D_MODEL), lambda l, n: (0, 0)),
                  pl.BlockSpec((None, D_MODEL, tn), lambda l, n: (l, 0, n)),
                  pl.BlockSpec((None, 1, tn), lambda l, n: (l, 0, n))],
        out_specs=pl.BlockSpec((None, nc, tn), lambda l, n: (l, 0, n)),
        out_shape=jax.ShapeDtypeStruct((DEPTH, nc, 3 * D_MODEL), F32),
        compiler_params=_cparams(("arbitrary", "arbitrary")),
        name="ada_mod",
    )(c_all, ada_w, ada_b.reshape(DEPTH, 1, 3 * D_MODEL))


def _inproj_kernel(x_ref, shift_ref, scale_ref, g_ref, w_ref, cos_ref, sa_ref, sb_ref,
                   u_ref, zs5_ref, q_ref, kv_ref, win_ref, znsa_ref, merge_ref, gate_ref):
    x = x_ref[...]
    h = x * lax.rsqrt(jnp.mean(x * x, axis=-1, keepdims=True) + RMS_EPS) * g_ref[...]
    h = h * (1.0 + scale_ref[...]) + shift_ref[...]
    hb = h.astype(BF16)

    def mm(lo, hi):
        return _dot(hb, w_ref[:, lo:hi])

    c, sa, sb = cos_ref[...], sa_ref[...], sb_ref[...]
    u_ref[...] = mm(0, 512)
    zs5_ref[...] = mm(512, 1024)
    for r in range(GROUP_SIZE):
        lo = 1024 + r * LANES
        q = _rope128(mm(lo, lo + LANES), c, sa, sb) * (HEAD_DIM ** -0.5)
        q_ref[:, r * LANES:(r + 1) * LANES] = q.astype(BF16)
    kv_ref[:, 0:256] = mm(1536, 1792)
    kv_ref[:, 256:384] = _rope128(mm(1792, 1920), c, sa, sb)
    kv_ref[:, 384:512] = mm(1920, 2048)
    win_ref[:, 0:128] = _rope128(mm(2048, 2176), c, sa, sb)
    win_ref[:, 128:256] = mm(2176, 2304)
    znsa_ref[...] = mm(2304, 2816)
    merge_ref[...] = mm(2816, 4864)
    gate_ref[...] = mm(4864, 4992)


def _inproj(x2d, shift, scale, g, wp, tabs, *, tm, tiles_per_b, per_row_mod):
    rows = x2d.shape[0]
    nt = rows // tm
    n_tab = tabs[0].shape[0] // tm
    if per_row_mod:
        mod_spec = pl.BlockSpec((tm, D_MODEL), lambda i: (i, 0))
    else:
        mod_spec = pl.BlockSpec((None, 1, D_MODEL), lambda i: (i // tiles_per_b, 0, 0))
    tab_spec = pl.BlockSpec((tm, LANES), lambda i: (i % n_tab, 0))

    def row_spec(w):
        return pl.BlockSpec((tm, w), lambda i: (i, 0))

    widths = (512, 512, 512, 512, 256, 512, 2048, 128)
    dtypes = (F32, F32, BF16, F32, F32, F32, F32, F32)
    return pl.pallas_call(
        _inproj_kernel,
        grid=(nt,),
        in_specs=[row_spec(D_MODEL), mod_spec, mod_spec,
                  pl.BlockSpec((1, D_MODEL), lambda i: (0, 0)),
                  pl.BlockSpec((D_MODEL, N_PROJ), lambda i: (0, 0)),
                  tab_spec, tab_spec, tab_spec],
        out_specs=[row_spec(w) for w in widths],
        out_shape=[jax.ShapeDtypeStruct((rows, w), d) for w, d in zip(widths, dtypes)],
        compiler_params=_cparams(("arbitrary",)),
        name="inproj",
    )(x2d, shift, scale, g, wp, *tabs)


def _compress(load_p, nbk, pe_ref, w1_ref, w2_ref, c, sa, sb):
    outs = []
    for s in range(2):
        acc = jnp.zeros((nbk, KV_W), F32)
        for p in range(BLOCK):
            xp = load_p(s, p) + pe_ref[s, p]
            acc = acc + _dot(xp.astype(BF16), w1_ref[s, p])
        outs.append(_dot(_silu(acc).astype(BF16), w2_ref[s]))
    return jnp.concatenate([_rope128(outs[0], c, sa, sb), outs[1]], axis=1)


def _cmp_prompt_kernel(k_ref, v_ref, pe_ref, w1_ref, w2_ref, cos_ref, sa_ref, sb_ref, o_ref, *, nbk):
    def load_p(s, p):
        return (k_ref, v_ref)[s][pl.ds(p, nbk, stride=BLOCK), :]

    res = _compress(load_p, nbk, pe_ref, w1_ref, w2_ref,
                    cos_ref[0:nbk, :], sa_ref[0:nbk, :], sb_ref[0:nbk, :])
    o_ref[...] = jnp.zeros(o_ref.shape, F32)
    o_ref[0:nbk, :] = res


def _cmp_prompt(kv2d, nb_batch, t, nbp, pe2, w1bd, w2bd, ctabs):
    nbk = t // BLOCK
    full = lambda shape: pl.BlockSpec(shape, lambda b: (0,) * len(shape))
    return pl.pallas_call(
        functools.partial(_cmp_prompt_kernel, nbk=nbk),
        grid=(nb_batch,),
        in_specs=[pl.BlockSpec((t, KV_W), lambda b: (b, 0)), pl.BlockSpec((t, KV_W), lambda b: (b, 1)),
                  full((2, BLOCK, 1, KV_W)), full((2, BLOCK, KV_W, KV_W)),
                  full((2, KV_W, KV_W)), full((nbp, LANES)), full((nbp, LANES)), full((nbp, LANES))],
        out_specs=pl.BlockSpec((None, nbp, 2 * KV_W), lambda b: (b, 0, 0)),
        out_shape=jax.ShapeDtypeStruct((nb_batch, nbp, 2 * KV_W), F32),
        compiler_params=_cparams(("arbitrary",)),
        name="cmp_prompt",
    )(kv2d, kv2d, pe2, w1bd, w2bd, *ctabs)


def _stack_queries(qt, tq):
    lane = lax.broadcasted_iota(jnp.int32, (tq, LANES), 1)
    qt = qt.astype(F32)
    blocks = []
    for g in range(N_KV):
        keep = (lane < HEAD_DIM) if g == 0 else (lane >= HEAD_DIM)
        for r in range(GROUP_SIZE):
            blocks.append(jnp.where(keep, qt[:, r * LANES:(r + 1) * LANES], 0.0))
    return jnp.concatenate(blocks, axis=0).astype(BF16)


def _select_blocks(imp, qblk, nb, n_sel):
    nq, nbp = imp.shape
    n = lax.broadcasted_iota(jnp.int32, (nq, nbp), 1)
    forced = (n == 0) | (n == qblk) | (n == qblk - 1)
    imp = jnp.where(forced, FORCED_SCORE, imp)
    imp = jnp.where(n > qblk, -1.0, imp)
    imp = jnp.where(n >= nb, -2.0, imp)
    rank = jnp.zeros((nq, nbp), F32)
    for m in range(nb):
        col = imp[:, m:m + 1]
        beats = (col > imp) | ((col == imp) & (n > m))
        rank = rank + jnp.where(beats, 1.0, 0.0)
    return jnp.where((rank < n_sel) & (imp > -0.5), 1.0, 0.0)


def _compressed_branch(qs, kcvc, tq_col, tq, nb):
    nbp = kcvc.shape[0]
    kc = kcvc[:, :KV_W].astype(BF16)
    vc = kcvc[:, KV_W:].astype(BF16)
    s = _dot_nt(qs, kc)
    n = lax.broadcasted_iota(jnp.int32, (1, nbp), 1)
    mask = (n * BLOCK + (BLOCK - 1) <= tq_col) & (n < nb)
    p, l = _masked_exp(s, mask)
    p = p * _safe_inv(l)
    o_c = _dot(p.astype(BF16), vc)
    imps = []
    for g in range(N_KV):
        acc = p[(g * GROUP_SIZE) * tq:(g * GROUP_SIZE + 1) * tq]
        for r in range(1, GROUP_SIZE):
            acc = acc + p[(g * GROUP_SIZE + r) * tq:(g * GROUP_SIZE + r + 1) * tq]
        imps.append(acc)
    return o_c, imps


def _block_mask(selstack, k0, tk):
    nbp = selstack.shape[1]
    kblk = (k0 + lax.broadcasted_iota(jnp.int32, (nbp, tk), 1)) // BLOCK
    e = jnp.where(kblk == lax.broadcasted_iota(jnp.int32, (nbp, tk), 0), 1.0, 0.0).astype(BF16)
    return _dot(selstack, e) > 0.5


def _gate_cols(gt, tq, branch):
    cols = []
    for g in range(N_KV):
        for r in range(GROUP_SIZE):
            j = branch * N_HEADS + g * GROUP_SIZE + r
            cols.append(gt[:, j:j + 1])
    return jnp.concatenate(cols, axis=0)


def _unstack_heads(o, tq):
    lane = lax.broadcasted_iota(jnp.int32, (tq, LANES), 1)
    chunks = []
    for r in range(GROUP_SIZE):
        a = o[r * tq:(r + 1) * tq]
        b = o[(GROUP_SIZE + r) * tq:(GROUP_SIZE + r + 1) * tq]
        chunks.append(jnp.where(lane < HEAD_DIM, a, b))
    return chunks


def _nsa_prompt_kernel(q_ref, kv_ref, win_ref, kcvc_ref, gate_ref, o_ref, m_sc, l_sc, acc_sc,
                       *, tq, tk, wk, nb, n_sel):
    i = pl.program_id(1)
    q0 = i * tq
    rows = N_HEADS * tq
    qs = _stack_queries(q_ref[...], tq)
    t_loc = lax.broadcasted_iota(jnp.int32, (rows, 1), 0) % tq
    tq_col = q0 + t_loc

    o_c, imps = _compressed_branch(qs, kcvc_ref[...], tq_col, tq, nb)

    qblk = (q0 + lax.broadcasted_iota(jnp.int32, (tq, 1), 0)) // BLOCK
    sels = [_select_blocks(imps[g], qblk, nb, n_sel).astype(BF16) for g in range(N_KV)]
    selstack = jnp.concatenate([sels[g] for g in range(N_KV) for _ in range(GROUP_SIZE)], axis=0)

    m_sc[...] = jnp.full(m_sc.shape, NEG_INF, F32)
    l_sc[...] = jnp.zeros(l_sc.shape, F32)
    acc_sc[...] = jnp.zeros(acc_sc.shape, F32)
    nkt = (q0 + tq + tk - 1) // tk

    def kv_step(j, carry):
        k0 = pl.multiple_of(j * tk, tk)
        ks = kv_ref[pl.ds(k0, tk), 0:KV_W].astype(BF16)
        vs = kv_ref[pl.ds(k0, tk), KV_W:2 * KV_W].astype(BF16)
        s = _dot_nt(qs, ks)
        kpos = k0 + lax.broadcasted_iota(jnp.int32, (1, tk), 1)
        mask = _block_mask(selstack, k0, tk) & (kpos <= tq_col)
        s = jnp.where(mask, s, NEG_INF)
        m_old = m_sc[...]
        m_new = jnp.maximum(m_old, jnp.max(s, axis=-1, keepdims=True))
        p = jnp.where(mask, jnp.exp(s - m_new), 0.0)
        alpha = jnp.exp(m_old - m_new)
        l_sc[...] = alpha * l_sc[...] + jnp.sum(p, axis=-1, keepdims=True)
        acc_sc[...] = alpha * acc_sc[...] + _dot(p.astype(BF16), vs)
        m_sc[...] = m_new
        return carry

    lax.fori_loop(0, nkt, kv_step, 0)
    o_s = acc_sc[...] * _safe_inv(l_sc[...])

    ks0 = pl.multiple_of(jnp.maximum(q0 + tq - wk, 0), tq)
    kw = win_ref[pl.ds(ks0, wk), 0:KV_W].astype(BF16)
    vw = win_ref[pl.ds(ks0, wk), KV_W:2 * KV_W].astype(BF16)
    s = _dot_nt(qs, kw)
    kpos = ks0 + lax.broadcasted_iota(jnp.int32, (1, wk), 1)
    p, l = _masked_exp(s, (kpos <= tq_col) & (kpos > tq_col - WINDOW))
    o_w = _dot(p.astype(BF16), vw) * _safe_inv(l)

    gt = jax.nn.sigmoid(gate_ref[...])
    o = _gate_cols(gt, tq, 0) * o_c + _gate_cols(gt, tq, 1) * o_s + _gate_cols(gt, tq, 2) * o_w
    for r, ch in enumerate(_unstack_heads(o, tq)):
        o_ref[:, r * LANES:(r + 1) * LANES] = ch


def _nsa_prompt(q2d, kv2d, win2d, kcvc, gate2d, nb_batch, t):
    tq = 128
    tk = 512 if t % 512 == 0 else 256
    wk = min(WINDOW + tq, t)
    nb = t // BLOCK
    nq = t // tq
    nbp = kcvc.shape[1]
    rows = N_HEADS * tq
    kern = functools.partial(_nsa_prompt_kernel, tq=tq, tk=tk, wk=wk, nb=nb, n_sel=min(N_SELECT, nb))
    return pl.pallas_call(
        kern,
        grid=(nb_batch, nq),
        in_specs=[pl.BlockSpec((tq, D_NSA), lambda b, i: (b * nq + i, 0)),
                  pl.BlockSpec((t, 2 * KV_W), lambda b, i: (b, 1)),
                  pl.BlockSpec((t, 2 * KV_W), lambda b, i: (b, 0)),
                  pl.BlockSpec((None, nbp, 2 * KV_W), lambda b, i: (b, 0, 0)),
                  pl.BlockSpec((tq, LANES), lambda b, i: (b * nq + i, 0))],
        out_specs=pl.BlockSpec((tq, D_NSA), lambda b, i: (b * nq + i, 0)),
        out_shape=jax.ShapeDtypeStruct((nb_batch * t, D_NSA), F32),
        scratch_shapes=[pltpu.VMEM((rows, 1), F32), pltpu.VMEM((rows, 1), F32),
                        pltpu.VMEM((rows, KV_W), F32)],
        compiler_params=_cparams(("arbitrary", "arbitrary")),
        name="nsa_prompt",
    )(q2d, kv2d, win2d, kcvc, gate2d)


def _page_copy(cache_ref, slabs, sem_ref, pt_ref, layer, b, p, slot, lane0, page, s):
    return pltpu.make_async_copy(
        cache_ref.at[pt_ref[b, p], layer, :, pl.ds(lane0 + s * KV_W, KV_W)],
        slabs[s].at[slot, pl.ds(p * page, page), :],
        sem_ref.at[slot])


def _gather_pages(cache_ref, slabs, sem_ref, pt_ref, layer, lane0, n_pages, page, new_ref):
    b = pl.program_id(0)
    nb_batch = pl.num_programs(0)
    slot = b % 2

    def copies(bb, sl):
        return [_page_copy(cache_ref, slabs, sem_ref, pt_ref, layer, bb, p, sl, lane0, page, s)
                for p in range(n_pages) for s in range(2)]

    @pl.when(b == 0)
    def _():
        for cp in copies(0, 0):
            cp.start()

    @pl.when(b + 1 < nb_batch)
    def _():
        for cp in copies(b + 1, 1 - slot):
            cp.start()

    for cp in copies(b, slot):
        cp.wait()
    past = n_pages * page
    for s in range(2):
        tail = slabs[s].shape[1] - past
        slabs[s][slot, pl.ds(past, tail), :] = jnp.zeros((tail, KV_W), F32)
        slabs[s][slot, pl.ds(past, 8), :] = new_ref[:, s * KV_W:(s + 1) * KV_W]
    return slot


def _cmp_sample_kernel(pt_ref, cache_ref, new_ref, pe_ref, w1_ref, w2_ref, cos_ref, sa_ref, sb_ref,
                       o_ref, slabk_ref, slabv_ref, sem_ref, *, layer, n_pages, page, nbk):
    slabs = (slabk_ref, slabv_ref)
    slot = _gather_pages(cache_ref, slabs, sem_ref, pt_ref, layer, 0, n_pages, page, new_ref)

    def load_p(s, p):
        return slabs[s][slot, pl.ds(p, nbk, stride=BLOCK), :]

    res = _compress(load_p, nbk, pe_ref, w1_ref, w2_ref,
                    cos_ref[0:nbk, :], sa_ref[0:nbk, :], sb_ref[0:nbk, :])
    o_ref[...] = jnp.zeros(o_ref.shape, F32)
    o_ref[0:nbk, :] = res


def _cmp_sample(page_table, cache4, newcmp, layer, nbp, pe2, w1bd, w2bd, ctabs, t_new):
    nb_batch, n_pages = page_table.shape
    page = cache4.shape[2]
    past = n_pages * page
    nbk = ((past + t_new + BLOCK - 1) // BLOCK + 7) // 8 * 8
    slab_rows = nbk * BLOCK
    full = lambda shape: pl.BlockSpec(shape, lambda b, pt: (0,) * len(shape))
    kern = functools.partial(_cmp_sample_kernel, layer=layer, n_pages=n_pages, page=page, nbk=nbk)
    return pl.pallas_call(
        kern,
        grid_spec=pltpu.PrefetchScalarGridSpec(
            num_scalar_prefetch=1,
            grid=(nb_batch,),
            in_specs=[pl.BlockSpec(memory_space=pl.ANY),
                      pl.BlockSpec((None, 8, 2 * KV_W), lambda b, pt: (b, 0, 0)),
                      full((2, BLOCK, 1, KV_W)), full((2, BLOCK, KV_W, KV_W)),
                      full((2, KV_W, KV_W)), full((nbp, LANES)), full((nbp, LANES)),
                      full((nbp, LANES))],
            out_specs=pl.BlockSpec((None, nbp, 2 * KV_W), lambda b, pt: (b, 0, 0)),
            scratch_shapes=[pltpu.VMEM((2, slab_rows, KV_W), F32), pltpu.VMEM((2, slab_rows, KV_W), F32),
                            pltpu.SemaphoreType.DMA((2,))]),
        out_shape=jax.ShapeDtypeStruct((nb_batch, nbp, 2 * KV_W), F32),
        compiler_params=_cparams(("arbitrary",)),
        name="cmp_sample",
    )(page_table, cache4, newcmp, pe2, w1bd, w2bd, *ctabs)


def _nsa_sample_kernel(pt_ref, cache_ref, q_ref, new_ref, wnew_ref, swin_ref, kcvc_ref, gate_ref,
                       o_ref, wout_ref, slabk_ref, slabv_ref, wslab_ref, sem_ref,
                       *, layer, n_pages, page, t_new, nb, n_sel):
    tq = 8
    past = n_pages * page
    rows = N_HEADS * tq
    slot = _gather_pages(cache_ref, (slabk_ref, slabv_ref), sem_ref, pt_ref, layer, 2 * KV_W,
                         n_pages, page, new_ref)
    nk = slabk_ref.shape[1]

    qs = _stack_queries(q_ref[...], tq)
    t_loc = lax.broadcasted_iota(jnp.int32, (rows, 1), 0) % tq
    tq_col = past + t_loc

    o_c, imps = _compressed_branch(qs, kcvc_ref[...], tq_col, tq, nb)
    qblk = (past + lax.broadcasted_iota(jnp.int32, (tq, 1), 0)) // BLOCK
    sels = [_select_blocks(imps[g], qblk, nb, n_sel).astype(BF16) for g in range(N_KV)]
    selstack = jnp.concatenate([sels[g] for g in range(N_KV) for _ in range(GROUP_SIZE)], axis=0)

    ks = slabk_ref[slot].astype(BF16)
    vs = slabv_ref[slot].astype(BF16)
    s = _dot_nt(qs, ks)
    kpos = lax.broadcasted_iota(jnp.int32, (1, nk), 1)
    p, l = _masked_exp(s, _block_mask(selstack, 0, nk) & (kpos <= tq_col))
    o_s = _dot(p.astype(BF16), vs) * _safe_inv(l)

    wbuf = swin_ref.shape[0]
    wk = wslab_ref.shape[0]
    wslab_ref[0:wbuf, :] = swin_ref[...]
    wslab_ref[wbuf:wk, :] = jnp.zeros((wk - wbuf, 2 * KV_W), F32)
    wslab_ref[wbuf:wbuf + 8, :] = wnew_ref[...]
    kw = wslab_ref[:, 0:KV_W].astype(BF16)
    vw = wslab_ref[:, KV_W:2 * KV_W].astype(BF16)
    s = _dot_nt(qs, kw)
    kpos = past - wbuf + lax.broadcasted_iota(jnp.int32, (1, wk), 1)
    p, l = _masked_exp(s, (kpos <= tq_col) & (kpos > tq_col - WINDOW) & (kpos >= 0))
    o_w = _dot(p.astype(BF16), vw) * _safe_inv(l)
    wout_ref[...] = wslab_ref[t_new:t_new + wbuf, :]

    gt = jax.nn.sigmoid(gate_ref[...])
    o = _gate_cols(gt, tq, 0) * o_c + _gate_cols(gt, tq, 1) * o_s + _gate_cols(gt, tq, 2) * o_w
    for r, ch in enumerate(_unstack_heads(o, tq)):
        o_ref[:, r * LANES:(r + 1) * LANES] = ch


def _nsa_sample(page_table, cache4, q3, newsel, wnew, state_win4, kcvc, gate3, layer, t_new):
    nb_batch, n_pages = page_table.shape
    page = cache4.shape[2]
    past = n_pages * page
    nb = (past + t_new + BLOCK - 1) // BLOCK
    nk = (past + t_new + LANES - 1) // LANES * LANES
    wbuf = state_win4.shape[2]
    wk = wbuf + LANES
    nbp = kcvc.shape[1]
    kern = functools.partial(_nsa_sample_kernel, layer=layer, n_pages=n_pages, page=page,
                             t_new=t_new, nb=nb, n_sel=min(N_SELECT, nb))
    per_b = lambda d1, d2: pl.BlockSpec((None, d1, d2), lambda b, pt: (b, 0, 0))
    return pl.pallas_call(
        kern,
        grid_spec=pltpu.PrefetchScalarGridSpec(
            num_scalar_prefetch=1,
            grid=(nb_batch,),
            in_specs=[pl.BlockSpec(memory_space=pl.ANY),
                      per_b(8, D_NSA), per_b(8, 2 * KV_W), per_b(8, 2 * KV_W),
                      pl.BlockSpec((None, None, wbuf, 2 * KV_W), lambda b, pt: (b, layer, 0, 0)),
                      per_b(nbp, 2 * KV_W), per_b(8, LANES)],
            out_specs=[per_b(8, D_NSA), per_b(wbuf, 2 * KV_W)],
            scratch_shapes=[pltpu.VMEM((2, nk, KV_W), F32), pltpu.VMEM((2, nk, KV_W), F32),
                            pltpu.VMEM((wk, 2 * KV_W), F32), pltpu.SemaphoreType.DMA((2,))]),
        out_shape=[jax.ShapeDtypeStruct((nb_batch, 8, D_NSA), F32),
                   jax.ShapeDtypeStruct((nb_batch, wbuf, 2 * KV_W), F32)],
        compiler_params=_cparams(("arbitrary",)),
        name="nsa_sample",
    )(page_table, cache4, q3, newsel, wnew, state_win4, kcvc, gate3)


def _s5_setup_kernel(are_ref, aim_ref, ldt_ref, bre_ref, bim_ref, cre_ref, cim_ref,
                     ckr_ref, cki_ref, bkr_ref, bki_ref, m_ref, pwr_ref, pwi_ref):
    nk = pwr_ref.shape[1]

    def body(g, carry):
        ar = are_ref[g]
        ai = aim_ref[g]
        dt = jnp.exp(ldt_ref[g])
        mag = jnp.exp(ar * dt)
        abr = mag * jnp.cos(ai * dt)
        abi = mag * jnp.sin(ai * dt)
        den = ar * ar + ai * ai
        nr = abr - 1.0
        e_re = (nr * ar + abi * ai) / den
        e_im = (abi * ar - nr * ai) / den
        bre = bre_ref[g]
        bim = bim_ref[g]
        bbr = e_re * bre - e_im * bim
        bbi = e_re * bim + e_im * bre
        kk = lax.broadcasted_iota(jnp.int32, (nk, S5_STATE), 0).astype(F32)
        pmag = jnp.exp(kk * (ar * dt))
        pwr = pmag * jnp.cos(kk * (ai * dt))
        pwi = pmag * jnp.sin(kk * (ai * dt))
        pwr_ref[g] = pwr
        pwi_ref[g] = pwi
        cre = cre_ref[g]
        cim = cim_ref[g]
        for k in range(S5_CHUNK + 1):
            wr = pwr[k:k + 1, :]
            wi = pwi[k:k + 1, :]
            ckr_ref[g, k * S5_GROUP:(k + 1) * S5_GROUP, :] = cre * wr - cim * wi
            cki_ref[g, k * S5_GROUP:(k + 1) * S5_GROUP, :] = -(cre * wi + cim * wr)
            if k < S5_CHUNK:
                bkr_ref[g, k * S5_GROUP:(k + 1) * S5_GROUP, :] = bbr * wr - bbi * wi
                bki_ref[g, k * S5_GROUP:(k + 1) * S5_GROUP, :] = bbr * wi + bbi * wr
        nl = S5_CHUNK * S5_GROUP
        hp = lax.Precision.HIGHEST
        m_ref[g] = (lax.dot_general(ckr_ref[g, 0:nl, :], bbr, (((1,), (1,)), ((), ())),
                                    precision=hp, preferred_element_type=F32)
                    + lax.dot_general(cki_ref[g, 0:nl, :], bbi, (((1,), (1,)), ((), ())),
                                      precision=hp, preferred_element_type=F32))
        return carry

    lax.fori_loop(0, S5_GROUPS, body, 0)


def _s5_setup(a_re, a_im, log_dt, b_re, b_im, c_re, c_im):
    g, p, c = S5_GROUPS, S5_STATE, S5_GROUP
    nk = 24
    outs = [(DEPTH, g, (S5_CHUNK + 1) * c, p), (DEPTH, g, (S5_CHUNK + 1) * c, p),
            (DEPTH, g, S5_CHUNK * c, p), (DEPTH, g, S5_CHUNK * c, p),
            (DEPTH, g, S5_CHUNK * c, c), (DEPTH, g, nk, p), (DEPTH, g, nk, p)]
    lay = lambda s: pl.BlockSpec((None,) + s[1:], lambda l: (l,) + (0,) * (len(s) - 1))
    ins = [a_re.reshape(DEPTH, g, 1, p), a_im.reshape(DEPTH, g, 1, p),
           jnp.broadcast_to(log_dt[:, :, None, None], (DEPTH, g, 1, p)),
           b_re.transpose(0, 1, 3, 2), b_im.transpose(0, 1, 3, 2), c_re, c_im]
    return pl.pallas_call(
        _s5_setup_kernel,
        grid=(DEPTH,),
        in_specs=[lay(x.shape) for x in ins],
        out_specs=[lay(s) for s in outs],
        out_shape=[jax.ShapeDtypeStruct(s, F32) for s in outs],
        compiler_params=_cparams(("arbitrary",)),
        name="s5_setup",
    )(*ins)


def _s5_chunk_kernel(u_ref, h0_ref, kmat_ref, smat_ref, ymat_ref, a1_ref, a2_ref,
                     y_ref, hl_ref, s_sc, hp_sc, *, gb, nj, nbt):
    for gi in range(gb):
        s_sc[gi] = _dot(u_ref[gi], smat_ref[gi])

    def step(j, hs):
        r0 = pl.multiple_of(j * nbt, nbt)
        new = []
        for gi in range(gb):
            h = hs[gi]
            hp_sc[gi, pl.ds(r0, nbt), :] = h
            new.append(a1_ref[gi] * h + a2_ref[gi] * pltpu.roll(h, S5_STATE, 1)
                       + s_sc[gi, pl.ds(r0, nbt), :])
        return tuple(new)

    hs = lax.fori_loop(0, nj, step, tuple(h0_ref[gi] for gi in range(gb)))
    for gi in range(gb):
        hl_ref[gi] = hs[gi]
        y_ref[gi] = _dot(u_ref[gi], kmat_ref[gi]) + _dot(hp_sc[gi].astype(BF16), ymat_ref[gi])


def _s5_chunk(u_g, h0_g, kmat, smat, ymat, a1, a2, nbt):
    g, rows, w = u_g.shape
    nj = rows // nbt
    gb = 8
    st = 2 * S5_STATE
    blk = lambda d1, d2: pl.BlockSpec((gb, d1, d2), lambda i: (i, 0, 0))
    kern = functools.partial(_s5_chunk_kernel, gb=gb, nj=nj, nbt=nbt)
    return pl.pallas_call(
        kern,
        grid=(g // gb,),
        in_specs=[blk(rows, w), blk(nbt, st), blk(w, w), blk(w, st), blk(st, w), blk(1, st), blk(1, st)],
        out_specs=[blk(rows, w), blk(nbt, st)],
        out_shape=[jax.ShapeDtypeStruct((g, rows, w), F32), jax.ShapeDtypeStruct((g, nbt, st), F32)],
        scratch_shapes=[pltpu.VMEM((gb, rows, st), F32), pltpu.VMEM((gb, rows, st), F32)],
        compiler_params=_cparams(("arbitrary",)),
        name="s5_chunk",
    )(u_g, h0_g, kmat, smat, ymat, a1, a2)


def _post_kernel(x_ref, gmod_ref, yssm_ref, u_ref, zs5_ref, o_ref, znsa_ref, merge_ref,
                 d_ref, gluw_ref, glub_ref, ws5_ref, wnsa_ref, wo_ref, xo_ref):
    y = yssm_ref[...] + d_ref[...] * u_ref[...]
    y = 0.5 * y * (1.0 + jnp.tanh(math.sqrt(2.0 / math.pi) * (y + 0.044715 * (y * y * y))))
    y = y * jax.nn.sigmoid(_dot(y.astype(BF16), gluw_ref[...]) + glub_ref[...])
    y = y * _silu(zs5_ref[...])
    b_s5 = _dot(y.astype(BF16), ws5_ref[...])
    b_nsa = _dot((o_ref[...] * _silu(znsa_ref[...])).astype(BF16), wnsa_ref[...])
    m = jax.nn.sigmoid(merge_ref[...])
    mix = m[:, :D_MODEL] * b_s5 + m[:, D_MODEL:] * b_nsa
    xo_ref[...] = x_ref[...] + gmod_ref[...] * _dot(mix.astype(BF16), wo_ref[...])


def _post(x2d, gmod, yssm, u, zs5, o, znsa, merge, d, gluw, glub, ws5, wnsa, wo,
          *, tm, tiles_per_b, per_row_mod):
    rows = x2d.shape[0]
    if per_row_mod:
        mod_spec = pl.BlockSpec((tm, D_MODEL), lambda i: (i, 0))
    else:
        mod_spec = pl.BlockSpec((None, 1, D_MODEL), lambda i: (i // tiles_per_b, 0, 0))
    row = lambda w: pl.BlockSpec((tm, w), lambda i: (i, 0))
    full = lambda a, b: pl.BlockSpec((a, b), lambda i: (0, 0))
    return pl.pallas_call(
        _post_kernel,
        grid=(rows // tm,),
        in_specs=[row(D_MODEL), mod_spec, row(512), row(512), row(512), row(512), row(512), row(2048),
                  full(1, 512), full(512, 512), full(1, 512), full(512, D_MODEL), full(512, D_MODEL),
                  full(D_MODEL, D_MODEL)],
        out_specs=row(D_MODEL),
        out_shape=jax.ShapeDtypeStruct((rows, D_MODEL), F32),
        compiler_params=_cparams(("arbitrary",)),
        name="post",
    )(x2d, gmod, yssm, u, zs5, o, znsa, merge, d, gluw, glub, ws5, wnsa, wo)


def _final_norm_kernel(x_ref, g_ref, o_ref):
    x = x_ref[...]
    o_ref[...] = x * lax.rsqrt(jnp.mean(x * x, axis=-1, keepdims=True) + RMS_EPS) * g_ref[...]


def _final_norm(x2d, g, tm):
    rows = x2d.shape[0]
    return pl.pallas_call(
        _final_norm_kernel,
        grid=(rows // tm,),
        in_specs=[pl.BlockSpec((tm, D_MODEL), lambda i: (i, 0)), pl.BlockSpec((1, D_MODEL), lambda i: (0, 0))],
        out_specs=pl.BlockSpec((tm, D_MODEL), lambda i: (i, 0)),
        out_shape=jax.ShapeDtypeStruct((rows, D_MODEL), F32),
        compiler_params=_cparams(("arbitrary",)),
        name="final_norm",
    )(x2d, g)


def _head_perm():
    idx = [HEAD_DIM * (GROUP_SIZE * g + r) + d
           for r in range(GROUP_SIZE) for g in range(N_KV) for d in range(HEAD_DIM)]
    return np.asarray(idx, np.int32)


def _rope_tables(pos, width=LANES):
    inv = ROPE_THETA ** (-jnp.arange(ROT_HALF, dtype=F32) / ROT_HALF)
    ang = pos.astype(F32)[:, None] * inv[None, :]
    cos, sin = jnp.cos(ang), jnp.sin(ang)
    n = pos.shape[0]
    one = jnp.ones((n, HEAD_DIM - 2 * ROT_HALF), F32)
    zero8 = jnp.zeros((n, ROT_HALF), F32)
    zero = jnp.zeros((n, HEAD_DIM - 2 * ROT_HALF), F32)
    c = jnp.concatenate([cos, cos, one], axis=1)
    sa = jnp.concatenate([-sin, zero8, zero], axis=1)
    sb = jnp.concatenate([zero8, sin, zero], axis=1)
    rep = width // HEAD_DIM
    return tuple(jnp.tile(t, (1, rep)) for t in (c, sa, sb))


def _blockdiag2(w):
    z = jnp.zeros_like(w)
    return jnp.concatenate([jnp.concatenate([w, z], -1), jnp.concatenate([z, w], -1)], -2)


def _s5_matrices(tabs, l, t_eff):
    ckr, cki, bkr, bki, m, pwr, pwi = [t[l] for t in tabs]
    g, c, L = S5_GROUPS, S5_GROUP, S5_CHUNK
    mk = m.reshape(g, L, c, c)
    s_idx = np.arange(L)[:, None]
    t_idx = np.arange(L)[None, :]
    lag = np.clip(t_idx - s_idx, 0, L - 1)
    kfull = mk[:, lag]
    kfull = jnp.where((t_idx >= s_idx)[None, :, :, None, None], kfull, 0.0)
    kmat = kfull.transpose(0, 1, 4, 2, 3).reshape(g, L * c, L * c).astype(BF16)
    ck = jnp.concatenate([ckr, cki], axis=-1)
    ymat = ck[:, c:, :].transpose(0, 2, 1).astype(BF16)
    bk = jnp.concatenate([bkr, bki], axis=-1).reshape(g, L, c, 2 * S5_STATE)
    sm = bk[:, t_eff - 1::-1] if t_eff == L else bk[:, np.arange(t_eff - 1, -1, -1)]
    if t_eff < L:
        sm = jnp.concatenate([sm, jnp.zeros((g, L - t_eff, c, 2 * S5_STATE), F32)], axis=1)
    smat = sm.reshape(g, L * c, 2 * S5_STATE).astype(BF16)
    ar = pwr[:, t_eff][:, None, :]
    ai = pwi[:, t_eff][:, None, :]
    a1 = jnp.concatenate([ar, ar], axis=-1)
    a2 = jnp.concatenate([-ai, ai], axis=-1)
    return kmat, smat, ymat, a1, a2


def _layer_weights(l, w_in, cmp_pe, cmp_w1, cmp_w2, s5_glu_w, w_s5_out, w_nsa_out, w_o, perm):
    wi = w_in[l]
    gate_w = jnp.pad(wi[:, 2304:2328], ((0, 0), (0, LANES - 3 * N_HEADS)))
    wp = jnp.concatenate([wi[:, :1024], wi[:, 1024:1536][:, perm], wi[:, 1536:2304],
                          wi[:, 2328:2840][:, perm], wi[:, 2840:], gate_w], axis=1).astype(BF16)
    pe = cmp_pe[l]
    pe2 = jnp.concatenate([pe, pe], axis=-1)[:, :, None, :]
    w1bd = _blockdiag2(cmp_w1[l].reshape(2, BLOCK, HEAD_DIM, HEAD_DIM)).astype(BF16)
    w2bd = _blockdiag2(cmp_w2[l]).astype(BF16)
    return dict(wp=wp, pe2=pe2, w1bd=w1bd, w2bd=w2bd, gluw=s5_glu_w[l].astype(BF16),
                ws5=w_s5_out[l].astype(BF16), wnsa=w_nsa_out[l][perm, :].astype(BF16),
                wo=w_o[l].astype(BF16))


def _to_groups(u2d, nbt, nj):
    u5 = u2d.reshape(nbt, nj, S5_CHUNK, S5_GROUPS, S5_GROUP)
    return u5.transpose(3, 1, 0, 2, 4).reshape(S5_GROUPS, nj * nbt, S5_CHUNK * S5_GROUP)


def _from_groups(y_g, nbt, nj):
    y5 = y_g.reshape(S5_GROUPS, nj, nbt, S5_CHUNK, S5_GROUP)
    return y5.transpose(2, 1, 3, 0, 4).reshape(nbt * nj * S5_CHUNK, D_S5)


def _state_to_groups(h):
    return h.transpose(2, 0, 1, 3).reshape(S5_GROUPS, h.shape[0], 2 * S5_STATE)


def _state_from_groups(hg):
    g, b, _ = hg.shape
    return hg.reshape(g, b, 2, S5_STATE).transpose(1, 2, 0, 3)


def kernel(x_prompt, x_sample, c_prompt, c_sample, cache_kv, page_table, state_win, state_ssm, ada_w, ada_b, norm_g, w_in, s5_a_re, s5_a_im, s5_log_dt, s5_b_re, s5_b_im, s5_c_re, s5_c_im, s5_d, s5_glu_w, s5_glu_b, cmp_pe, cmp_w1, cmp_w2, w_s5_out, w_nsa_out, w_o, final_g):
    bp, tp, _ = x_prompt.shape
    bs, ts, _ = x_sample.shape
    n_pool, _, page = cache_kv.shape[:3]
    n_pages = page_table.shape[1]
    past = n_pages * page
    wbuf = state_win.shape[2]
    assert tp % S5_CHUNK == 0 and tp % 128 == 0 and ts <= 8 and bs % 8 == 0 and bp % 8 == 0

    perm = _head_perm()
    mod = _ada_mod(jnp.concatenate([c_prompt, c_sample], axis=0), ada_w, ada_b)
    s5tabs = _s5_setup(s5_a_re, s5_a_im, s5_log_dt, s5_b_re, s5_b_im, s5_c_re, s5_c_im)

    tabs_p = _rope_tables(jnp.arange(tp))
    rs = bs * ts
    tabs_s = _rope_tables(jnp.tile(past + jnp.arange(ts), bs))
    nb_p = tp // BLOCK
    nbp_p = (nb_p + LANES - 1) // LANES * LANES
    ctabs_p = _rope_tables(jnp.arange(nbp_p) * BLOCK + (BLOCK - 1))
    nb_s = (past + ts + BLOCK - 1) // BLOCK
    nbp_s = ((nb_s + 7) // 8 * 8 + LANES - 1) // LANES * LANES
    ctabs_s = _rope_tables(jnp.arange(nbp_s) * BLOCK + (BLOCK - 1))

    cache4 = cache_kv.reshape(n_pool, DEPTH, page, 4 * KV_W)
    state_win4 = state_win.reshape(bs, DEPTH, wbuf, 2 * KV_W)
    tm_p = 256
    nj_p = tp // S5_CHUNK

    xp = x_prompt.reshape(bp * tp, D_MODEL)
    xs = x_sample.reshape(rs, D_MODEL)
    kv_p, kv_s, win_p, win_s, ssm_p, ssm_s = [], [], [], [], [], []

    def pad_rows(a3):
        return jnp.pad(a3, ((0, 0), (0, 8 - ts), (0, 0)))

    for l in range(DEPTH):
        lw = _layer_weights(l, w_in, cmp_pe, cmp_w1, cmp_w2, s5_glu_w, w_s5_out, w_nsa_out, w_o, perm)
        g_row = norm_g[l][None, :]
        d_row = s5_d[l].reshape(1, D_S5)
        glub = s5_glu_b[l][None, :]
        mp, ms = mod[l, :bp], mod[l, bp:]
        post_w = (d_row, lw['gluw'], glub, lw['ws5'], lw['wnsa'], lw['wo'])

        shift, scale, gmod = [mp[:, k * D_MODEL:(k + 1) * D_MODEL][:, None, :] for k in range(3)]
        u, zs5, q, kv, win, znsa, merge, gate = _inproj(
            xp, shift, scale, g_row, lw['wp'], tabs_p, tm=tm_p, tiles_per_b=tp // tm_p, per_row_mod=False)
        kcvc = _cmp_prompt(kv, bp, tp, nbp_p, lw['pe2'], lw['w1bd'], lw['w2bd'], ctabs_p)
        o = _nsa_prompt(q, kv, win, kcvc, gate, bp, tp)
        mats = _s5_matrices(s5tabs, l, S5_CHUNK)
        y_g, hl = _s5_chunk(_to_groups(u, bp, nj_p).astype(BF16),
                            jnp.zeros((S5_GROUPS, bp, 2 * S5_STATE), F32), *mats, nbt=bp)
        yssm = _from_groups(y_g, bp, nj_p)
        xp = _post(xp, gmod, yssm, u, zs5, o, znsa, merge, *post_w,
                   tm=tm_p, tiles_per_b=tp // tm_p, per_row_mod=False)
        kv_p.append(kv.reshape(bp, tp, 4, N_KV, HEAD_DIM))
        keep = min(WINDOW, tp)
        win_p.append(win.reshape(bp, tp, 2, N_KV, HEAD_DIM)[:, tp - keep:])
        ssm_p.append(_state_from_groups(hl))

        shift, scale, gmod = [jnp.repeat(ms[:, k * D_MODEL:(k + 1) * D_MODEL], ts, axis=0) for k in range(3)]
        u, zs5, q, kv, win, znsa, merge, gate = _inproj(
            xs, shift, scale, g_row, lw['wp'], tabs_s, tm=rs, tiles_per_b=1, per_row_mod=True)
        kv3 = kv.reshape(bs, ts, 4 * KV_W)
        kcvc = _cmp_sample(page_table, cache4, pad_rows(kv3[:, :, :2 * KV_W]), l, nbp_s,
                           lw['pe2'], lw['w1bd'], lw['w2bd'], ctabs_s, ts)
        o8, wst = _nsa_sample(page_table, cache4, pad_rows(q.reshape(bs, ts, D_NSA)),
                              pad_rows(kv3[:, :, 2 * KV_W:]), pad_rows(win.reshape(bs, ts, 2 * KV_W)),
                              state_win4, kcvc, pad_rows(gate.reshape(bs, ts, LANES)), l, ts)
        o = o8[:, :ts].reshape(rs, D_NSA)
        mats = _s5_matrices(s5tabs, l, ts)
        u_pad = jnp.pad(u.reshape(bs, ts, D_S5), ((0, 0), (0, S5_CHUNK - ts), (0, 0)))
        y_g, hl = _s5_chunk(_to_groups(u_pad.reshape(bs * S5_CHUNK, D_S5), bs, 1).astype(BF16),
                            _state_to_groups(state_ssm[:, l]), *mats, nbt=bs)
        yssm = _from_groups(y_g, bs, 1).reshape(bs, S5_CHUNK, D_S5)[:, :ts].reshape(rs, D_S5)
        xs = _post(xs, gmod, yssm, u, zs5, o, znsa, merge, *post_w,
                   tm=rs, tiles_per_b=1, per_row_mod=True)
        kv_s.append(kv.reshape(bs, ts, 4, N_KV, HEAD_DIM))
        win_s.append(wst.reshape(bs, wbuf, 2, N_KV, HEAD_DIM))
        ssm_s.append(_state_from_groups(hl))

    fg = final_g[None, :]
    y_prompt = _final_norm(xp, fg, tm_p).reshape(bp, tp, D_MODEL)
    y_sample = _final_norm(xs, fg, rs).reshape(bs, ts, D_MODEL)
    return (y_prompt, y_sample, jnp.stack(kv_p, axis=1), jnp.stack(kv_s, axis=1),
            jnp.stack(win_p, axis=1), jnp.stack(win_s, axis=1),
            jnp.stack(ssm_p, axis=1), jnp.stack(ssm_s, axis=1))
```

```python
import functools
import math

import numpy as np
import jax
import jax.numpy as jnp
from jax import lax
from jax.experimental import pallas as pl
from jax.experimental.pallas import tpu as pltpu

F32 = jnp.float32
BF16 = jnp.bfloat16

D_MODEL = 1024
DEPTH = 4
D_S5 = 512
S5_GROUP = 16
S5_GROUPS = 32
S5_STATE = 64
D_NSA = 512
HEAD_DIM = 64
N_HEADS = 8
N_KV = 2
GROUP_SIZE = 4
BLOCK = 64
N_SELECT = 16
WINDOW = 512
ROT_HALF = 8
ROPE_THETA = 500000.0
RMS_EPS = 1e-6
NEG_INF = -1e30
FORCED_SCORE = 1e4

LANES = 128
S5_CHUNK = 16
KV_W = N_KV * HEAD_DIM
N_PROJ = 4992
VMEM_LIMIT = 56 * 1024 * 1024


def _dot(a, b):
    return jnp.dot(a, b, preferred_element_type=F32)


def _dot_nt(a, b):
    return lax.dot_general(a, b, (((1,), (1,)), ((), ())), preferred_element_type=F32)


def _silu(x):
    return x * jax.nn.sigmoid(x)


def _rope128(x, c, sa, sb):
    return x * c + pltpu.roll(x, LANES - ROT_HALF, 1) * sa + pltpu.roll(x, ROT_HALF, 1) * sb


def _masked_exp(s, mask):
    s = jnp.where(mask, s, NEG_INF)
    m = jnp.max(s, axis=-1, keepdims=True)
    p = jnp.where(mask, jnp.exp(s - m), 0.0)
    return p, jnp.sum(p, axis=-1, keepdims=True)


def _safe_inv(l):
    return jnp.where(l > 0.0, 1.0 / l, 0.0)


def _cparams(sem):
    return pltpu.CompilerParams(dimension_semantics=sem, vmem_limit_bytes=VMEM_LIMIT)


def _ada_kernel(c_ref, w_ref, b_ref, o_ref):
    c = c_ref[...]
    o_ref[...] = _dot(_silu(c).astype(BF16), w_ref[...].astype(BF16)) + b_ref[...]


def _ada_mod(c_all, ada_w, ada_b):
    nc = c_all.shape[0]
    tn = 1024
    return pl.pallas_call(
        _ada_kernel,
        grid=(DEPTH, 3 * D_MODEL // tn),
        in_specs=[pl.BlockSpec((nc, D_MODEL), lambda l, n: (0, 0)),
                  pl.BlockSpec((None, D_MODEL, tn), lambda l, n: (l, 0, n)),
                  pl.BlockSpec((None, 1, tn), lambda l, n: (l, 0, n))],
        out_specs=pl.BlockSpec((None, nc, tn), lambda l, n: (l, 0, n)),
        out_shape=jax.ShapeDtypeStruct((DEPTH, nc, 3 * D_MODEL), F32),
        compiler_params=_cparams(("arbitrary", "arbitrary")),
        name="ada_mod",
    )(c_all, ada_w, ada_b.reshape(DEPTH, 1, 3 * D_MODEL))


def _inproj_kernel(x_ref, shift_ref, scale_ref, g_ref, w_ref, cos_ref, sa_ref, sb_ref,
                   u_ref, zs5_ref, q_ref, kv_ref, win_ref, znsa_ref, merge_ref, gate_ref):
    x = x_ref[...]
    h = x * lax.rsqrt(jnp.mean(x * x, axis=-1, keepdims=True) + RMS_EPS) * g_ref[...]
    h = h * (1.0 + scale_ref[...]) + shift_ref[...]
    hb = h.astype(BF16)

    def mm(lo, hi):
        return _dot(hb, w_ref[:, lo:hi])

    c, sa, sb = cos_ref[...], sa_ref[...], sb_ref[...]
    u_ref[...] = mm(0, 512)
    zs5_ref[...] = mm(512, 1024)
    for r in range(GROUP_SIZE):
        lo = 1024 + r * LANES
        q = _rope128(mm(lo, lo + LANES), c, sa, sb) * (HEAD_DIM ** -0.5)
        q_ref[:, r * LANES:(r + 1) * LANES] = q.astype(BF16)
    kv_ref[:, 0:256] = mm(1536, 1792)
    kv_ref[:, 256:384] = _rope128(mm(1792, 1920), c, sa, sb)
    kv_ref[:, 384:512] = mm(1920, 2048)
    win_ref[:, 0:128] = _rope128(mm(2048, 2176), c, sa, sb)
    win_ref[:, 128:256] = mm(2176, 2304)
    znsa_ref[...] = mm(2304, 2816)
    merge_ref[...] = mm(2816, 4864)
    gate_ref[...] = mm(4864, 4992)


def _inproj(x2d, shift, scale, g, wp, tabs, *, tm, tiles_per_b, per_row_mod):
    rows = x2d.shape[0]
    nt = rows // tm
    n_tab = tabs[0].shape[0] // tm
    if per_row_mod:
        mod_spec = pl.BlockSpec((tm, D_MODEL), lambda i: (i, 0))
    else:
        mod_spec = pl.BlockSpec((None, 1, D_MODEL), lambda i: (i // tiles_per_b, 0, 0))
    tab_spec = pl.BlockSpec((tm, LANES), lambda i: (i % n_tab, 0))

    def row_spec(w):
        return pl.BlockSpec((tm, w), lambda i: (i, 0))

    widths = (512, 512, 512, 512, 256, 512, 2048, 128)
    dtypes = (F32, F32, BF16, F32, F32, F32, F32, F32)
    return pl.pallas_call(
        _inproj_kernel,
        grid=(nt,),
        in_specs=[row_spec(D_MODEL), mod_spec, mod_spec,
                  pl.BlockSpec((1, D_MODEL), lambda i: (0, 0)),
                  pl.BlockSpec((D_MODEL, N_PROJ), lambda i: (0, 0)),
                  tab_spec, tab_spec, tab_spec],
        out_specs=[row_spec(w) for w in widths],
        out_shape=[jax.ShapeDtypeStruct((rows, w), d) for w, d in zip(widths, dtypes)],
        compiler_params=_cparams(("arbitrary",)),
        name="inproj",
    )(x2d, shift, scale, g, wp, *tabs)


def _rope128_t(x, c, sa, sb):
    return x * c + pltpu.roll(x, KV_W - ROT_HALF, 0) * sa + pltpu.roll(x, ROT_HALF, 0) * sb


def _inproj_prompt_kernel(x_ref, shift_ref, scale_ref, g_ref, w_ref, wt_ref, cos_ref, sa_ref, sb_ref,
                          cost_ref, sat_ref, sbt_ref,
                          u_ref, zs5_ref, q_ref, kvt_ref, wint_ref, cmp_ref, znsa_ref, merge_ref, gate_ref):
    x = x_ref[...]
    h = x * lax.rsqrt(jnp.mean(x * x, axis=-1, keepdims=True) + RMS_EPS) * g_ref[...]
    h = h * (1.0 + scale_ref[...]) + shift_ref[...]
    hb = h.astype(BF16)

    def mm(lo, hi):
        return _dot(hb, w_ref[:, lo:hi])

    c, sa, sb = cos_ref[...], sa_ref[...], sb_ref[...]
    u_ref[...] = mm(0, 512)
    zs5_ref[...] = mm(512, 1024)
    for r in range(GROUP_SIZE):
        lo = 1024 + r * LANES
        q = _rope128(mm(lo, lo + LANES), c, sa, sb) * (HEAD_DIM ** -0.5)
        q_ref[:, r * LANES:(r + 1) * LANES] = q.astype(BF16)
    cmp_ref[...] = mm(1536, 1792)
    znsa_ref[...] = mm(2304, 2816)
    merge_ref[...] = mm(2816, 4864)
    gate_ref[...] = mm(4864, 4992)

    ct, sat, sbt = cost_ref[...], sat_ref[...], sbt_ref[...]

    def mmt(s):
        return _dot_nt(wt_ref[s * KV_W:(s + 1) * KV_W, :], hb)

    kvt_ref[0:KV_W, :] = mmt(0)
    kvt_ref[KV_W:2 * KV_W, :] = mmt(1)
    kvt_ref[2 * KV_W:3 * KV_W, :] = _rope128_t(mmt(2), ct, sat, sbt)
    kvt_ref[3 * KV_W:4 * KV_W, :] = mmt(3)
    wint_ref[0:KV_W, :] = _rope128_t(mmt(4), ct, sat, sbt)
    wint_ref[KV_W:2 * KV_W, :] = mmt(5)


def _inproj_prompt(x2d, shift, scale, g, wp, wt, tabs, tabs_t, *, tm, nb_batch, t):
    rows = x2d.shape[0]
    tpb = t // tm
    mod_spec = pl.BlockSpec((None, 1, D_MODEL), lambda i: (i // tpb, 0, 0))
    tab_spec = pl.BlockSpec((tm, LANES), lambda i: (i % tpb, 0))
    tabt_spec = pl.BlockSpec((KV_W, tm), lambda i: (0, i % tpb))
    row = lambda w: pl.BlockSpec((tm, w), lambda i: (i, 0))
    tr = lambda h: pl.BlockSpec((None, h, tm), lambda i: (i // tpb, 0, i % tpb))
    outs = [((rows, 512), F32, row(512)), ((rows, 512), F32, row(512)), ((rows, 512), BF16, row(512)),
            ((nb_batch, 4 * KV_W, t), F32, tr(4 * KV_W)), ((nb_batch, 2 * KV_W, t), F32, tr(2 * KV_W)),
            ((rows, 256), F32, row(256)), ((rows, 512), F32, row(512)), ((rows, 2048), F32, row(2048)),
            ((rows, 128), F32, row(128))]
    return pl.pallas_call(
        _inproj_prompt_kernel,
        grid=(rows // tm,),
        in_specs=[row(D_MODEL), mod_spec, mod_spec,
                  pl.BlockSpec((1, D_MODEL), lambda i: (0, 0)),
                  pl.BlockSpec((D_MODEL, N_PROJ), lambda i: (0, 0)),
                  pl.BlockSpec((6 * KV_W, D_MODEL), lambda i: (0, 0)),
                  tab_spec, tab_spec, tab_spec, tabt_spec, tabt_spec, tabt_spec],
        out_specs=[o[2] for o in outs],
        out_shape=[jax.ShapeDtypeStruct(o[0], o[1]) for o in outs],
        compiler_params=_cparams(("arbitrary",)),
        name="inproj_prompt",
    )(x2d, shift, scale, g, wp, wt, *tabs, *tabs_t)


def _compress(load_p, nbk, pe_ref, w1_ref, w2_ref, c, sa, sb):
    outs = []
    for s in range(2):
        acc = jnp.zeros((nbk, KV_W), F32)
        for p in range(BLOCK):
            xp = load_p(s, p) + pe_ref[s, p]
            acc = acc + _dot(xp.astype(BF16), w1_ref[s, p])
        outs.append(_dot(_silu(acc).astype(BF16), w2_ref[s]))
    return jnp.concatenate([_rope128(outs[0], c, sa, sb), outs[1]], axis=1)


def _cmp_prompt_kernel(k_ref, v_ref, pe_ref, w1_ref, w2_ref, cos_ref, sa_ref, sb_ref, o_ref, *, nbk):
    def load_p(s, p):
        return (k_ref, v_ref)[s][pl.ds(p, nbk, stride=BLOCK), :]

    res = _compress(load_p, nbk, pe_ref, w1_ref, w2_ref,
                    cos_ref[0:nbk, :], sa_ref[0:nbk, :], sb_ref[0:nbk, :])
    o_ref[...] = jnp.zeros(o_ref.shape, F32)
    o_ref[0:nbk, :] = res


def _cmp_prompt(kv2d, nb_batch, t, nbp, pe2, w1bd, w2bd, ctabs):
    nbk = t // BLOCK
    full = lambda shape: pl.BlockSpec(shape, lambda b: (0,) * len(shape))
    return pl.pallas_call(
        functools.partial(_cmp_prompt_kernel, nbk=nbk),
        grid=(nb_batch,),
        in_specs=[pl.BlockSpec((t, KV_W), lambda b: (b, 0)), pl.BlockSpec((t, KV_W), lambda b: (b, 1)),
                  full((2, BLOCK, 1, KV_W)), full((2, BLOCK, KV_W, KV_W)),
                  full((2, KV_W, KV_W)), full((nbp, LANES)), full((nbp, LANES)), full((nbp, LANES))],
        out_specs=pl.BlockSpec((None, nbp, 2 * KV_W), lambda b: (b, 0, 0)),
        out_shape=jax.ShapeDtypeStruct((nb_batch, nbp, 2 * KV_W), F32),
        compiler_params=_cparams(("arbitrary",)),
        name="cmp_prompt",
    )(kv2d, kv2d, pe2, w1bd, w2bd, *ctabs)


def _stack_queries(qt, tq):
    lane = lax.broadcasted_iota(jnp.int32, (tq, LANES), 1)
    qt = qt.astype(F32)
    blocks = []
    for g in range(N_KV):
        keep = (lane < HEAD_DIM) if g == 0 else (lane >= HEAD_DIM)
        for r in range(GROUP_SIZE):
            blocks.append(jnp.where(keep, qt[:, r * LANES:(r + 1) * LANES], 0.0))
    return jnp.concatenate(blocks, axis=0).astype(BF16)


def _select_blocks(imp, qblk, nb, n_sel):
    nq, nbp = imp.shape
    n = lax.broadcasted_iota(jnp.int32, (nq, nbp), 1)
    forced = (n == 0) | (n == qblk) | (n == qblk - 1)
    imp = jnp.where(forced, FORCED_SCORE, imp)
    imp = jnp.where(n > qblk, -1.0, imp)
    imp = jnp.where(n >= nb, -2.0, imp)
    rank = jnp.zeros((nq, nbp), F32)
    for m in range(nb):
        col = imp[:, m:m + 1]
        beats = (col > imp) | ((col == imp) & (n > m))
        rank = rank + jnp.where(beats, 1.0, 0.0)
    return jnp.where((rank < n_sel) & (imp > -0.5), 1.0, 0.0)


def _compressed_branch(qs, kcvc, tq_col, tq, nb):
    nbp = kcvc.shape[0]
    kc = kcvc[:, :KV_W].astype(BF16)
    vc = kcvc[:, KV_W:].astype(BF16)
    s = _dot_nt(qs, kc)
    n = lax.broadcasted_iota(jnp.int32, (1, nbp), 1)
    mask = (n * BLOCK + (BLOCK - 1) <= tq_col) & (n < nb)
    p, l = _masked_exp(s, mask)
    p = p * _safe_inv(l)
    o_c = _dot(p.astype(BF16), vc)
    imps = []
    for g in range(N_KV):
        acc = p[(g * GROUP_SIZE) * tq:(g * GROUP_SIZE + 1) * tq]
        for r in range(1, GROUP_SIZE):
            acc = acc + p[(g * GROUP_SIZE + r) * tq:(g * GROUP_SIZE + r + 1) * tq]
        imps.append(acc)
    return o_c, imps


def _block_mask(selstack, k0, tk):
    nbp = selstack.shape[1]
    kblk = (k0 + lax.broadcasted_iota(jnp.int32, (nbp, tk), 1)) // BLOCK
    e = jnp.where(kblk == lax.broadcasted_iota(jnp.int32, (nbp, tk), 0), 1.0, 0.0).astype(BF16)
    return _dot(selstack, e) > 0.5


def _gate_cols(gt, tq, branch):
    cols = []
    for g in range(N_KV):
        for r in range(GROUP_SIZE):
            j = branch * N_HEADS + g * GROUP_SIZE + r
            cols.append(gt[:, j:j + 1])
    return jnp.concatenate(cols, axis=0)


def _unstack_heads(o, tq):
    lane = lax.broadcasted_iota(jnp.int32, (tq, LANES), 1)
    chunks = []
    for r in range(GROUP_SIZE):
        a = o[r * tq:(r + 1) * tq]
        b = o[(GROUP_SIZE + r) * tq:(GROUP_SIZE + r + 1) * tq]
        chunks.append(jnp.where(lane < HEAD_DIM, a, b))
    return chunks


def _select_blocks_t(imp, qblk_row, nb, n_sel):
    nq, nbp = imp.shape
    nbr = (nb + 7) // 8 * 8
    x = imp.T[0:nbr, :]
    n = lax.broadcasted_iota(jnp.int32, (nbr, nq), 0)
    forced = (n == 0) | (n == qblk_row) | (n == qblk_row - 1)
    x = jnp.where(forced, FORCED_SCORE, x)
    x = jnp.where(n > qblk_row, -1.0, x)
    x = jnp.where(n >= nb, -2.0, x)
    rank = jnp.zeros((nbr, nq), F32)
    for m in range(nb):
        row = x[m:m + 1, :]
        beats = (row > x) | ((row == x) & (n > m))
        rank = rank + jnp.where(beats, 1.0, 0.0)
    sel = jnp.where((rank < n_sel) & (x > -0.5), 1.0, 0.0)
    if nbr < nbp:
        sel = jnp.concatenate([sel, jnp.zeros((nbp - nbr, nq), F32)], axis=0)
    return sel.T


def _nsa_prompt_kernel(q_ref, kvt_ref, wint_ref, kcvc_ref, gate_ref, e_ref, cbias_ref, wbias_ref,
                       o_ref, kaug_sc, vsel_sc, waug_sc, vwin_sc, s_sc, m_sc, l_sc, acc_sc, sel_sc,
                       *, tq, nb, n_sel):
    i = pl.program_id(1)
    q0 = pl.multiple_of(i * tq, tq)
    rows = N_HEADS * tq
    wk = WINDOW + tq
    tk = 2 * LANES
    neg = jnp.asarray(NEG_INF, F32)

    @pl.when(i == 0)
    def _():
        kaug_sc[0:KV_W, :] = kvt_ref[0:KV_W, :].astype(BF16)
        kaug_sc[KV_W:2 * KV_W, :] = e_ref[...]
        vsel_sc[...] = kvt_ref[KV_W:2 * KV_W, :].astype(BF16)
        waug_sc[...] = jnp.zeros(waug_sc.shape, BF16)
        waug_sc[0:KV_W, WINDOW:] = wint_ref[0:KV_W, :].astype(BF16)
        waug_sc[KV_W:KV_W + 16, 0:WINDOW] = jnp.ones((16, WINDOW), BF16)
        vwin_sc[:, 0:WINDOW] = jnp.zeros((KV_W, WINDOW), BF16)
        vwin_sc[:, WINDOW:] = wint_ref[KV_W:2 * KV_W, :].astype(BF16)

    qs = _stack_queries(q_ref[...], tq)
    t_loc = lax.broadcasted_iota(jnp.int32, (rows, 1), 0) % tq
    tq_col = q0 + t_loc

    o_c, imps = _compressed_branch(qs, kcvc_ref[...], tq_col, tq, nb)

    sel_sc[...] = jnp.ones(sel_sc.shape, F32)

    @pl.when(q0 + tq > n_sel * BLOCK)
    def _():
        qblk_row = (q0 + lax.broadcasted_iota(jnp.int32, (1, tq), 1)) // BLOCK
        for g in range(N_KV):
            sel_sc[g] = _select_blocks_t(imps[g], qblk_row, nb, n_sel)

    selb = [((sel_sc[g] - 1.0) * (-NEG_INF)).astype(BF16) for g in range(N_KV)]
    selstack = jnp.concatenate([selb[g] for g in range(N_KV) for _ in range(GROUP_SIZE)], axis=0)
    qaug = jnp.concatenate([qs, selstack], axis=1)

    n_full = q0 // tk
    par = (q0 // tq) % 2
    m_sc[...] = jnp.full(m_sc.shape, neg, F32)

    def score_tile(k0, bias):
        s = _dot(qaug, kaug_sc[:, pl.ds(k0, tk)])
        if bias is not None:
            s = s + bias
        s_sc[:, pl.ds(k0, tk)] = s
        m_sc[...] = jnp.maximum(m_sc[...], jnp.maximum(s[:, :LANES], s[:, LANES:]))

    def pass1(j, carry):
        score_tile(pl.multiple_of(j * tk, tk), None)
        return carry

    lax.fori_loop(0, n_full, pass1, 0)
    score_tile(pl.multiple_of(n_full * tk, tk), cbias_ref[par])
    m_b = jnp.broadcast_to(jnp.max(m_sc[...], axis=-1, keepdims=True), (rows, LANES))

    l_sc[...] = jnp.zeros(l_sc.shape, F32)
    acc_sc[...] = jnp.zeros(acc_sc.shape, F32)

    def pass2(j, carry):
        k0 = pl.multiple_of(j * tk, tk)
        s = s_sc[:, pl.ds(k0, tk)]
        p0 = jnp.exp(s[:, :LANES] - m_b)
        p1 = jnp.exp(s[:, LANES:] - m_b)
        l_sc[...] = l_sc[...] + (p0 + p1)
        p = jnp.concatenate([p0, p1], axis=1).astype(BF16)
        acc_sc[...] = acc_sc[...] + _dot_nt(p, vsel_sc[:, pl.ds(k0, tk)])
        return carry

    lax.fori_loop(0, n_full + 1, pass2, 0)
    o_s = acc_sc[...] * _safe_inv(jnp.sum(l_sc[...], axis=-1, keepdims=True))

    lane = lax.broadcasted_iota(jnp.int32, (rows, LANES), 1)
    padcol = jnp.where(lane == 0, neg, 0.0).astype(BF16)
    qaug_w = jnp.concatenate([qs, padcol], axis=1)
    s = _dot(qaug_w, waug_sc[:, pl.ds(q0, wk)]) + wbias_ref[...]
    p = jnp.exp(s - jnp.max(s, axis=-1, keepdims=True))
    l = jnp.sum(p, axis=-1, keepdims=True)
    o_w = _dot_nt(p.astype(BF16), vwin_sc[:, pl.ds(q0, wk)]) * _safe_inv(l)

    gt = jax.nn.sigmoid(gate_ref[...])
    o = _gate_cols(gt, tq, 0) * o_c + _gate_cols(gt, tq, 1) * o_s + _gate_cols(gt, tq, 2) * o_w
    for r, ch in enumerate(_unstack_heads(o, tq)):
        o_ref[:, r * LANES:(r + 1) * LANES] = ch


def _attn_constants(t, tq):
    rows = N_HEADS * tq
    t_loc = np.arange(rows)[:, None] % tq
    k = np.arange(2 * LANES)[None, :]
    cb = np.stack([np.where(k <= t_loc, 0.0, NEG_INF), np.where(k - LANES <= t_loc, 0.0, NEG_INF)])
    kr = np.arange(WINDOW + tq)[None, :]
    wb = np.where((kr - WINDOW <= t_loc) & (kr > t_loc), 0.0, NEG_INF)
    e = (np.arange(t)[None, :] // BLOCK == np.arange(LANES)[:, None]).astype(np.float32)
    return jnp.asarray(e, BF16), jnp.asarray(cb, F32), jnp.asarray(wb, F32)


def _nsa_prompt(q2d, kvt, wint, kcvc, gate2d, consts, nb_batch, t, tq):
    nb = t // BLOCK
    nq = t // tq
    nbp = kcvc.shape[1]
    rows = N_HEADS * tq
    wk = WINDOW + tq
    assert nbp == LANES and t % (2 * LANES) == 0
    e, cb, wb = consts
    kern = functools.partial(_nsa_prompt_kernel, tq=tq, nb=nb, n_sel=min(N_SELECT, nb))
    return pl.pallas_call(
        kern,
        grid=(nb_batch, nq),
        in_specs=[pl.BlockSpec((tq, D_NSA), lambda b, i: (b * nq + i, 0)),
                  pl.BlockSpec((None, 2 * KV_W, t), lambda b, i: (b, 1, 0)),
                  pl.BlockSpec((None, 2 * KV_W, t), lambda b, i: (b, 0, 0)),
                  pl.BlockSpec((None, nbp, 2 * KV_W), lambda b, i: (b, 0, 0)),
                  pl.BlockSpec((tq, LANES), lambda b, i: (b * nq + i, 0)),
                  pl.BlockSpec((LANES, t), lambda b, i: (0, 0)),
                  pl.BlockSpec((2, rows, 2 * LANES), lambda b, i: (0, 0, 0)),
                  pl.BlockSpec((rows, wk), lambda b, i: (0, 0))],
        out_specs=pl.BlockSpec((tq, D_NSA), lambda b, i: (b * nq + i, 0)),
        out_shape=jax.ShapeDtypeStruct((nb_batch * t, D_NSA), F32),
        scratch_shapes=[pltpu.VMEM((2 * KV_W, t), BF16), pltpu.VMEM((KV_W, t), BF16),
                        pltpu.VMEM((2 * KV_W, WINDOW + t), BF16), pltpu.VMEM((KV_W, WINDOW + t), BF16),
                        pltpu.VMEM((rows, t), F32), pltpu.VMEM((rows, LANES), F32),
                        pltpu.VMEM((rows, LANES), F32), pltpu.VMEM((rows, KV_W), F32),
                        pltpu.VMEM((N_KV, tq, LANES), F32)],
        compiler_params=_cparams(("arbitrary", "arbitrary")),
        name="nsa_prompt",
    )(q2d, kvt, wint, kcvc, gate2d, e, cb, wb)


def _page_copy(cache_ref, slabs, sem_ref, pt_ref, layer, b, p, slot, s0, page, s):
    return pltpu.make_async_copy(
        cache_ref.at[pt_ref[b, p], layer, s0 + s],
        slabs[s].at[slot, :, pl.ds(p * page, page)],
        sem_ref.at[slot])


def _gather_pages(cache_ref, slabs, sem_ref, pt_ref, layer, s0, n_pages, page):
    b = pl.program_id(0)
    nb_batch = pl.num_programs(0)
    slot = b % 2

    def copies(bb, sl):
        return [_page_copy(cache_ref, slabs, sem_ref, pt_ref, layer, bb, p, sl, s0, page, s)
                for p in range(n_pages) for s in range(2)]

    @pl.when(b == 0)
    def _():
        for cp in copies(0, 0):
            cp.start()

    @pl.when(b + 1 < nb_batch)
    def _():
        for cp in copies(b + 1, 1 - slot):
            cp.start()

    for cp in copies(b, slot):
        cp.wait()
    return slot


def _cmp_sample_kernel(pt_ref, cache_ref, new_ref, pe_ref, w1_ref, w2_ref, cos_ref, sa_ref, sb_ref,
                       o_ref, stagek_ref, stagev_ref, rowk_ref, rowv_ref, sem_ref,
                       *, layer, n_pages, page, nbk):
    stages = (stagek_ref, stagev_ref)
    rowm = (rowk_ref, rowv_ref)
    slot = _gather_pages(cache_ref, stages, sem_ref, pt_ref, layer, 0, n_pages, page)
    past = n_pages * page
    for s in range(2):
        def xpose(p, carry):
            r0 = pl.multiple_of(p * page, page)
            rowm[s][pl.ds(r0, page), :] = stages[s][slot, :, pl.ds(r0, page)].T
            return carry

        lax.fori_loop(0, n_pages, xpose, 0)
        tail = rowm[s].shape[0] - past
        rowm[s][pl.ds(past, tail), :] = jnp.zeros((tail, KV_W), F32)
        rowm[s][pl.ds(past, 8), :] = new_ref[:, s * KV_W:(s + 1) * KV_W]

    def load_p(s, p):
        return rowm[s][pl.ds(p, nbk, stride=BLOCK), :]

    res = _compress(load_p, nbk, pe_ref, w1_ref, w2_ref,
                    cos_ref[0:nbk, :], sa_ref[0:nbk, :], sb_ref[0:nbk, :])
    o_ref[...] = jnp.zeros(o_ref.shape, F32)
    o_ref[0:nbk, :] = res


def _cmp_sample(page_table, cache5, newcmp, layer, nbp, pe2, w1bd, w2bd, ctabs, t_new):
    nb_batch, n_pages = page_table.shape
    page = cache5.shape[4]
    past = n_pages * page
    nbk = ((past + t_new + BLOCK - 1) // BLOCK + 7) // 8 * 8
    slab_rows = nbk * BLOCK
    full = lambda shape: pl.BlockSpec(shape, lambda b, pt: (0,) * len(shape))
    kern = functools.partial(_cmp_sample_kernel, layer=layer, n_pages=n_pages, page=page, nbk=nbk)
    return pl.pallas_call(
        kern,
        grid_spec=pltpu.PrefetchScalarGridSpec(
            num_scalar_prefetch=1,
            grid=(nb_batch,),
            in_specs=[pl.BlockSpec(memory_space=pl.ANY),
                      pl.BlockSpec((None, 8, 2 * KV_W), lambda b, pt: (b, 0, 0)),
                      full((2, BLOCK, 1, KV_W)), full((2, BLOCK, KV_W, KV_W)),
                      full((2, KV_W, KV_W)), full((nbp, LANES)), full((nbp, LANES)),
                      full((nbp, LANES))],
            out_specs=pl.BlockSpec((None, nbp, 2 * KV_W), lambda b, pt: (b, 0, 0)),
            scratch_shapes=[pltpu.VMEM((2, KV_W, past), F32), pltpu.VMEM((2, KV_W, past), F32),
                            pltpu.VMEM((slab_rows, KV_W), F32), pltpu.VMEM((slab_rows, KV_W), F32),
                            pltpu.SemaphoreType.DMA((2,))]),
        out_shape=jax.ShapeDtypeStruct((nb_batch, nbp, 2 * KV_W), F32),
        compiler_params=_cparams(("arbitrary",)),
        name="cmp_sample",
    )(page_table, cache5, newcmp, pe2, w1bd, w2bd, *ctabs)


def _nsa_sample_kernel(pt_ref, cache_ref, q_ref, newt_ref, wnewt_ref, swint_ref, kcvc_ref, gate_ref,
                       o_ref, wout_ref, slabk_ref, slabv_ref, wslab_ref, sem_ref,
                       *, layer, n_pages, page, t_new, nb, n_sel):
    tq = 8
    past = n_pages * page
    rows = N_HEADS * tq
    slabs = (slabk_ref, slabv_ref)
    slot = _gather_pages(cache_ref, slabs, sem_ref, pt_ref, layer, 2, n_pages, page)
    nk = slabk_ref.shape[2]
    for s in range(2):
        slabs[s][slot, :, pl.ds(past, LANES)] = newt_ref[s]

    qs = _stack_queries(q_ref[...], tq)
    t_loc = lax.broadcasted_iota(jnp.int32, (rows, 1), 0) % tq
    tq_col = past + t_loc

    o_c, imps = _compressed_branch(qs, kcvc_ref[...], tq_col, tq, nb)
    qblk = (past + lax.broadcasted_iota(jnp.int32, (tq, 1), 0)) // BLOCK
    sels = [_select_blocks(imps[g], qblk, nb, n_sel).astype(BF16) for g in range(N_KV)]
    selstack = jnp.concatenate([sels[g] for g in range(N_KV) for _ in range(GROUP_SIZE)], axis=0)

    s = _dot(qs, slabk_ref[slot].astype(BF16))
    kpos = lax.broadcasted_iota(jnp.int32, (1, nk), 1)
    p, l = _masked_exp(s, _block_mask(selstack, 0, nk) & (kpos <= tq_col))
    o_s = _dot_nt(p.astype(BF16), slabv_ref[slot].astype(BF16)) * _safe_inv(l)

    wbuf = swint_ref.shape[2]
    wk = wslab_ref.shape[2]
    wslab_ref[:, :, 0:wbuf] = swint_ref[...]
    wslab_ref[:, :, wbuf:wk] = wnewt_ref[...]
    s = _dot(qs, wslab_ref[0].astype(BF16))
    kpos = past - wbuf + lax.broadcasted_iota(jnp.int32, (1, wk), 1)
    p, l = _masked_exp(s, (kpos <= tq_col) & (kpos > tq_col - WINDOW) & (kpos >= 0))
    o_w = _dot_nt(p.astype(BF16), wslab_ref[1].astype(BF16)) * _safe_inv(l)
    wout_ref[...] = wslab_ref[:, :, t_new:t_new + wbuf]

    gt = jax.nn.sigmoid(gate_ref[...])
    o = _gate_cols(gt, tq, 0) * o_c + _gate_cols(gt, tq, 1) * o_s + _gate_cols(gt, tq, 2) * o_w
    for r, ch in enumerate(_unstack_heads(o, tq)):
        o_ref[:, r * LANES:(r + 1) * LANES] = ch


def _nsa_sample(page_table, cache5, q3, newselt, wnewt, state_wint, kcvc, gate3, layer, t_new):
    nb_batch, n_pages = page_table.shape
    page = cache5.shape[4]
    past = n_pages * page
    nb = (past + t_new + BLOCK - 1) // BLOCK
    nk = past + LANES
    wbuf = state_wint.shape[4]
    wk = wbuf + LANES
    nbp = kcvc.shape[1]
    kern = functools.partial(_nsa_sample_kernel, layer=layer, n_pages=n_pages, page=page,
                             t_new=t_new, nb=nb, n_sel=min(N_SELECT, nb))
    per_b = lambda d1, d2: pl.BlockSpec((None, d1, d2), lambda b, pt: (b, 0, 0))
    per_b4 = lambda d1, d2, d3: pl.BlockSpec((None, d1, d2, d3), lambda b, pt: (b, 0, 0, 0))
    return pl.pallas_call(
        kern,
        grid_spec=pltpu.PrefetchScalarGridSpec(
            num_scalar_prefetch=1,
            grid=(nb_batch,),
            in_specs=[pl.BlockSpec(memory_space=pl.ANY),
                      per_b(8, D_NSA), per_b4(2, KV_W, LANES), per_b4(2, KV_W, LANES),
                      pl.BlockSpec((None, None, 2, KV_W, wbuf), lambda b, pt: (b, layer, 0, 0, 0)),
                      per_b(nbp, 2 * KV_W), per_b(8, LANES)],
            out_specs=[per_b(8, D_NSA), per_b4(2, KV_W, wbuf)],
            scratch_shapes=[pltpu.VMEM((2, KV_W, nk), F32), pltpu.VMEM((2, KV_W, nk), F32),
                            pltpu.VMEM((2, KV_W, wk), F32), pltpu.SemaphoreType.DMA((2,))]),
        out_shape=[jax.ShapeDtypeStruct((nb_batch, 8, D_NSA), F32),
                   jax.ShapeDtypeStruct((nb_batch, 2, KV_W, wbuf), F32)],
        compiler_params=_cparams(("arbitrary",)),
        name="nsa_sample",
    )(page_table, cache5, q3, newselt, wnewt, state_wint, kcvc, gate3)


def _s5_setup_kernel(are_ref, aim_ref, ldt_ref, bre_ref, bim_ref, cre_ref, cim_ref,
                     ckr_ref, cki_ref, bkr_ref, bki_ref, m_ref, pwr_ref, pwi_ref):
    nk = pwr_ref.shape[1]

    def body(g, carry):
        ar = are_ref[g]
        ai = aim_ref[g]
        dt = jnp.exp(ldt_ref[g])
        mag = jnp.exp(ar * dt)
        abr = mag * jnp.cos(ai * dt)
        abi = mag * jnp.sin(ai * dt)
        den = ar * ar + ai * ai
        nr = abr - 1.0
        e_re = (nr * ar + abi * ai) / den
        e_im = (abi * ar - nr * ai) / den
        bre = bre_ref[g]
        bim = bim_ref[g]
        bbr = e_re * bre - e_im * bim
        bbi = e_re * bim + e_im * bre
        kk = lax.broadcasted_iota(jnp.int32, (nk, S5_STATE), 0).astype(F32)
        pmag = jnp.exp(kk * (ar * dt))
        pwr = pmag * jnp.cos(kk * (ai * dt))
        pwi = pmag * jnp.sin(kk * (ai * dt))
        pwr_ref[g] = pwr
        pwi_ref[g] = pwi
        cre = cre_ref[g]
        cim = cim_ref[g]
        for k in range(S5_CHUNK + 1):
            wr = pwr[k:k + 1, :]
            wi = pwi[k:k + 1, :]
            ckr_ref[g, k * S5_GROUP:(k + 1) * S5_GROUP, :] = cre * wr - cim * wi
            cki_ref[g, k * S5_GROUP:(k + 1) * S5_GROUP, :] = -(cre * wi + cim * wr)
            if k < S5_CHUNK:
                bkr_ref[g, k * S5_GROUP:(k + 1) * S5_GROUP, :] = bbr * wr - bbi * wi
                bki_ref[g, k * S5_GROUP:(k + 1) * S5_GROUP, :] = bbr * wi + bbi * wr
        nl = S5_CHUNK * S5_GROUP
        hp = lax.Precision.HIGHEST
        m_ref[g] = (lax.dot_general(ckr_ref[g, 0:nl, :], bbr, (((1,), (1,)), ((), ())),
                                    precision=hp, preferred_element_type=F32)
                    + lax.dot_general(cki_ref[g, 0:nl, :], bbi, (((1,), (1,)), ((), ())),
                                      precision=hp, preferred_element_type=F32))
        return carry

    lax.fori_loop(0, S5_GROUPS, body, 0)


def _s5_setup(a_re, a_im, log_dt, b_re, b_im, c_re, c_im):
    g, p, c = S5_GROUPS, S5_STATE, S5_GROUP
    nk = 24
    outs = [(DEPTH, g, (S5_CHUNK + 1) * c, p), (DEPTH, g, (S5_CHUNK + 1) * c, p),
            (DEPTH, g, S5_CHUNK * c, p), (DEPTH, g, S5_CHUNK * c, p),
            (DEPTH, g, S5_CHUNK * c, c), (DEPTH, g, nk, p), (DEPTH, g, nk, p)]
    lay = lambda s: pl.BlockSpec((None,) + s[1:], lambda l: (l,) + (0,) * (len(s) - 1))
    ins = [a_re.reshape(DEPTH, g, 1, p), a_im.reshape(DEPTH, g, 1, p),
           jnp.broadcast_to(log_dt[:, :, None, None], (DEPTH, g, 1, p)),
           b_re.transpose(0, 1, 3, 2), b_im.transpose(0, 1, 3, 2), c_re, c_im]
    return pl.pallas_call(
        _s5_setup_kernel,
        grid=(DEPTH,),
        in_specs=[lay(x.shape) for x in ins],
        out_specs=[lay(s) for s in outs],
        out_shape=[jax.ShapeDtypeStruct(s, F32) for s in outs],
        compiler_params=_cparams(("arbitrary",)),
        name="s5_setup",
    )(*ins)


def _s5_chunk_kernel(u_ref, h0_ref, kmat_ref, smat_ref, ymat_ref, a1_ref, a2_ref,
                     y_ref, hl_ref, s_sc, hp_sc, *, gb, nj, nbt):
    for gi in range(gb):
        s_sc[gi] = _dot(u_ref[gi], smat_ref[gi])

    def step(j, hs):
        r0 = pl.multiple_of(j * nbt, nbt)
        new = []
        for gi in range(gb):
            h = hs[gi]
            hp_sc[gi, pl.ds(r0, nbt), :] = h
            new.append(a1_ref[gi] * h + a2_ref[gi] * pltpu.roll(h, S5_STATE, 1)
                       + s_sc[gi, pl.ds(r0, nbt), :])
        return tuple(new)

    hs = lax.fori_loop(0, nj, step, tuple(h0_ref[gi] for gi in range(gb)))
    for gi in range(gb):
        hl_ref[gi] = hs[gi]
        y_ref[gi] = _dot(u_ref[gi], kmat_ref[gi]) + _dot(hp_sc[gi].astype(BF16), ymat_ref[gi])


def _s5_chunk(u_g, h0_g, kmat, smat, ymat, a1, a2, nbt):
    g, rows, w = u_g.shape
    nj = rows // nbt
    gb = 8
    st = 2 * S5_STATE
    blk = lambda d1, d2: pl.BlockSpec((gb, d1, d2), lambda i: (i, 0, 0))
    kern = functools.partial(_s5_chunk_kernel, gb=gb, nj=nj, nbt=nbt)
    return pl.pallas_call(
        kern,
        grid=(g // gb,),
        in_specs=[blk(rows, w), blk(nbt, st), blk(w, w), blk(w, st), blk(st, w), blk(1, st), blk(1, st)],
        out_specs=[blk(rows, w), blk(nbt, st)],
        out_shape=[jax.ShapeDtypeStruct((g, rows, w), F32), jax.ShapeDtypeStruct((g, nbt, st), F32)],
        scratch_shapes=[pltpu.VMEM((gb, rows, st), F32), pltpu.VMEM((gb, rows, st), F32)],
        compiler_params=_cparams(("arbitrary",)),
        name="s5_chunk",
    )(u_g, h0_g, kmat, smat, ymat, a1, a2)


def _post_kernel(x_ref, gmod_ref, yssm_ref, u_ref, zs5_ref, o_ref, znsa_ref, merge_ref,
                 d_ref, gluw_ref, glub_ref, ws5_ref, wnsa_ref, wo_ref, xo_ref):
    y = yssm_ref[...] + d_ref[...] * u_ref[...]
    y = 0.5 * y * (1.0 + jnp.tanh(math.sqrt(2.0 / math.pi) * (y + 0.044715 * (y * y * y))))
    y = y * jax.nn.sigmoid(_dot(y.astype(BF16), gluw_ref[...]) + glub_ref[...])
    y = y * _silu(zs5_ref[...])
    b_s5 = _dot(y.astype(BF16), ws5_ref[...])
    b_nsa = _dot((o_ref[...] * _silu(znsa_ref[...])).astype(BF16), wnsa_ref[...])
    m = jax.nn.sigmoid(merge_ref[...])
    mix = m[:, :D_MODEL] * b_s5 + m[:, D_MODEL:] * b_nsa
    xo_ref[...] = x_ref[...] + gmod_ref[...] * _dot(mix.astype(BF16), wo_ref[...])


def _post(x2d, gmod, yssm, u, zs5, o, znsa, merge, d, gluw, glub, ws5, wnsa, wo,
          *, tm, tiles_per_b, per_row_mod):
    rows = x2d.shape[0]
    if per_row_mod:
        mod_spec = pl.BlockSpec((tm, D_MODEL), lambda i: (i, 0))
    else:
        mod_spec = pl.BlockSpec((None, 1, D_MODEL), lambda i: (i // tiles_per_b, 0, 0))
    row = lambda w: pl.BlockSpec((tm, w), lambda i: (i, 0))
    full = lambda a, b: pl.BlockSpec((a, b), lambda i: (0, 0))
    return pl.pallas_call(
        _post_kernel,
        grid=(rows // tm,),
        in_specs=[row(D_MODEL), mod_spec, row(512), row(512), row(512), row(512), row(512), row(2048),
                  full(1, 512), full(512, 512), full(1, 512), full(512, D_MODEL), full(512, D_MODEL),
                  full(D_MODEL, D_MODEL)],
        out_specs=row(D_MODEL),
        out_shape=jax.ShapeDtypeStruct((rows, D_MODEL), F32),
        compiler_params=_cparams(("arbitrary",)),
        name="post",
    )(x2d, gmod, yssm, u, zs5, o, znsa, merge, d, gluw, glub, ws5, wnsa, wo)


def _final_norm_kernel(x_ref, g_ref, o_ref):
    x = x_ref[...]
    o_ref[...] = x * lax.rsqrt(jnp.mean(x * x, axis=-1, keepdims=True) + RMS_EPS) * g_ref[...]


def _final_norm(x2d, g, tm):
    rows = x2d.shape[0]
    return pl.pallas_call(
        _final_norm_kernel,
        grid=(rows // tm,),
        in_specs=[pl.BlockSpec((tm, D_MODEL), lambda i: (i, 0)), pl.BlockSpec((1, D_MODEL), lambda i: (0, 0))],
        out_specs=pl.BlockSpec((tm, D_MODEL), lambda i: (i, 0)),
        out_shape=jax.ShapeDtypeStruct((rows, D_MODEL), F32),
        compiler_params=_cparams(("arbitrary",)),
        name="final_norm",
    )(x2d, g)


def _head_perm():
    idx = [HEAD_DIM * (GROUP_SIZE * g + r) + d
           for r in range(GROUP_SIZE) for g in range(N_KV) for d in range(HEAD_DIM)]
    return np.asarray(idx, np.int32)


def _rope_tables(pos, width=LANES):
    inv = ROPE_THETA ** (-jnp.arange(ROT_HALF, dtype=F32) / ROT_HALF)
    ang = pos.astype(F32)[:, None] * inv[None, :]
    cos, sin = jnp.cos(ang), jnp.sin(ang)
    n = pos.shape[0]
    one = jnp.ones((n, HEAD_DIM - 2 * ROT_HALF), F32)
    zero8 = jnp.zeros((n, ROT_HALF), F32)
    zero = jnp.zeros((n, HEAD_DIM - 2 * ROT_HALF), F32)
    c = jnp.concatenate([cos, cos, one], axis=1)
    sa = jnp.concatenate([-sin, zero8, zero], axis=1)
    sb = jnp.concatenate([zero8, sin, zero], axis=1)
    rep = width // HEAD_DIM
    return tuple(jnp.tile(t, (1, rep)) for t in (c, sa, sb))


def _blockdiag2(w):
    z = jnp.zeros_like(w)
    return jnp.concatenate([jnp.concatenate([w, z], -1), jnp.concatenate([z, w], -1)], -2)


def _s5_matrices(tabs, l, t_eff):
    ckr, cki, bkr, bki, m, pwr, pwi = [t[l] for t in tabs]
    g, c, L = S5_GROUPS, S5_GROUP, S5_CHUNK
    mk = m.reshape(g, L, c, c)
    s_idx = np.arange(L)[:, None]
    t_idx = np.arange(L)[None, :]
    lag = np.clip(t_idx - s_idx, 0, L - 1)
    kfull = mk[:, lag]
    kfull = jnp.where((t_idx >= s_idx)[None, :, :, None, None], kfull, 0.0)
    kmat = kfull.transpose(0, 1, 4, 2, 3).reshape(g, L * c, L * c).astype(BF16)
    ck = jnp.concatenate([ckr, cki], axis=-1)
    ymat = ck[:, c:, :].transpose(0, 2, 1).astype(BF16)
    bk = jnp.concatenate([bkr, bki], axis=-1).reshape(g, L, c, 2 * S5_STATE)
    sm = bk[:, t_eff - 1::-1] if t_eff == L else bk[:, np.arange(t_eff - 1, -1, -1)]
    if t_eff < L:
        sm = jnp.concatenate([sm, jnp.zeros((g, L - t_eff, c, 2 * S5_STATE), F32)], axis=1)
    smat = sm.reshape(g, L * c, 2 * S5_STATE).astype(BF16)
    ar = pwr[:, t_eff][:, None, :]
    ai = pwi[:, t_eff][:, None, :]
    a1 = jnp.concatenate([ar, ar], axis=-1)
    a2 = jnp.concatenate([-ai, ai], axis=-1)
    return kmat, smat, ymat, a1, a2


def _layer_weights(l, w_in, cmp_pe, cmp_w1, cmp_w2, s5_glu_w, w_s5_out, w_nsa_out, w_o, perm):
    wi = w_in[l]
    gate_w = jnp.pad(wi[:, 2304:2328], ((0, 0), (0, LANES - 3 * N_HEADS)))
    wp = jnp.concatenate([wi[:, :1024], wi[:, 1024:1536][:, perm], wi[:, 1536:2304],
                          wi[:, 2328:2840][:, perm], wi[:, 2840:], gate_w], axis=1).astype(BF16)
    pe = cmp_pe[l]
    pe2 = jnp.concatenate([pe, pe], axis=-1)[:, :, None, :]
    w1bd = _blockdiag2(cmp_w1[l].reshape(2, BLOCK, HEAD_DIM, HEAD_DIM)).astype(BF16)
    w2bd = _blockdiag2(cmp_w2[l]).astype(BF16)
    wt = wi[:, 1536:2304].T.astype(BF16)
    return dict(wp=wp, wt=wt, pe2=pe2, w1bd=w1bd, w2bd=w2bd, gluw=s5_glu_w[l].astype(BF16),
                ws5=w_s5_out[l].astype(BF16), wnsa=w_nsa_out[l][perm, :].astype(BF16),
                wo=w_o[l].astype(BF16))


def _to_groups(u2d, nbt, nj):
    u5 = u2d.reshape(nbt, nj, S5_CHUNK, S5_GROUPS, S5_GROUP)
    return u5.transpose(3, 1, 0, 2, 4).reshape(S5_GROUPS, nj * nbt, S5_CHUNK * S5_GROUP)


def _from_groups(y_g, nbt, nj):
    y5 = y_g.reshape(S5_GROUPS, nj, nbt, S5_CHUNK, S5_GROUP)
    return y5.transpose(2, 1, 3, 0, 4).reshape(nbt * nj * S5_CHUNK, D_S5)


def _state_to_groups(h):
    return h.transpose(2, 0, 1, 3).reshape(S5_GROUPS, h.shape[0], 2 * S5_STATE)


def _state_from_groups(hg):
    g, b, _ = hg.shape
    return hg.reshape(g, b, 2, S5_STATE).transpose(1, 2, 0, 3)


def kernel(x_prompt, x_sample, c_prompt, c_sample, cache_kv, page_table, state_win, state_ssm, ada_w, ada_b, norm_g, w_in, s5_a_re, s5_a_im, s5_log_dt, s5_b_re, s5_b_im, s5_c_re, s5_c_im, s5_d, s5_glu_w, s5_glu_b, cmp_pe, cmp_w1, cmp_w2, w_s5_out, w_nsa_out, w_o, final_g):
    bp, tp, _ = x_prompt.shape
    bs, ts, _ = x_sample.shape
    n_pool, _, page = cache_kv.shape[:3]
    n_pages = page_table.shape[1]
    past = n_pages * page
    wbuf = state_win.shape[2]
    assert tp % S5_CHUNK == 0 and tp % 128 == 0 and ts <= 8 and bs % 8 == 0 and bp % 8 == 0

    perm = _head_perm()
    mod = _ada_mod(jnp.concatenate([c_prompt, c_sample], axis=0), ada_w, ada_b)
    s5tabs = _s5_setup(s5_a_re, s5_a_im, s5_log_dt, s5_b_re, s5_b_im, s5_c_re, s5_c_im)

    tabs_p = _rope_tables(jnp.arange(tp))
    rs = bs * ts
    tabs_s = _rope_tables(jnp.tile(past + jnp.arange(ts), bs))
    nb_p = tp // BLOCK
    nbp_p = (nb_p + LANES - 1) // LANES * LANES
    ctabs_p = _rope_tables(jnp.arange(nbp_p) * BLOCK + (BLOCK - 1))
    nb_s = (past + ts + BLOCK - 1) // BLOCK
    nbp_s = ((nb_s + 7) // 8 * 8 + LANES - 1) // LANES * LANES
    ctabs_s = _rope_tables(jnp.arange(nbp_s) * BLOCK + (BLOCK - 1))

    cache5 = cache_kv.transpose(0, 1, 3, 4, 5, 2).reshape(n_pool, DEPTH, 4, KV_W, page)
    state_wint = state_win.transpose(0, 1, 3, 4, 5, 2).reshape(bs, DEPTH, 2, KV_W, wbuf)
    tabs_pt = tuple(tb.T for tb in tabs_p)
    tm_p = 256
    tq_p = 128
    nj_p = tp // S5_CHUNK
    consts = _attn_constants(tp, tq_p)
    keep = min(WINDOW, tp)

    xp = x_prompt.reshape(bp * tp, D_MODEL)
    xs = x_sample.reshape(rs, D_MODEL)
    kv_p, kv_s, win_p, win_s, ssm_p, ssm_s = [], [], [], [], [], []

    def pad_rows(a3):
        return jnp.pad(a3, ((0, 0), (0, 8 - ts), (0, 0)))

    def new_tiles(a3):
        a4 = a3.reshape(bs, ts, 2, KV_W).transpose(0, 2, 3, 1)
        return jnp.pad(a4, ((0, 0), (0, 0), (0, 0), (0, LANES - ts)))

    for l in range(DEPTH):
        lw = _layer_weights(l, w_in, cmp_pe, cmp_w1, cmp_w2, s5_glu_w, w_s5_out, w_nsa_out, w_o, perm)
        g_row = norm_g[l][None, :]
        d_row = s5_d[l].reshape(1, D_S5)
        glub = s5_glu_b[l][None, :]
        mp, ms = mod[l, :bp], mod[l, bp:]
        post_w = (d_row, lw['gluw'], glub, lw['ws5'], lw['wnsa'], lw['wo'])

        shift, scale, gmod = [mp[:, k * D_MODEL:(k + 1) * D_MODEL][:, None, :] for k in range(3)]
        u, zs5, q, kvt, wint, cmp, znsa, merge, gate = _inproj_prompt(
            xp, shift, scale, g_row, lw['wp'], lw['wt'], tabs_p, tabs_pt, tm=tm_p, nb_batch=bp, t=tp)
        kcvc = _cmp_prompt(cmp, bp, tp, nbp_p, lw['pe2'], lw['w1bd'], lw['w2bd'], ctabs_p)
        o = _nsa_prompt(q, kvt, wint, kcvc, gate, consts, bp, tp, tq_p)
        mats = _s5_matrices(s5tabs, l, S5_CHUNK)
        y_g, hl = _s5_chunk(_to_groups(u, bp, nj_p).astype(BF16),
                            jnp.zeros((S5_GROUPS, bp, 2 * S5_STATE), F32), *mats, nbt=bp)
        yssm = _from_groups(y_g, bp, nj_p)
        xp = _post(xp, gmod, yssm, u, zs5, o, znsa, merge, *post_w,
                   tm=tm_p, tiles_per_b=tp // tm_p, per_row_mod=False)
        kv_p.append(kvt)
        win_p.append(wint[:, :, tp - keep:])
        ssm_p.append(_state_from_groups(hl))

        shift, scale, gmod = [jnp.repeat(ms[:, k * D_MODEL:(k + 1) * D_MODEL], ts, axis=0) for k in range(3)]
        u, zs5, q, kv, win, znsa, merge, gate = _inproj(
            xs, shift, scale, g_row, lw['wp'], tabs_s, tm=rs, tiles_per_b=1, per_row_mod=True)
        kv3 = kv.reshape(bs, ts, 4 * KV_W)
        kcvc = _cmp_sample(page_table, cache5, pad_rows(kv3[:, :, :2 * KV_W]), l, nbp_s,
                           lw['pe2'], lw['w1bd'], lw['w2bd'], ctabs_s, ts)
        o8, wst = _nsa_sample(page_table, cache5, pad_rows(q.reshape(bs, ts, D_NSA)),
                              new_tiles(kv3[:, :, 2 * KV_W:]), new_tiles(win.reshape(bs, ts, 2 * KV_W)),
                              state_wint, kcvc, pad_rows(gate.reshape(bs, ts, LANES)), l, ts)
        o = o8[:, :ts].reshape(rs, D_NSA)
        mats = _s5_matrices(s5tabs, l, ts)
        u_pad = jnp.pad(u.reshape(bs, ts, D_S5), ((0, 0), (0, S5_CHUNK - ts), (0, 0)))
        y_g, hl = _s5_chunk(_to_groups(u_pad.reshape(bs * S5_CHUNK, D_S5), bs, 1).astype(BF16),
                            _state_to_groups(state_ssm[:, l]), *mats, nbt=bs)
        yssm = _from_groups(y_g, bs, 1).reshape(bs, S5_CHUNK, D_S5)[:, :ts].reshape(rs, D_S5)
        xs = _post(xs, gmod, yssm, u, zs5, o, znsa, merge, *post_w,
                   tm=rs, tiles_per_b=1, per_row_mod=True)
        kv_s.append(kv.reshape(bs, ts, 4, N_KV, HEAD_DIM))
        win_s.append(wst)
        ssm_s.append(_state_from_groups(hl))

    def from_t(parts, n_streams):
        a = jnp.stack(parts, axis=1)
        nbt, keys = a.shape[0], a.shape[-1]
        return a.reshape(nbt, DEPTH, n_streams, N_KV, HEAD_DIM, keys).transpose(0, 1, 5, 2, 3, 4)

    fg = final_g[None, :]
    y_prompt = _final_norm(xp, fg, tm_p).reshape(bp, tp, D_MODEL)
    y_sample = _final_norm(xs, fg, rs).reshape(bs, ts, D_MODEL)
    return (y_prompt, y_sample, from_t(kv_p, 4), jnp.stack(kv_s, axis=1),
            from_t(win_p, 2), from_t(win_s, 2),
            jnp.stack(ssm_p, axis=1), jnp.stack(ssm_s, axis=1))
```

```python
import functools
import math

import numpy as np
import jax
import jax.numpy as jnp
from jax import lax
from jax.experimental import pallas as pl
from jax.experimental.pallas import tpu as pltpu

F32 = jnp.float32
BF16 = jnp.bfloat16

D_MODEL = 1024
DEPTH = 4
D_S5 = 512
S5_GROUP = 16
S5_GROUPS = 32
S5_STATE = 64
D_NSA = 512
HEAD_DIM = 64
N_HEADS = 8
N_KV = 2
GROUP_SIZE = 4
BLOCK = 64
N_SELECT = 16
WINDOW = 512
ROT_HALF = 8
ROPE_THETA = 500000.0
RMS_EPS = 1e-6
NEG_INF = -1e30
FORCED_SCORE = 1e4

LANES = 128
S5_CHUNK = 16
KV_W = N_KV * HEAD_DIM
N_PROJ = 4992
CMP_PITCH = 72
VMEM_LIMIT = 56 * 1024 * 1024


def _dot(a, b):
    return jnp.dot(a, b, preferred_element_type=F32)


def _dot_nt(a, b):
    return lax.dot_general(a, b, (((1,), (1,)), ((), ())), preferred_element_type=F32)


def _silu(x):
    return x * jax.nn.sigmoid(x)


def _rope128(x, c, sa, sb):
    return x * c + pltpu.roll(x, LANES - ROT_HALF, 1) * sa + pltpu.roll(x, ROT_HALF, 1) * sb


def _masked_exp(s, mask):
    s = jnp.where(mask, s, NEG_INF)
    m = jnp.max(s, axis=-1, keepdims=True)
    p = jnp.where(mask, jnp.exp(s - m), 0.0)
    return p, jnp.sum(p, axis=-1, keepdims=True)


def _safe_inv(l):
    return jnp.where(l > 0.0, 1.0 / l, 0.0)


def _cparams(sem):
    return pltpu.CompilerParams(dimension_semantics=sem, vmem_limit_bytes=VMEM_LIMIT)


def _ada_kernel(c_ref, w_ref, b_ref, o_ref):
    c = c_ref[...]
    o_ref[...] = _dot(_silu(c).astype(BF16), w_ref[...].astype(BF16)) + b_ref[...]


def _ada_mod(c_all, ada_w, ada_b):
    nc = c_all.shape[0]
    tn = 1024
    return pl.pallas_call(
        _ada_kernel,
        grid=(DEPTH, 3 * D_MODEL // tn),
        in_specs=[pl.BlockSpec((nc, D_MODEL), lambda l, n: (0, 0)),
                  pl.BlockSpec((None, D_MODEL, tn), lambda l, n: (l, 0, n)),
                  pl.BlockSpec((None, 1, tn), lambda l, n: (l, 0, n))],
        out_specs=pl.BlockSpec((None, nc, tn), lambda l, n: (l, 0, n)),
        out_shape=jax.ShapeDtypeStruct((DEPTH, nc, 3 * D_MODEL), F32),
        compiler_params=_cparams(("arbitrary", "arbitrary")),
        name="ada_mod",
    )(c_all, ada_w, ada_b.reshape(DEPTH, 1, 3 * D_MODEL))


def _inproj_kernel(x_ref, shift_ref, scale_ref, g_ref, w_ref, cos_ref, sa_ref, sb_ref,
                   u_ref, zs5_ref, q_ref, kv_ref, win_ref, znsa_ref, merge_ref, gate_ref):
    x = x_ref[...]
    h = x * lax.rsqrt(jnp.mean(x * x, axis=-1, keepdims=True) + RMS_EPS) * g_ref[...]
    h = h * (1.0 + scale_ref[...]) + shift_ref[...]
    hb = h.astype(BF16)

    def mm(lo, hi):
        return _dot(hb, w_ref[:, lo:hi])

    c, sa, sb = cos_ref[...], sa_ref[...], sb_ref[...]
    u_ref[...] = mm(0, 512)
    zs5_ref[...] = mm(512, 1024)
    for r in range(GROUP_SIZE):
        lo = 1024 + r * LANES
        q = _rope128(mm(lo, lo + LANES), c, sa, sb) * (HEAD_DIM ** -0.5)
        q_ref[:, r * LANES:(r + 1) * LANES] = q.astype(BF16)
    kv_ref[:, 0:256] = mm(1536, 1792)
    kv_ref[:, 256:384] = _rope128(mm(1792, 1920), c, sa, sb)
    kv_ref[:, 384:512] = mm(1920, 2048)
    win_ref[:, 0:128] = _rope128(mm(2048, 2176), c, sa, sb)
    win_ref[:, 128:256] = mm(2176, 2304)
    znsa_ref[...] = mm(2304, 2816)
    merge_ref[...] = mm(2816, 4864)
    gate_ref[...] = mm(4864, 4992)


def _inproj(x2d, shift, scale, g, wp, tabs, *, tm, tiles_per_b, per_row_mod):
    rows = x2d.shape[0]
    nt = rows // tm
    n_tab = tabs[0].shape[0] // tm
    if per_row_mod:
        mod_spec = pl.BlockSpec((tm, D_MODEL), lambda i: (i, 0))
    else:
        mod_spec = pl.BlockSpec((None, 1, D_MODEL), lambda i: (i // tiles_per_b, 0, 0))
    tab_spec = pl.BlockSpec((tm, LANES), lambda i: (i % n_tab, 0))

    def row_spec(w):
        return pl.BlockSpec((tm, w), lambda i: (i, 0))

    widths = (512, 512, 512, 512, 256, 512, 2048, 128)
    dtypes = (F32, F32, BF16, F32, F32, F32, F32, F32)
    return pl.pallas_call(
        _inproj_kernel,
        grid=(nt,),
        in_specs=[row_spec(D_MODEL), mod_spec, mod_spec,
                  pl.BlockSpec((1, D_MODEL), lambda i: (0, 0)),
                  pl.BlockSpec((D_MODEL, N_PROJ), lambda i: (0, 0)),
                  tab_spec, tab_spec, tab_spec],
        out_specs=[row_spec(w) for w in widths],
        out_shape=[jax.ShapeDtypeStruct((rows, w), d) for w, d in zip(widths, dtypes)],
        compiler_params=_cparams(("arbitrary",)),
        name="inproj",
    )(x2d, shift, scale, g, wp, *tabs)


def _rope128_t(x, c, sa, sb):
    return x * c + pltpu.roll(x, KV_W - ROT_HALF, 0) * sa + pltpu.roll(x, ROT_HALF, 0) * sb


def _inproj_prompt_kernel(x_ref, shift_ref, scale_ref, g_ref, w_ref, wt_ref, cos_ref, sa_ref, sb_ref,
                          cost_ref, sat_ref, sbt_ref,
                          u_ref, zs5_ref, q_ref, kvt_ref, wint_ref, cmp_ref, znsa_ref, merge_ref, gate_ref):
    x = x_ref[...]
    h = x * lax.rsqrt(jnp.mean(x * x, axis=-1, keepdims=True) + RMS_EPS) * g_ref[...]
    h = h * (1.0 + scale_ref[...]) + shift_ref[...]
    hb = h.astype(BF16)

    def mm(lo, hi):
        return _dot(hb, w_ref[:, lo:hi])

    c, sa, sb = cos_ref[...], sa_ref[...], sb_ref[...]
    for ch in range(D_S5 // LANES):
        u_ref[ch] = mm(ch * LANES, (ch + 1) * LANES)
    zs5_ref[...] = mm(512, 1024)
    for r in range(GROUP_SIZE):
        lo = 1024 + r * LANES
        q = _rope128(mm(lo, lo + LANES), c, sa, sb) * (HEAD_DIM ** -0.5)
        q_ref[:, r * LANES:(r + 1) * LANES] = q.astype(BF16)
    cmp_ref[...] = mm(1536, 1792)
    znsa_ref[...] = mm(2304, 2816)
    merge_ref[...] = mm(2816, 4864)
    gate_ref[...] = mm(4864, 4992)

    ct, sat, sbt = cost_ref[...], sat_ref[...], sbt_ref[...]

    def mmt(s):
        return _dot_nt(wt_ref[s * KV_W:(s + 1) * KV_W, :], hb)

    kvt_ref[0:KV_W, :] = mmt(0)
    kvt_ref[KV_W:2 * KV_W, :] = mmt(1)
    kvt_ref[2 * KV_W:3 * KV_W, :] = _rope128_t(mmt(2), ct, sat, sbt)
    kvt_ref[3 * KV_W:4 * KV_W, :] = mmt(3)
    wint_ref[0:KV_W, :] = _rope128_t(mmt(4), ct, sat, sbt)
    wint_ref[KV_W:2 * KV_W, :] = mmt(5)


def _inproj_prompt(x2d, shift, scale, g, wp, wt, tabs, tabs_t, *, tm, nb_batch, t):
    rows = x2d.shape[0]
    tpb = t // tm
    mod_spec = pl.BlockSpec((None, 1, D_MODEL), lambda i: (i // tpb, 0, 0))
    tab_spec = pl.BlockSpec((tm, LANES), lambda i: (i % tpb, 0))
    tabt_spec = pl.BlockSpec((KV_W, tm), lambda i: (0, i % tpb))
    row = lambda w: pl.BlockSpec((tm, w), lambda i: (i, 0))
    tr = lambda h: pl.BlockSpec((None, h, tm), lambda i: (i // tpb, 0, i % tpb))
    u_spec = pl.BlockSpec((D_S5 // LANES, tm, LANES), lambda i: (0, i, 0))
    outs = [((D_S5 // LANES, rows, LANES), F32, u_spec), ((rows, 512), F32, row(512)),
            ((rows, 512), BF16, row(512)),
            ((nb_batch, 4 * KV_W, t), F32, tr(4 * KV_W)), ((nb_batch, 2 * KV_W, t), F32, tr(2 * KV_W)),
            ((rows, 256), F32, row(256)), ((rows, 512), F32, row(512)), ((rows, 2048), F32, row(2048)),
            ((rows, 128), F32, row(128))]
    return pl.pallas_call(
        _inproj_prompt_kernel,
        grid=(rows // tm,),
        in_specs=[row(D_MODEL), mod_spec, mod_spec,
                  pl.BlockSpec((1, D_MODEL), lambda i: (0, 0)),
                  pl.BlockSpec((D_MODEL, N_PROJ), lambda i: (0, 0)),
                  pl.BlockSpec((6 * KV_W, D_MODEL), lambda i: (0, 0)),
                  tab_spec, tab_spec, tab_spec, tabt_spec, tabt_spec, tabt_spec],
        out_specs=[o[2] for o in outs],
        out_shape=[jax.ShapeDtypeStruct(o[0], o[1]) for o in outs],
        compiler_params=_cparams(("arbitrary",)),
        name="inproj_prompt",
    )(x2d, shift, scale, g, wp, wt, *tabs, *tabs_t)


def _compress(load_p, nbk, pe_ref, w1_ref, w2_ref, c, sa, sb):
    outs = []
    for s in range(2):
        acc = jnp.zeros((nbk, KV_W), F32)
        for p in range(BLOCK):
            xp = load_p(s, p) + pe_ref[s, p]
            acc = acc + _dot(xp.astype(BF16), w1_ref[s, p])
        outs.append(_dot(_silu(acc).astype(BF16), w2_ref[s]))
    return jnp.concatenate([_rope128(outs[0], c, sa, sb), outs[1]], axis=1)


def _cmp_prompt_kernel(k_ref, v_ref, pe_ref, w1_ref, w2_ref, cos_ref, sa_ref, sb_ref, o_ref, *, nbk):
    def load_p(s, p):
        return (k_ref, v_ref)[s][pl.ds(p, nbk, stride=BLOCK), :]

    res = _compress(load_p, nbk, pe_ref, w1_ref, w2_ref,
                    cos_ref[0:nbk, :], sa_ref[0:nbk, :], sb_ref[0:nbk, :])
    o_ref[...] = jnp.zeros(o_ref.shape, F32)
    o_ref[0:nbk, :] = res


def _cmp_prompt(kv2d, nb_batch, t, nbp, pe2, w1bd, w2bd, ctabs):
    nbk = t // BLOCK
    full = lambda shape: pl.BlockSpec(shape, lambda b: (0,) * len(shape))
    return pl.pallas_call(
        functools.partial(_cmp_prompt_kernel, nbk=nbk),
        grid=(nb_batch,),
        in_specs=[pl.BlockSpec((t, KV_W), lambda b: (b, 0)), pl.BlockSpec((t, KV_W), lambda b: (b, 1)),
                  full((2, BLOCK, 1, KV_W)), full((2, BLOCK, KV_W, KV_W)),
                  full((2, KV_W, KV_W)), full((nbp, LANES)), full((nbp, LANES)), full((nbp, LANES))],
        out_specs=pl.BlockSpec((None, nbp, 2 * KV_W), lambda b: (b, 0, 0)),
        out_shape=jax.ShapeDtypeStruct((nb_batch, nbp, 2 * KV_W), F32),
        compiler_params=_cparams(("arbitrary",)),
        name="cmp_prompt",
    )(kv2d, kv2d, pe2, w1bd, w2bd, *ctabs)


def _stack_queries(qt, tq):
    lane = lax.broadcasted_iota(jnp.int32, (tq, LANES), 1)
    qt = qt.astype(F32)
    blocks = []
    for g in range(N_KV):
        keep = (lane < HEAD_DIM) if g == 0 else (lane >= HEAD_DIM)
        for r in range(GROUP_SIZE):
            blocks.append(jnp.where(keep, qt[:, r * LANES:(r + 1) * LANES], 0.0))
    return jnp.concatenate(blocks, axis=0).astype(BF16)


def _select_blocks(imp, qblk, nb, n_sel):
    nq, nbp = imp.shape
    n = lax.broadcasted_iota(jnp.int32, (nq, nbp), 1)
    forced = (n == 0) | (n == qblk) | (n == qblk - 1)
    imp = jnp.where(forced, FORCED_SCORE, imp)
    imp = jnp.where(n > qblk, -1.0, imp)
    imp = jnp.where(n >= nb, -2.0, imp)
    rank = jnp.zeros((nq, nbp), F32)
    for m in range(nb):
        col = imp[:, m:m + 1]
        beats = (col > imp) | ((col == imp) & (n > m))
        rank = rank + jnp.where(beats, 1.0, 0.0)
    return jnp.where((rank < n_sel) & (imp > -0.5), 1.0, 0.0)


def _compressed_branch(qs, kcvc, tq_col, tq, nb):
    nbp = kcvc.shape[0]
    kc = kcvc[:, :KV_W].astype(BF16)
    vc = kcvc[:, KV_W:].astype(BF16)
    s = _dot_nt(qs, kc)
    n = lax.broadcasted_iota(jnp.int32, (1, nbp), 1)
    mask = (n * BLOCK + (BLOCK - 1) <= tq_col) & (n < nb)
    p, l = _masked_exp(s, mask)
    p = p * _safe_inv(l)
    o_c = _dot(p.astype(BF16), vc)
    imps = []
    for g in range(N_KV):
        acc = p[(g * GROUP_SIZE) * tq:(g * GROUP_SIZE + 1) * tq]
        for r in range(1, GROUP_SIZE):
            acc = acc + p[(g * GROUP_SIZE + r) * tq:(g * GROUP_SIZE + r + 1) * tq]
        imps.append(acc)
    return o_c, imps


def _block_mask(selstack, k0, tk):
    nbp = selstack.shape[1]
    kblk = (k0 + lax.broadcasted_iota(jnp.int32, (nbp, tk), 1)) // BLOCK
    e = jnp.where(kblk == lax.broadcasted_iota(jnp.int32, (nbp, tk), 0), 1.0, 0.0).astype(BF16)
    return _dot(selstack, e) > 0.5


def _gate_cols(gt, tq, branch):
    cols = []
    for g in range(N_KV):
        for r in range(GROUP_SIZE):
            j = branch * N_HEADS + g * GROUP_SIZE + r
            cols.append(gt[:, j:j + 1])
    return jnp.concatenate(cols, axis=0)


def _unstack_heads(o, tq):
    lane = lax.broadcasted_iota(jnp.int32, (tq, LANES), 1)
    chunks = []
    for r in range(GROUP_SIZE):
        a = o[r * tq:(r + 1) * tq]
        b = o[(GROUP_SIZE + r) * tq:(GROUP_SIZE + r + 1) * tq]
        chunks.append(jnp.where(lane < HEAD_DIM, a, b))
    return chunks


def _select_blocks_t(imp, qblk_row, nb, n_sel):
    nq, nbp = imp.shape
    nbr = (nb + 7) // 8 * 8
    x = imp.T[0:nbr, :]
    n = lax.broadcasted_iota(jnp.int32, (nbr, nq), 0)
    forced = (n == 0) | (n == qblk_row) | (n == qblk_row - 1)
    x = jnp.where(forced, FORCED_SCORE, x)
    x = jnp.where(n > qblk_row, -1.0, x)
    x = jnp.where(n >= nb, -2.0, x)
    rank = jnp.zeros((nbr, nq), F32)
    for m in range(nb):
        row = x[m:m + 1, :]
        beats = (row > x) | ((row == x) & (n > m))
        rank = rank + jnp.where(beats, 1.0, 0.0)
    sel = jnp.where((rank < n_sel) & (x > -0.5), 1.0, 0.0)
    if nbr < nbp:
        sel = jnp.concatenate([sel, jnp.zeros((nbp - nbr, nq), F32)], axis=0)
    return sel.T


def _nsa_prompt_kernel(q_ref, kvt_ref, wint_ref, kcvc_ref, gate_ref, e_ref, cbias_ref, wbias_ref,
                       o_ref, kaug_sc, vsel_sc, waug_sc, vwin_sc, s_sc, m_sc, l_sc, acc_sc, sel_sc,
                       *, tq, nb, n_sel):
    i = pl.program_id(1)
    q0 = pl.multiple_of(i * tq, tq)
    rows = N_HEADS * tq
    wk = WINDOW + tq
    tk = 2 * LANES
    neg = jnp.asarray(NEG_INF, F32)

    @pl.when(i == 0)
    def _():
        kaug_sc[0:KV_W, :] = kvt_ref[0:KV_W, :].astype(BF16)
        kaug_sc[KV_W:2 * KV_W, :] = e_ref[...]
        vsel_sc[...] = kvt_ref[KV_W:2 * KV_W, :].astype(BF16)
        waug_sc[...] = jnp.zeros(waug_sc.shape, BF16)
        waug_sc[0:KV_W, WINDOW:] = wint_ref[0:KV_W, :].astype(BF16)
        waug_sc[KV_W:KV_W + 16, 0:WINDOW] = jnp.ones((16, WINDOW), BF16)
        vwin_sc[:, 0:WINDOW] = jnp.zeros((KV_W, WINDOW), BF16)
        vwin_sc[:, WINDOW:] = wint_ref[KV_W:2 * KV_W, :].astype(BF16)

    qs = _stack_queries(q_ref[...], tq)
    t_loc = lax.broadcasted_iota(jnp.int32, (rows, 1), 0) % tq
    tq_col = q0 + t_loc

    o_c, imps = _compressed_branch(qs, kcvc_ref[...], tq_col, tq, nb)

    sel_sc[...] = jnp.ones(sel_sc.shape, F32)

    @pl.when(q0 + tq > n_sel * BLOCK)
    def _():
        qblk_row = (q0 + lax.broadcasted_iota(jnp.int32, (1, tq), 1)) // BLOCK
        for g in range(N_KV):
            sel_sc[g] = _select_blocks_t(imps[g], qblk_row, nb, n_sel)

    selb = [((sel_sc[g] - 1.0) * (-NEG_INF)).astype(BF16) for g in range(N_KV)]
    selstack = jnp.concatenate([selb[g] for g in range(N_KV) for _ in range(GROUP_SIZE)], axis=0)
    qaug = jnp.concatenate([qs, selstack], axis=1)

    n_full = q0 // tk
    par = (q0 // tq) % 2
    m_sc[...] = jnp.full(m_sc.shape, neg, F32)

    def score_tile(k0, bias):
        s = _dot(qaug, kaug_sc[:, pl.ds(k0, tk)])
        if bias is not None:
            s = s + bias
        s_sc[:, pl.ds(k0, tk)] = s
        m_sc[...] = jnp.maximum(m_sc[...], jnp.maximum(s[:, :LANES], s[:, LANES:]))

    def pass1(j, carry):
        score_tile(pl.multiple_of(j * tk, tk), None)
        return carry

    lax.fori_loop(0, n_full, pass1, 0)
    score_tile(pl.multiple_of(n_full * tk, tk), cbias_ref[par])
    m_b = jnp.broadcast_to(jnp.max(m_sc[...], axis=-1, keepdims=True), (rows, LANES))

    l_sc[...] = jnp.zeros(l_sc.shape, F32)
    acc_sc[...] = jnp.zeros(acc_sc.shape, F32)

    def pass2(j, carry):
        k0 = pl.multiple_of(j * tk, tk)
        s = s_sc[:, pl.ds(k0, tk)]
        p0 = jnp.exp(s[:, :LANES] - m_b)
        p1 = jnp.exp(s[:, LANES:] - m_b)
        l_sc[...] = l_sc[...] + (p0 + p1)
        p = jnp.concatenate([p0, p1], axis=1).astype(BF16)
        acc_sc[...] = acc_sc[...] + _dot_nt(p, vsel_sc[:, pl.ds(k0, tk)])
        return carry

    lax.fori_loop(0, n_full + 1, pass2, 0)
    o_s = acc_sc[...] * _safe_inv(jnp.sum(l_sc[...], axis=-1, keepdims=True))

    lane = lax.broadcasted_iota(jnp.int32, (rows, LANES), 1)
    padcol = jnp.where(lane == 0, neg, 0.0).astype(BF16)
    qaug_w = jnp.concatenate([qs, padcol], axis=1)
    s = _dot(qaug_w, waug_sc[:, pl.ds(q0, wk)]) + wbias_ref[...]
    p = jnp.exp(s - jnp.max(s, axis=-1, keepdims=True))
    l = jnp.sum(p, axis=-1, keepdims=True)
    o_w = _dot_nt(p.astype(BF16), vwin_sc[:, pl.ds(q0, wk)]) * _safe_inv(l)

    gt = jax.nn.sigmoid(gate_ref[...])
    o = _gate_cols(gt, tq, 0) * o_c + _gate_cols(gt, tq, 1) * o_s + _gate_cols(gt, tq, 2) * o_w
    for r, ch in enumerate(_unstack_heads(o, tq)):
        o_ref[:, r * LANES:(r + 1) * LANES] = ch


def _attn_constants(t, tq):
    rows = N_HEADS * tq
    t_loc = np.arange(rows)[:, None] % tq
    k = np.arange(2 * LANES)[None, :]
    cb = np.stack([np.where(k <= t_loc, 0.0, NEG_INF), np.where(k - LANES <= t_loc, 0.0, NEG_INF)])
    kr = np.arange(WINDOW + tq)[None, :]
    wb = np.where((kr - WINDOW <= t_loc) & (kr > t_loc), 0.0, NEG_INF)
    e = (np.arange(t)[None, :] // BLOCK == np.arange(LANES)[:, None]).astype(np.float32)
    return jnp.asarray(e, BF16), jnp.asarray(cb, F32), jnp.asarray(wb, F32)


def _nsa_prompt(q2d, kvt, wint, kcvc, gate2d, consts, nb_batch, t, tq):
    nb = t // BLOCK
    nq = t // tq
    nbp = kcvc.shape[1]
    rows = N_HEADS * tq
    wk = WINDOW + tq
    assert nbp == LANES and t % (2 * LANES) == 0
    e, cb, wb = consts
    kern = functools.partial(_nsa_prompt_kernel, tq=tq, nb=nb, n_sel=min(N_SELECT, nb))
    return pl.pallas_call(
        kern,
        grid=(nb_batch, nq),
        in_specs=[pl.BlockSpec((tq, D_NSA), lambda b, i: (b * nq + i, 0)),
                  pl.BlockSpec((None, 2 * KV_W, t), lambda b, i: (b, 1, 0)),
                  pl.BlockSpec((None, 2 * KV_W, t), lambda b, i: (b, 0, 0)),
                  pl.BlockSpec((None, nbp, 2 * KV_W), lambda b, i: (b, 0, 0)),
                  pl.BlockSpec((tq, LANES), lambda b, i: (b * nq + i, 0)),
                  pl.BlockSpec((LANES, t), lambda b, i: (0, 0)),
                  pl.BlockSpec((2, rows, 2 * LANES), lambda b, i: (0, 0, 0)),
                  pl.BlockSpec((rows, wk), lambda b, i: (0, 0))],
        out_specs=pl.BlockSpec((tq, D_NSA), lambda b, i: (b * nq + i, 0)),
        out_shape=jax.ShapeDtypeStruct((nb_batch * t, D_NSA), F32),
        scratch_shapes=[pltpu.VMEM((2 * KV_W, t), BF16), pltpu.VMEM((KV_W, t), BF16),
                        pltpu.VMEM((2 * KV_W, WINDOW + t), BF16), pltpu.VMEM((KV_W, WINDOW + t), BF16),
                        pltpu.VMEM((rows, t), F32), pltpu.VMEM((rows, LANES), F32),
                        pltpu.VMEM((rows, LANES), F32), pltpu.VMEM((rows, KV_W), F32),
                        pltpu.VMEM((N_KV, tq, LANES), F32)],
        compiler_params=_cparams(("arbitrary", "arbitrary")),
        name="nsa_prompt",
    )(q2d, kvt, wint, kcvc, gate2d, e, cb, wb)


def _page_copy(cache_ref, slabs, sem_ref, pt_ref, layer, b, p, slot, s0, page, s):
    return pltpu.make_async_copy(
        cache_ref.at[pt_ref[b, p], layer, s0 + s],
        slabs[s].at[slot, :, pl.ds(p * page, page)],
        sem_ref.at[slot])


def _gather_pages(cache_ref, slabs, sem_ref, pt_ref, layer, s0, n_pages, page):
    b = pl.program_id(0)
    nb_batch = pl.num_programs(0)
    slot = b % 2

    def copies(bb, sl):
        return [_page_copy(cache_ref, slabs, sem_ref, pt_ref, layer, bb, p, sl, s0, page, s)
                for p in range(n_pages) for s in range(2)]

    @pl.when(b == 0)
    def _():
        for cp in copies(0, 0):
            cp.start()

    @pl.when(b + 1 < nb_batch)
    def _():
        for cp in copies(b + 1, 1 - slot):
            cp.start()

    for cp in copies(b, slot):
        cp.wait()
    return slot


def _cmp_sample_kernel(pt_ref, cache_ref, new_ref, pe_ref, w1_ref, w2_ref, cos_ref, sa_ref, sb_ref,
                       o_ref, stagek_ref, stagev_ref, rowk_ref, rowv_ref, sem_ref,
                       *, layer, n_pages, page, nbk):
    stages = (stagek_ref, stagev_ref)
    rowm = (rowk_ref, rowv_ref)
    slot = _gather_pages(cache_ref, stages, sem_ref, pt_ref, layer, 0, n_pages, page)
    bpp = page // BLOCK
    nb_past = n_pages * bpp
    unroll = next(k for k in (8, 4, 2, 1) if n_pages % k == 0)
    for s in range(2):
        def xpose(i, carry):
            for k in range(unroll):
                p = i * unroll + k
                c0 = pl.multiple_of(p * page, page)
                tile = stages[s][slot, :, pl.ds(c0, page)].T
                for h in range(bpp):
                    r0 = pl.multiple_of((p * bpp + h) * CMP_PITCH, 8)
                    rowm[s][pl.ds(r0, BLOCK), :] = tile[h * BLOCK:(h + 1) * BLOCK, :]
            return carry

        lax.fori_loop(0, n_pages // unroll, xpose, 0)
        tail0 = nb_past * CMP_PITCH
        tail = rowm[s].shape[0] - tail0
        rowm[s][pl.ds(tail0, tail), :] = jnp.zeros((tail, KV_W), F32)
        rowm[s][pl.ds(tail0, 8), :] = new_ref[:, s * KV_W:(s + 1) * KV_W]

    def load_p(s, p):
        return rowm[s][pl.ds(p, nbk, stride=CMP_PITCH), :]

    res = _compress(load_p, nbk, pe_ref, w1_ref, w2_ref,
                    cos_ref[0:nbk, :], sa_ref[0:nbk, :], sb_ref[0:nbk, :])
    o_ref[...] = jnp.zeros(o_ref.shape, F32)
    o_ref[0:nbk, :] = res


def _cmp_sample(page_table, cache5, newcmp, layer, nbp, pe2, w1bd, w2bd, ctabs, t_new):
    nb_batch, n_pages = page_table.shape
    page = cache5.shape[4]
    past = n_pages * page
    nbk = ((past + t_new + BLOCK - 1) // BLOCK + 7) // 8 * 8
    slab_rows = nbk * CMP_PITCH
    assert page % BLOCK == 0 and t_new <= 8
    full = lambda shape: pl.BlockSpec(shape, lambda b, pt: (0,) * len(shape))
    kern = functools.partial(_cmp_sample_kernel, layer=layer, n_pages=n_pages, page=page, nbk=nbk)
    return pl.pallas_call(
        kern,
        grid_spec=pltpu.PrefetchScalarGridSpec(
            num_scalar_prefetch=1,
            grid=(nb_batch,),
            in_specs=[pl.BlockSpec(memory_space=pl.ANY),
                      pl.BlockSpec((None, 8, 2 * KV_W), lambda b, pt: (b, 0, 0)),
                      full((2, BLOCK, 1, KV_W)), full((2, BLOCK, KV_W, KV_W)),
                      full((2, KV_W, KV_W)), full((nbp, LANES)), full((nbp, LANES)),
                      full((nbp, LANES))],
            out_specs=pl.BlockSpec((None, nbp, 2 * KV_W), lambda b, pt: (b, 0, 0)),
            scratch_shapes=[pltpu.VMEM((2, KV_W, past), F32), pltpu.VMEM((2, KV_W, past), F32),
                            pltpu.VMEM((slab_rows, KV_W), F32), pltpu.VMEM((slab_rows, KV_W), F32),
                            pltpu.SemaphoreType.DMA((2,))]),
        out_shape=jax.ShapeDtypeStruct((nb_batch, nbp, 2 * KV_W), F32),
        compiler_params=_cparams(("arbitrary",)),
        name="cmp_sample",
    )(page_table, cache5, newcmp, pe2, w1bd, w2bd, *ctabs)


def _nsa_sample_kernel(pt_ref, cache_ref, q_ref, newt_ref, wnewt_ref, swint_ref, kcvc_ref, gate_ref,
                       o_ref, wout_ref, slabk_ref, slabv_ref, wslab_ref, sem_ref,
                       *, layer, n_pages, page, t_new, nb, n_sel):
    tq = 8
    past = n_pages * page
    rows = N_HEADS * tq
    slabs = (slabk_ref, slabv_ref)
    slot = _gather_pages(cache_ref, slabs, sem_ref, pt_ref, layer, 2, n_pages, page)
    nk = slabk_ref.shape[2]
    for s in range(2):
        slabs[s][slot, :, pl.ds(past, LANES)] = newt_ref[s]

    qs = _stack_queries(q_ref[...], tq)
    t_loc = lax.broadcasted_iota(jnp.int32, (rows, 1), 0) % tq
    tq_col = past + t_loc

    o_c, imps = _compressed_branch(qs, kcvc_ref[...], tq_col, tq, nb)
    qblk = (past + lax.broadcasted_iota(jnp.int32, (tq, 1), 0)) // BLOCK
    sels = [_select_blocks(imps[g], qblk, nb, n_sel).astype(BF16) for g in range(N_KV)]
    selstack = jnp.concatenate([sels[g] for g in range(N_KV) for _ in range(GROUP_SIZE)], axis=0)

    s = _dot(qs, slabk_ref[slot].astype(BF16))
    kpos = lax.broadcasted_iota(jnp.int32, (1, nk), 1)
    p, l = _masked_exp(s, _block_mask(selstack, 0, nk) & (kpos <= tq_col))
    o_s = _dot_nt(p.astype(BF16), slabv_ref[slot].astype(BF16)) * _safe_inv(l)

    wbuf = swint_ref.shape[2]
    wk = wslab_ref.shape[2]
    wslab_ref[:, :, 0:wbuf] = swint_ref[...]
    wslab_ref[:, :, wbuf:wk] = wnewt_ref[...]
    s = _dot(qs, wslab_ref[0].astype(BF16))
    kpos = past - wbuf + lax.broadcasted_iota(jnp.int32, (1, wk), 1)
    p, l = _masked_exp(s, (kpos <= tq_col) & (kpos > tq_col - WINDOW) & (kpos >= 0))
    o_w = _dot_nt(p.astype(BF16), wslab_ref[1].astype(BF16)) * _safe_inv(l)
    wout_ref[...] = wslab_ref[:, :, t_new:t_new + wbuf]

    gt = jax.nn.sigmoid(gate_ref[...])
    o = _gate_cols(gt, tq, 0) * o_c + _gate_cols(gt, tq, 1) * o_s + _gate_cols(gt, tq, 2) * o_w
    for r, ch in enumerate(_unstack_heads(o, tq)):
        o_ref[:, r * LANES:(r + 1) * LANES] = ch


def _nsa_sample(page_table, cache5, q3, newselt, wnewt, state_wint, kcvc, gate3, layer, t_new):
    nb_batch, n_pages = page_table.shape
    page = cache5.shape[4]
    past = n_pages * page
    nb = (past + t_new + BLOCK - 1) // BLOCK
    nk = past + LANES
    wbuf = state_wint.shape[4]
    wk = wbuf + LANES
    nbp = kcvc.shape[1]
    kern = functools.partial(_nsa_sample_kernel, layer=layer, n_pages=n_pages, page=page,
                             t_new=t_new, nb=nb, n_sel=min(N_SELECT, nb))
    per_b = lambda d1, d2: pl.BlockSpec((None, d1, d2), lambda b, pt: (b, 0, 0))
    per_b4 = lambda d1, d2, d3: pl.BlockSpec((None, d1, d2, d3), lambda b, pt: (b, 0, 0, 0))
    return pl.pallas_call(
        kern,
        grid_spec=pltpu.PrefetchScalarGridSpec(
            num_scalar_prefetch=1,
            grid=(nb_batch,),
            in_specs=[pl.BlockSpec(memory_space=pl.ANY),
                      per_b(8, D_NSA), per_b4(2, KV_W, LANES), per_b4(2, KV_W, LANES),
                      pl.BlockSpec((None, None, 2, KV_W, wbuf), lambda b, pt: (b, layer, 0, 0, 0)),
                      per_b(nbp, 2 * KV_W), per_b(8, LANES)],
            out_specs=[per_b(8, D_NSA), per_b4(2, KV_W, wbuf)],
            scratch_shapes=[pltpu.VMEM((2, KV_W, nk), F32), pltpu.VMEM((2, KV_W, nk), F32),
                            pltpu.VMEM((2, KV_W, wk), F32), pltpu.SemaphoreType.DMA((2,))]),
        out_shape=[jax.ShapeDtypeStruct((nb_batch, 8, D_NSA), F32),
                   jax.ShapeDtypeStruct((nb_batch, 2, KV_W, wbuf), F32)],
        compiler_params=_cparams(("arbitrary",)),
        name="nsa_sample",
    )(page_table, cache5, q3, newselt, wnewt, state_wint, kcvc, gate3)


def _s5_setup_kernel(are_ref, aim_ref, ldt_ref, bre_ref, bim_ref, cre_ref, cim_ref,
                     ckr_ref, cki_ref, bkr_ref, bki_ref, m_ref, pwr_ref, pwi_ref):
    nk = pwr_ref.shape[1]

    def body(g, carry):
        ar = are_ref[g]
        ai = aim_ref[g]
        dt = jnp.exp(ldt_ref[g])
        mag = jnp.exp(ar * dt)
        abr = mag * jnp.cos(ai * dt)
        abi = mag * jnp.sin(ai * dt)
        den = ar * ar + ai * ai
        nr = abr - 1.0
        e_re = (nr * ar + abi * ai) / den
        e_im = (abi * ar - nr * ai) / den
        bre = bre_ref[g]
        bim = bim_ref[g]
        bbr = e_re * bre - e_im * bim
        bbi = e_re * bim + e_im * bre
        kk = lax.broadcasted_iota(jnp.int32, (nk, S5_STATE), 0).astype(F32)
        pmag = jnp.exp(kk * (ar * dt))
        pwr = pmag * jnp.cos(kk * (ai * dt))
        pwi = pmag * jnp.sin(kk * (ai * dt))
        pwr_ref[g] = pwr
        pwi_ref[g] = pwi
        cre = cre_ref[g]
        cim = cim_ref[g]
        for k in range(S5_CHUNK + 1):
            wr = pwr[k:k + 1, :]
            wi = pwi[k:k + 1, :]
            ckr_ref[g, k * S5_GROUP:(k + 1) * S5_GROUP, :] = cre * wr - cim * wi
            cki_ref[g, k * S5_GROUP:(k + 1) * S5_GROUP, :] = -(cre * wi + cim * wr)
            if k < S5_CHUNK:
                bkr_ref[g, k * S5_GROUP:(k + 1) * S5_GROUP, :] = bbr * wr - bbi * wi
                bki_ref[g, k * S5_GROUP:(k + 1) * S5_GROUP, :] = bbr * wi + bbi * wr
        nl = S5_CHUNK * S5_GROUP
        hp = lax.Precision.HIGHEST
        m_ref[g] = (lax.dot_general(ckr_ref[g, 0:nl, :], bbr, (((1,), (1,)), ((), ())),
                                    precision=hp, preferred_element_type=F32)
                    + lax.dot_general(cki_ref[g, 0:nl, :], bbi, (((1,), (1,)), ((), ())),
                                      precision=hp, preferred_element_type=F32))
        return carry

    lax.fori_loop(0, S5_GROUPS, body, 0)


def _s5_setup(a_re, a_im, log_dt, b_re, b_im, c_re, c_im):
    g, p, c = S5_GROUPS, S5_STATE, S5_GROUP
    nk = 24
    outs = [(DEPTH, g, (S5_CHUNK + 1) * c, p), (DEPTH, g, (S5_CHUNK + 1) * c, p),
            (DEPTH, g, S5_CHUNK * c, p), (DEPTH, g, S5_CHUNK * c, p),
            (DEPTH, g, S5_CHUNK * c, c), (DEPTH, g, nk, p), (DEPTH, g, nk, p)]
    lay = lambda s: pl.BlockSpec((None,) + s[1:], lambda l: (l,) + (0,) * (len(s) - 1))
    ins = [a_re.reshape(DEPTH, g, 1, p), a_im.reshape(DEPTH, g, 1, p),
           jnp.broadcast_to(log_dt[:, :, None, None], (DEPTH, g, 1, p)),
           b_re.transpose(0, 1, 3, 2), b_im.transpose(0, 1, 3, 2), c_re, c_im]
    return pl.pallas_call(
        _s5_setup_kernel,
        grid=(DEPTH,),
        in_specs=[lay(x.shape) for x in ins],
        out_specs=[lay(s) for s in outs],
        out_shape=[jax.ShapeDtypeStruct(s, F32) for s in outs],
        compiler_params=_cparams(("arbitrary",)),
        name="s5_setup",
    )(*ins)


def _s5_chunk_kernel(u_ref, h0_ref, kmat_ref, smat_ref, ymat_ref, a1_ref, a2_ref,
                     y_ref, hl_ref, s_sc, hp_sc, *, gb, nj, nbt):
    for gi in range(gb):
        s_sc[gi] = _dot(u_ref[gi], smat_ref[gi])

    def step(j, hs):
        r0 = pl.multiple_of(j * nbt, nbt)
        new = []
        for gi in range(gb):
            h = hs[gi]
            hp_sc[gi, pl.ds(r0, nbt), :] = h
            new.append(a1_ref[gi] * h + a2_ref[gi] * pltpu.roll(h, S5_STATE, 1)
                       + s_sc[gi, pl.ds(r0, nbt), :])
        return tuple(new)

    hs = lax.fori_loop(0, nj, step, tuple(h0_ref[gi] for gi in range(gb)))
    for gi in range(gb):
        hl_ref[gi] = hs[gi]
        y_ref[gi] = _dot(u_ref[gi], kmat_ref[gi]) + _dot(hp_sc[gi].astype(BF16), ymat_ref[gi])


def _s5_chunk(u_g, h0_g, kmat, smat, ymat, a1, a2, nbt):
    g, rows, w = u_g.shape
    nj = rows // nbt
    gb = 8
    st = 2 * S5_STATE
    blk = lambda d1, d2: pl.BlockSpec((gb, d1, d2), lambda i: (i, 0, 0))
    kern = functools.partial(_s5_chunk_kernel, gb=gb, nj=nj, nbt=nbt)
    return pl.pallas_call(
        kern,
        grid=(g // gb,),
        in_specs=[blk(rows, w), blk(nbt, st), blk(w, w), blk(w, st), blk(st, w), blk(1, st), blk(1, st)],
        out_specs=[blk(rows, w), blk(nbt, st)],
        out_shape=[jax.ShapeDtypeStruct((g, rows, w), F32), jax.ShapeDtypeStruct((g, nbt, st), F32)],
        scratch_shapes=[pltpu.VMEM((gb, rows, st), F32), pltpu.VMEM((gb, rows, st), F32)],
        compiler_params=_cparams(("arbitrary",)),
        name="s5_chunk",
    )(u_g, h0_g, kmat, smat, ymat, a1, a2)


def _s5_rows_kernel(u_ref, w_ref, sw_ref, yw_ref, a1_ref, a2_ref, y_ref, hl_ref,
                    xr_sc, sg_sc, hp_sc, hcat_sc, *, nbl, nj):
    L = S5_CHUNK
    mr = nbl * nj
    ng = LANES // S5_GROUP
    st = 2 * S5_STATE
    for s in range(L):
        xr_sc[:, (L - 1 - s) * LANES:(L - s) * LANES] = u_ref[pl.ds(s, mr, stride=L), :].astype(BF16)
    sall = _dot(xr_sc[...], sw_ref[...])
    for g in range(ng):
        sg_sc[g] = sall[:, g * st:(g + 1) * st]

    def step(j, hs):
        new = []
        for g in range(ng):
            h = hs[g]
            hp_sc.at[g][pl.ds(j, nbl, stride=nj), :] = h
            sj = sg_sc.at[g][pl.ds(j, nbl, stride=nj), :]
            new.append(a1_ref[g] * h + a2_ref[g] * pltpu.roll(h, S5_STATE, 1) + sj)
        return tuple(new)

    hs = lax.fori_loop(0, nj, step, tuple(jnp.zeros((nbl, st), F32) for _ in range(ng)))
    for g in range(ng):
        hl_ref[g] = hs[g]
        hcat_sc[:, g * st:(g + 1) * st] = hp_sc[g].astype(BF16)
    for t in range(L):
        y = (_dot(xr_sc[:, (L - 1 - t) * LANES:], w_ref[0:(t + 1) * LANES, :])
             + _dot(hcat_sc[...], yw_ref[:, t * LANES:(t + 1) * LANES]))
        y_ref[pl.ds(t, mr, stride=L), :] = y


def _s5_rows(u4, w, sw, yw, a1, a2, nb_batch, t):
    nch, rows, _ = u4.shape
    nsplit = 2
    nbl = nb_batch // nsplit
    nj = t // S5_CHUNK
    rb = nbl * t
    mr = nbl * nj
    ng = LANES // S5_GROUP
    st = 2 * S5_STATE
    kern = functools.partial(_s5_rows_kernel, nbl=nbl, nj=nj)
    per_c = lambda *s: pl.BlockSpec((None,) + s, lambda c, h: (c,) + (0,) * len(s))
    return pl.pallas_call(
        kern,
        grid=(nch, nsplit),
        in_specs=[pl.BlockSpec((None, rb, LANES), lambda c, h: (c, h, 0)),
                  per_c(S5_CHUNK * LANES, LANES), per_c(S5_CHUNK * LANES, ng * st),
                  per_c(ng * st, S5_CHUNK * LANES), per_c(ng, 1, st), per_c(ng, 1, st)],
        out_specs=[pl.BlockSpec((None, rb, LANES), lambda c, h: (c, h, 0)),
                   pl.BlockSpec((None, ng, None, nbl, st), lambda c, h: (c, 0, h, 0, 0))],
        out_shape=[jax.ShapeDtypeStruct((nch, rows, LANES), F32),
                   jax.ShapeDtypeStruct((nch, ng, nsplit, nbl, st), F32)],
        scratch_shapes=[pltpu.VMEM((mr, S5_CHUNK * LANES), BF16), pltpu.VMEM((ng, mr, st), F32),
                        pltpu.VMEM((ng, mr, st), F32), pltpu.VMEM((mr, ng * st), BF16)],
        compiler_params=_cparams(("arbitrary", "arbitrary")),
        name="s5_rows",
    )(u4, w, sw, yw, a1, a2)


def _post_kernel(x_ref, gmod_ref, yssm_ref, u_ref, zs5_ref, o_ref, znsa_ref, merge_ref,
                 d_ref, gluw_ref, glub_ref, ws5_ref, wnsa_ref, wo_ref, xo_ref):
    y = jnp.concatenate([yssm_ref[ch] + d_ref[ch] * u_ref[ch] for ch in range(D_S5 // LANES)], axis=1)
    y = 0.5 * y * (1.0 + jnp.tanh(math.sqrt(2.0 / math.pi) * (y + 0.044715 * (y * y * y))))
    y = y * jax.nn.sigmoid(_dot(y.astype(BF16), gluw_ref[...]) + glub_ref[...])
    y = y * _silu(zs5_ref[...])
    b_s5 = _dot(y.astype(BF16), ws5_ref[...])
    b_nsa = _dot((o_ref[...] * _silu(znsa_ref[...])).astype(BF16), wnsa_ref[...])
    m = jax.nn.sigmoid(merge_ref[...])
    mix = m[:, :D_MODEL] * b_s5 + m[:, D_MODEL:] * b_nsa
    xo_ref[...] = x_ref[...] + gmod_ref[...] * _dot(mix.astype(BF16), wo_ref[...])


def _post(x2d, gmod, yssm, u, zs5, o, znsa, merge, d, gluw, glub, ws5, wnsa, wo,
          *, tm, tiles_per_b, per_row_mod):
    rows = x2d.shape[0]
    if per_row_mod:
        mod_spec = pl.BlockSpec((tm, D_MODEL), lambda i: (i, 0))
    else:
        mod_spec = pl.BlockSpec((None, 1, D_MODEL), lambda i: (i // tiles_per_b, 0, 0))
    row = lambda w: pl.BlockSpec((tm, w), lambda i: (i, 0))
    full = lambda a, b: pl.BlockSpec((a, b), lambda i: (0, 0))
    nch = D_S5 // LANES
    chunked = pl.BlockSpec((nch, tm, LANES), lambda i: (0, i, 0))
    return pl.pallas_call(
        _post_kernel,
        grid=(rows // tm,),
        in_specs=[row(D_MODEL), mod_spec, chunked, chunked, row(512), row(512), row(512), row(2048),
                  pl.BlockSpec((nch, 1, LANES), lambda i: (0, 0, 0)),
                  full(512, 512), full(1, 512), full(512, D_MODEL), full(512, D_MODEL),
                  full(D_MODEL, D_MODEL)],
        out_specs=row(D_MODEL),
        out_shape=jax.ShapeDtypeStruct((rows, D_MODEL), F32),
        compiler_params=_cparams(("arbitrary",)),
        name="post",
    )(x2d, gmod, yssm, u, zs5, o, znsa, merge, d, gluw, glub, ws5, wnsa, wo)


def _final_norm_kernel(x_ref, g_ref, o_ref):
    x = x_ref[...]
    o_ref[...] = x * lax.rsqrt(jnp.mean(x * x, axis=-1, keepdims=True) + RMS_EPS) * g_ref[...]


def _final_norm(x2d, g, tm):
    rows = x2d.shape[0]
    return pl.pallas_call(
        _final_norm_kernel,
        grid=(rows // tm,),
        in_specs=[pl.BlockSpec((tm, D_MODEL), lambda i: (i, 0)), pl.BlockSpec((1, D_MODEL), lambda i: (0, 0))],
        out_specs=pl.BlockSpec((tm, D_MODEL), lambda i: (i, 0)),
        out_shape=jax.ShapeDtypeStruct((rows, D_MODEL), F32),
        compiler_params=_cparams(("arbitrary",)),
        name="final_norm",
    )(x2d, g)


def _head_perm():
    idx = [HEAD_DIM * (GROUP_SIZE * g + r) + d
           for r in range(GROUP_SIZE) for g in range(N_KV) for d in range(HEAD_DIM)]
    return np.asarray(idx, np.int32)


def _rope_tables(pos, width=LANES):
    inv = ROPE_THETA ** (-jnp.arange(ROT_HALF, dtype=F32) / ROT_HALF)
    ang = pos.astype(F32)[:, None] * inv[None, :]
    cos, sin = jnp.cos(ang), jnp.sin(ang)
    n = pos.shape[0]
    one = jnp.ones((n, HEAD_DIM - 2 * ROT_HALF), F32)
    zero8 = jnp.zeros((n, ROT_HALF), F32)
    zero = jnp.zeros((n, HEAD_DIM - 2 * ROT_HALF), F32)
    c = jnp.concatenate([cos, cos, one], axis=1)
    sa = jnp.concatenate([-sin, zero8, zero], axis=1)
    sb = jnp.concatenate([zero8, sin, zero], axis=1)
    rep = width // HEAD_DIM
    return tuple(jnp.tile(t, (1, rep)) for t in (c, sa, sb))


def _blockdiag2(w):
    z = jnp.zeros_like(w)
    return jnp.concatenate([jnp.concatenate([w, z], -1), jnp.concatenate([z, w], -1)], -2)


def _s5_matrices(tabs, l, t_eff):
    ckr, cki, bkr, bki, m, pwr, pwi = [t[l] for t in tabs]
    g, c, L = S5_GROUPS, S5_GROUP, S5_CHUNK
    mk = m.reshape(g, L, c, c)
    s_idx = np.arange(L)[:, None]
    t_idx = np.arange(L)[None, :]
    lag = np.clip(t_idx - s_idx, 0, L - 1)
    kfull = mk[:, lag]
    kfull = jnp.where((t_idx >= s_idx)[None, :, :, None, None], kfull, 0.0)
    kmat = kfull.transpose(0, 1, 4, 2, 3).reshape(g, L * c, L * c).astype(BF16)
    ck = jnp.concatenate([ckr, cki], axis=-1)
    ymat = ck[:, c:, :].transpose(0, 2, 1).astype(BF16)
    bk = jnp.concatenate([bkr, bki], axis=-1).reshape(g, L, c, 2 * S5_STATE)
    sm = bk[:, t_eff - 1::-1] if t_eff == L else bk[:, np.arange(t_eff - 1, -1, -1)]
    if t_eff < L:
        sm = jnp.concatenate([sm, jnp.zeros((g, L - t_eff, c, 2 * S5_STATE), F32)], axis=1)
    smat = sm.reshape(g, L * c, 2 * S5_STATE).astype(BF16)
    ar = pwr[:, t_eff][:, None, :]
    ai = pwi[:, t_eff][:, None, :]
    a1 = jnp.concatenate([ar, ar], axis=-1)
    a2 = jnp.concatenate([-ai, ai], axis=-1)
    return kmat, smat, ymat, a1, a2


def _s5_row_weights(tabs, l):
    ckr, cki, bkr, bki, m, pwr, pwi = [t[l] for t in tabs]
    c, L, st = S5_GROUP, S5_CHUNK, 2 * S5_STATE
    ng = LANES // c
    nch = S5_GROUPS // ng
    eye = jnp.eye(ng, dtype=F32)
    mk = m.reshape(nch, ng, L, c, c)
    w = jnp.einsum('hgloi,gq->hlgiqo', mk, eye).reshape(nch, L * LANES, LANES)
    bk = jnp.concatenate([bkr, bki], axis=-1).reshape(nch, ng, L, c, st)
    sw = jnp.einsum('hgkcp,gq->hkgcqp', bk, eye).reshape(nch, L * LANES, ng * st)
    ck = jnp.concatenate([ckr, cki], axis=-1).reshape(nch, ng, L + 1, c, st)[:, :, 1:]
    yw = jnp.einsum('hgtcp,gq->hgptqc', ck, eye).reshape(nch, ng * st, L * LANES)
    ar = pwr[:, L].reshape(nch, ng, 1, S5_STATE)
    ai = pwi[:, L].reshape(nch, ng, 1, S5_STATE)
    a1 = jnp.concatenate([ar, ar], axis=-1)
    a2 = jnp.concatenate([-ai, ai], axis=-1)
    return w.astype(BF16), sw.astype(BF16), yw.astype(BF16), a1, a2


def _layer_weights(l, w_in, cmp_pe, cmp_w1, cmp_w2, s5_glu_w, w_s5_out, w_nsa_out, w_o, perm):
    wi = w_in[l]
    gate_w = jnp.pad(wi[:, 2304:2328], ((0, 0), (0, LANES - 3 * N_HEADS)))
    wp = jnp.concatenate([wi[:, :1024], wi[:, 1024:1536][:, perm], wi[:, 1536:2304],
                          wi[:, 2328:2840][:, perm], wi[:, 2840:], gate_w], axis=1).astype(BF16)
    pe = cmp_pe[l]
    pe2 = jnp.concatenate([pe, pe], axis=-1)[:, :, None, :]
    w1bd = _blockdiag2(cmp_w1[l].reshape(2, BLOCK, HEAD_DIM, HEAD_DIM)).astype(BF16)
    w2bd = _blockdiag2(cmp_w2[l]).astype(BF16)
    wt = wi[:, 1536:2304].T.astype(BF16)
    return dict(wp=wp, wt=wt, pe2=pe2, w1bd=w1bd, w2bd=w2bd, gluw=s5_glu_w[l].astype(BF16),
                ws5=w_s5_out[l].astype(BF16), wnsa=w_nsa_out[l][perm, :].astype(BF16),
                wo=w_o[l].astype(BF16))


def _to_groups(u2d, nbt, nj):
    u5 = u2d.reshape(nbt, nj, S5_CHUNK, S5_GROUPS, S5_GROUP)
    return u5.transpose(3, 1, 0, 2, 4).reshape(S5_GROUPS, nj * nbt, S5_CHUNK * S5_GROUP)


def _from_groups(y_g, nbt, nj):
    y5 = y_g.reshape(S5_GROUPS, nj, nbt, S5_CHUNK, S5_GROUP)
    return y5.transpose(2, 1, 3, 0, 4).reshape(nbt * nj * S5_CHUNK, D_S5)


def _state_to_groups(h):
    return h.transpose(2, 0, 1, 3).reshape(S5_GROUPS, h.shape[0], 2 * S5_STATE)


def _state_from_groups(hg):
    g, b, _ = hg.shape
    return hg.reshape(g, b, 2, S5_STATE).transpose(1, 2, 0, 3)


def kernel(x_prompt, x_sample, c_prompt, c_sample, cache_kv, page_table, state_win, state_ssm, ada_w, ada_b, norm_g, w_in, s5_a_re, s5_a_im, s5_log_dt, s5_b_re, s5_b_im, s5_c_re, s5_c_im, s5_d, s5_glu_w, s5_glu_b, cmp_pe, cmp_w1, cmp_w2, w_s5_out, w_nsa_out, w_o, final_g):
    bp, tp, _ = x_prompt.shape
    bs, ts, _ = x_sample.shape
    n_pool, _, page = cache_kv.shape[:3]
    n_pages = page_table.shape[1]
    past = n_pages * page
    wbuf = state_win.shape[2]
    assert tp % S5_CHUNK == 0 and tp % 128 == 0 and ts <= 8 and bs % 8 == 0 and bp % 8 == 0

    perm = _head_perm()
    mod = _ada_mod(jnp.concatenate([c_prompt, c_sample], axis=0), ada_w, ada_b)
    s5tabs = _s5_setup(s5_a_re, s5_a_im, s5_log_dt, s5_b_re, s5_b_im, s5_c_re, s5_c_im)

    tabs_p = _rope_tables(jnp.arange(tp))
    rs = bs * ts
    tabs_s = _rope_tables(jnp.tile(past + jnp.arange(ts), bs))
    nb_p = tp // BLOCK
    nbp_p = (nb_p + LANES - 1) // LANES * LANES
    ctabs_p = _rope_tables(jnp.arange(nbp_p) * BLOCK + (BLOCK - 1))
    nb_s = (past + ts + BLOCK - 1) // BLOCK
    nbp_s = ((nb_s + 7) // 8 * 8 + LANES - 1) // LANES * LANES
    ctabs_s = _rope_tables(jnp.arange(nbp_s) * BLOCK + (BLOCK - 1))

    cache5 = cache_kv.transpose(0, 1, 3, 4, 5, 2).reshape(n_pool, DEPTH, 4, KV_W, page)
    state_wint = state_win.transpose(0, 1, 3, 4, 5, 2).reshape(bs, DEPTH, 2, KV_W, wbuf)
    tabs_pt = tuple(tb.T for tb in tabs_p)
    tm_p = 256
    tq_p = 128
    nj_p = tp // S5_CHUNK
    consts = _attn_constants(tp, tq_p)
    keep = min(WINDOW, tp)

    xp = x_prompt.reshape(bp * tp, D_MODEL)
    xs = x_sample.reshape(rs, D_MODEL)
    kv_p, kv_s, win_p, win_s, ssm_p, ssm_s = [], [], [], [], [], []

    def pad_rows(a3):
        return jnp.pad(a3, ((0, 0), (0, 8 - ts), (0, 0)))

    def to_chunks(a2):
        return a2.reshape(a2.shape[0], D_S5 // LANES, LANES).transpose(1, 0, 2)

    def new_tiles(a3):
        a4 = a3.reshape(bs, ts, 2, KV_W).transpose(0, 2, 3, 1)
        return jnp.pad(a4, ((0, 0), (0, 0), (0, 0), (0, LANES - ts)))

    for l in range(DEPTH):
        lw = _layer_weights(l, w_in, cmp_pe, cmp_w1, cmp_w2, s5_glu_w, w_s5_out, w_nsa_out, w_o, perm)
        g_row = norm_g[l][None, :]
        d_row = s5_d[l].reshape(D_S5 // LANES, 1, LANES)
        glub = s5_glu_b[l][None, :]
        mp, ms = mod[l, :bp], mod[l, bp:]
        post_w = (d_row, lw['gluw'], glub, lw['ws5'], lw['wnsa'], lw['wo'])

        shift, scale, gmod = [mp[:, k * D_MODEL:(k + 1) * D_MODEL][:, None, :] for k in range(3)]
        u, zs5, q, kvt, wint, cmp, znsa, merge, gate = _inproj_prompt(
            xp, shift, scale, g_row, lw['wp'], lw['wt'], tabs_p, tabs_pt, tm=tm_p, nb_batch=bp, t=tp)
        kcvc = _cmp_prompt(cmp, bp, tp, nbp_p, lw['pe2'], lw['w1bd'], lw['w2bd'], ctabs_p)
        o = _nsa_prompt(q, kvt, wint, kcvc, gate, consts, bp, tp, tq_p)
        yssm, hl = _s5_rows(u, *_s5_row_weights(s5tabs, l), bp, tp)
        xp = _post(xp, gmod, yssm, u, zs5, o, znsa, merge, *post_w,
                   tm=tm_p, tiles_per_b=tp // tm_p, per_row_mod=False)
        kv_p.append(kvt)
        win_p.append(wint[:, :, tp - keep:])
        ssm_p.append(_state_from_groups(hl.reshape(S5_GROUPS, bp, 2 * S5_STATE)))

        shift, scale, gmod = [jnp.repeat(ms[:, k * D_MODEL:(k + 1) * D_MODEL], ts, axis=0) for k in range(3)]
        u, zs5, q, kv, win, znsa, merge, gate = _inproj(
            xs, shift, scale, g_row, lw['wp'], tabs_s, tm=rs, tiles_per_b=1, per_row_mod=True)
        kv3 = kv.reshape(bs, ts, 4 * KV_W)
        kcvc = _cmp_sample(page_table, cache5, pad_rows(kv3[:, :, :2 * KV_W]), l, nbp_s,
                           lw['pe2'], lw['w1bd'], lw['w2bd'], ctabs_s, ts)
        o8, wst = _nsa_sample(page_table, cache5, pad_rows(q.reshape(bs, ts, D_NSA)),
                              new_tiles(kv3[:, :, 2 * KV_W:]), new_tiles(win.reshape(bs, ts, 2 * KV_W)),
                              state_wint, kcvc, pad_rows(gate.reshape(bs, ts, LANES)), l, ts)
        o = o8[:, :ts].reshape(rs, D_NSA)
        mats = _s5_matrices(s5tabs, l, ts)
        u_pad = jnp.pad(u.reshape(bs, ts, D_S5), ((0, 0), (0, S5_CHUNK - ts), (0, 0)))
        y_g, hl = _s5_chunk(_to_groups(u_pad.reshape(bs * S5_CHUNK, D_S5), bs, 1).astype(BF16),
                            _state_to_groups(state_ssm[:, l]), *mats, nbt=bs)
        yssm = _from_groups(y_g, bs, 1).reshape(bs, S5_CHUNK, D_S5)[:, :ts].reshape(rs, D_S5)
        xs = _post(xs, gmod, to_chunks(yssm), to_chunks(u), zs5, o, znsa, merge, *post_w,
                   tm=rs, tiles_per_b=1, per_row_mod=True)
        kv_s.append(kv.reshape(bs, ts, 4, N_KV, HEAD_DIM))
        win_s.append(wst)
        ssm_s.append(_state_from_groups(hl))

    def from_t(parts, n_streams):
        a = jnp.stack(parts, axis=1)
        nbt, keys = a.shape[0], a.shape[-1]
        return a.reshape(nbt, DEPTH, n_streams, N_KV, HEAD_DIM, keys).transpose(0, 1, 5, 2, 3, 4)

    fg = final_g[None, :]
    y_prompt = _final_norm(xp, fg, tm_p).reshape(bp, tp, D_MODEL)
    y_sample = _final_norm(xs, fg, rs).reshape(bs, ts, D_MODEL)
    return (y_prompt, y_sample, from_t(kv_p, 4), jnp.stack(kv_s, axis=1),
            from_t(win_p, 2), from_t(win_s, 2),
            jnp.stack(ssm_p, axis=1), jnp.stack(ssm_s, axis=1))
```

```python
import functools
import math

import numpy as np
import jax
import jax.numpy as jnp
from jax import lax
from jax.experimental import pallas as pl
from jax.experimental.pallas import tpu as pltpu

F32 = jnp.float32
BF16 = jnp.bfloat16

D_MODEL = 1024
DEPTH = 4
D_S5 = 512
S5_GROUP = 16
S5_GROUPS = 32
S5_STATE = 64
D_NSA = 512
HEAD_DIM = 64
N_HEADS = 8
N_KV = 2
GROUP_SIZE = 4
BLOCK = 64
N_SELECT = 16
WINDOW = 512
ROT_HALF = 8
ROPE_THETA = 500000.0
RMS_EPS = 1e-6
NEG_INF = -1e30
FORCED_SCORE = 1e4

LANES = 128
S5_CHUNK = 16
KV_W = N_KV * HEAD_DIM
N_PROJ = 4992
CMP_PITCH = 72
VMEM_LIMIT = 56 * 1024 * 1024


def _dot(a, b):
    return jnp.dot(a, b, preferred_element_type=F32)


def _dot_nt(a, b):
    return lax.dot_general(a, b, (((1,), (1,)), ((), ())), preferred_element_type=F32)


def _silu(x):
    return x * jax.nn.sigmoid(x)


def _rope128(x, c, sa, sb):
    return x * c + pltpu.roll(x, LANES - ROT_HALF, 1) * sa + pltpu.roll(x, ROT_HALF, 1) * sb


def _masked_exp(s, mask):
    s = jnp.where(mask, s, NEG_INF)
    m = jnp.max(s, axis=-1, keepdims=True)
    p = jnp.where(mask, jnp.exp(s - m), 0.0)
    return p, jnp.sum(p, axis=-1, keepdims=True)


def _safe_inv(l):
    return jnp.where(l > 0.0, 1.0 / l, 0.0)


def _cparams(sem):
    return pltpu.CompilerParams(dimension_semantics=sem, vmem_limit_bytes=VMEM_LIMIT)


def _ada_kernel(c_ref, w_ref, b_ref, o_ref):
    c = c_ref[...]
    o_ref[...] = _dot(_silu(c).astype(BF16), w_ref[...].astype(BF16)) + b_ref[...]


def _ada_mod(c_all, ada_w, ada_b):
    nc = c_all.shape[0]
    tn = 1024
    return pl.pallas_call(
        _ada_kernel,
        grid=(DEPTH, 3 * D_MODEL // tn),
        in_specs=[pl.BlockSpec((nc, D_MODEL), lambda l, n: (0, 0)),
                  pl.BlockSpec((None, D_MODEL, tn), lambda l, n: (l, 0, n)),
                  pl.BlockSpec((None, 1, tn), lambda l, n: (l, 0, n))],
        out_specs=pl.BlockSpec((None, nc, tn), lambda l, n: (l, 0, n)),
        out_shape=jax.ShapeDtypeStruct((DEPTH, nc, 3 * D_MODEL), F32),
        compiler_params=_cparams(("arbitrary", "arbitrary")),
        name="ada_mod",
    )(c_all, ada_w, ada_b.reshape(DEPTH, 1, 3 * D_MODEL))


def _inproj_kernel(x_ref, shift_ref, scale_ref, g_ref, w_ref, cos_ref, sa_ref, sb_ref,
                   u_ref, zs5_ref, q_ref, kv_ref, win_ref, znsa_ref, merge_ref, gate_ref):
    x = x_ref[...]
    h = x * lax.rsqrt(jnp.mean(x * x, axis=-1, keepdims=True) + RMS_EPS) * g_ref[...]
    h = h * (1.0 + scale_ref[...]) + shift_ref[...]
    hb = h.astype(BF16)

    def mm(lo, hi):
        return _dot(hb, w_ref[:, lo:hi])

    c, sa, sb = cos_ref[...], sa_ref[...], sb_ref[...]
    u_ref[...] = mm(0, 512)
    zs5_ref[...] = mm(512, 1024)
    for r in range(GROUP_SIZE):
        lo = 1024 + r * LANES
        q = _rope128(mm(lo, lo + LANES), c, sa, sb) * (HEAD_DIM ** -0.5)
        q_ref[:, r * LANES:(r + 1) * LANES] = q.astype(BF16)
    kv_ref[:, 0:256] = mm(1536, 1792)
    kv_ref[:, 256:384] = _rope128(mm(1792, 1920), c, sa, sb)
    kv_ref[:, 384:512] = mm(1920, 2048)
    win_ref[:, 0:128] = _rope128(mm(2048, 2176), c, sa, sb)
    win_ref[:, 128:256] = mm(2176, 2304)
    znsa_ref[...] = mm(2304, 2816)
    merge_ref[...] = mm(2816, 4864)
    gate_ref[...] = mm(4864, 4992)


def _inproj(x2d, shift, scale, g, wp, tabs, *, tm, tiles_per_b, per_row_mod):
    rows = x2d.shape[0]
    nt = rows // tm
    n_tab = tabs[0].shape[0] // tm
    if per_row_mod:
        mod_spec = pl.BlockSpec((tm, D_MODEL), lambda i: (i, 0))
    else:
        mod_spec = pl.BlockSpec((None, 1, D_MODEL), lambda i: (i // tiles_per_b, 0, 0))
    tab_spec = pl.BlockSpec((tm, LANES), lambda i: (i % n_tab, 0))

    def row_spec(w):
        return pl.BlockSpec((tm, w), lambda i: (i, 0))

    widths = (512, 512, 512, 512, 256, 512, 2048, 128)
    dtypes = (F32, F32, BF16, F32, F32, F32, F32, F32)
    return pl.pallas_call(
        _inproj_kernel,
        grid=(nt,),
        in_specs=[row_spec(D_MODEL), mod_spec, mod_spec,
                  pl.BlockSpec((1, D_MODEL), lambda i: (0, 0)),
                  pl.BlockSpec((D_MODEL, N_PROJ), lambda i: (0, 0)),
                  tab_spec, tab_spec, tab_spec],
        out_specs=[row_spec(w) for w in widths],
        out_shape=[jax.ShapeDtypeStruct((rows, w), d) for w, d in zip(widths, dtypes)],
        compiler_params=_cparams(("arbitrary",)),
        name="inproj",
    )(x2d, shift, scale, g, wp, *tabs)


def _rope128_t(x, c, sa, sb):
    return x * c + pltpu.roll(x, KV_W - ROT_HALF, 0) * sa + pltpu.roll(x, ROT_HALF, 0) * sb


def _inproj_prompt_kernel(x_ref, shift_ref, scale_ref, g_ref, w_ref, wt_ref, cos_ref, sa_ref, sb_ref,
                          cost_ref, sat_ref, sbt_ref,
                          u_ref, zs5_ref, q_ref, kvt_ref, wint_ref, cmp_ref, znsa_ref, merge_ref, gate_ref):
    x = x_ref[...]
    h = x * lax.rsqrt(jnp.mean(x * x, axis=-1, keepdims=True) + RMS_EPS) * g_ref[...]
    h = h * (1.0 + scale_ref[...]) + shift_ref[...]
    hb = h.astype(BF16)

    def mm(lo, hi):
        return _dot(hb, w_ref[:, lo:hi])

    c, sa, sb = cos_ref[...], sa_ref[...], sb_ref[...]
    for ch in range(D_S5 // LANES):
        u_ref[ch] = mm(ch * LANES, (ch + 1) * LANES)
    zs5_ref[...] = mm(512, 1024)
    for r in range(GROUP_SIZE):
        lo = 1024 + r * LANES
        q = _rope128(mm(lo, lo + LANES), c, sa, sb) * (HEAD_DIM ** -0.5)
        q_ref[:, r * LANES:(r + 1) * LANES] = q.astype(BF16)
    cmp_ref[...] = mm(1536, 1792)
    znsa_ref[...] = mm(2304, 2816)
    merge_ref[...] = mm(2816, 4864)
    gate_ref[...] = mm(4864, 4992)

    ct, sat, sbt = cost_ref[...], sat_ref[...], sbt_ref[...]

    def mmt(s):
        return _dot_nt(wt_ref[s * KV_W:(s + 1) * KV_W, :], hb)

    kvt_ref[0:KV_W, :] = mmt(0)
    kvt_ref[KV_W:2 * KV_W, :] = mmt(1)
    kvt_ref[2 * KV_W:3 * KV_W, :] = _rope128_t(mmt(2), ct, sat, sbt)
    kvt_ref[3 * KV_W:4 * KV_W, :] = mmt(3)
    wint_ref[0:KV_W, :] = _rope128_t(mmt(4), ct, sat, sbt)
    wint_ref[KV_W:2 * KV_W, :] = mmt(5)


def _inproj_prompt(x2d, shift, scale, g, wp, wt, tabs, tabs_t, *, tm, nb_batch, t):
    rows = x2d.shape[0]
    tpb = t // tm
    mod_spec = pl.BlockSpec((None, 1, D_MODEL), lambda i: (i // tpb, 0, 0))
    tab_spec = pl.BlockSpec((tm, LANES), lambda i: (i % tpb, 0))
    tabt_spec = pl.BlockSpec((KV_W, tm), lambda i: (0, i % tpb))
    row = lambda w: pl.BlockSpec((tm, w), lambda i: (i, 0))
    tr = lambda h: pl.BlockSpec((None, h, tm), lambda i: (i // tpb, 0, i % tpb))
    u_spec = pl.BlockSpec((D_S5 // LANES, tm, LANES), lambda i: (0, i, 0))
    outs = [((D_S5 // LANES, rows, LANES), F32, u_spec), ((rows, 512), F32, row(512)),
            ((rows, 512), BF16, row(512)),
            ((nb_batch, 4 * KV_W, t), F32, tr(4 * KV_W)), ((nb_batch, 2 * KV_W, t), F32, tr(2 * KV_W)),
            ((rows, 256), F32, row(256)), ((rows, 512), F32, row(512)), ((rows, 2048), F32, row(2048)),
            ((rows, 128), F32, row(128))]
    return pl.pallas_call(
        _inproj_prompt_kernel,
        grid=(rows // tm,),
        in_specs=[row(D_MODEL), mod_spec, mod_spec,
                  pl.BlockSpec((1, D_MODEL), lambda i: (0, 0)),
                  pl.BlockSpec((D_MODEL, N_PROJ), lambda i: (0, 0)),
                  pl.BlockSpec((6 * KV_W, D_MODEL), lambda i: (0, 0)),
                  tab_spec, tab_spec, tab_spec, tabt_spec, tabt_spec, tabt_spec],
        out_specs=[o[2] for o in outs],
        out_shape=[jax.ShapeDtypeStruct(o[0], o[1]) for o in outs],
        compiler_params=_cparams(("arbitrary",)),
        name="inproj_prompt",
    )(x2d, shift, scale, g, wp, wt, *tabs, *tabs_t)


def _compress(load_p, nbk, pe_ref, w1_ref, w2_ref, c, sa, sb):
    outs = []
    for s in range(2):
        acc = jnp.zeros((nbk, KV_W), F32)
        for p in range(BLOCK):
            xp = load_p(s, p) + pe_ref[s, p]
            acc = acc + _dot(xp.astype(BF16), w1_ref[s, p])
        outs.append(_dot(_silu(acc).astype(BF16), w2_ref[s]))
    return jnp.concatenate([_rope128(outs[0], c, sa, sb), outs[1]], axis=1)


def _cmp_prompt_kernel(k_ref, v_ref, pe_ref, w1_ref, w2_ref, cos_ref, sa_ref, sb_ref, o_ref, *, nbk):
    def load_p(s, p):
        return (k_ref, v_ref)[s][pl.ds(p, nbk, stride=BLOCK), :]

    res = _compress(load_p, nbk, pe_ref, w1_ref, w2_ref,
                    cos_ref[0:nbk, :], sa_ref[0:nbk, :], sb_ref[0:nbk, :])
    o_ref[...] = jnp.zeros(o_ref.shape, F32)
    o_ref[0:nbk, :] = res


def _cmp_prompt(kv2d, nb_batch, t, nbp, pe2, w1bd, w2bd, ctabs):
    nbk = t // BLOCK
    full = lambda shape: pl.BlockSpec(shape, lambda b: (0,) * len(shape))
    return pl.pallas_call(
        functools.partial(_cmp_prompt_kernel, nbk=nbk),
        grid=(nb_batch,),
        in_specs=[pl.BlockSpec((t, KV_W), lambda b: (b, 0)), pl.BlockSpec((t, KV_W), lambda b: (b, 1)),
                  full((2, BLOCK, 1, KV_W)), full((2, BLOCK, KV_W, KV_W)),
                  full((2, KV_W, KV_W)), full((nbp, LANES)), full((nbp, LANES)), full((nbp, LANES))],
        out_specs=pl.BlockSpec((None, nbp, 2 * KV_W), lambda b: (b, 0, 0)),
        out_shape=jax.ShapeDtypeStruct((nb_batch, nbp, 2 * KV_W), F32),
        compiler_params=_cparams(("arbitrary",)),
        name="cmp_prompt",
    )(kv2d, kv2d, pe2, w1bd, w2bd, *ctabs)


def _stack_queries(qt, tq):
    lane = lax.broadcasted_iota(jnp.int32, (tq, LANES), 1)
    qt = qt.astype(F32)
    blocks = []
    for g in range(N_KV):
        keep = (lane < HEAD_DIM) if g == 0 else (lane >= HEAD_DIM)
        for r in range(GROUP_SIZE):
            blocks.append(jnp.where(keep, qt[:, r * LANES:(r + 1) * LANES], 0.0))
    return jnp.concatenate(blocks, axis=0).astype(BF16)


def _select_blocks(imp, qblk, nb, n_sel):
    nq, nbp = imp.shape
    n = lax.broadcasted_iota(jnp.int32, (nq, nbp), 1)
    forced = (n == 0) | (n == qblk) | (n == qblk - 1)
    imp = jnp.where(forced, FORCED_SCORE, imp)
    imp = jnp.where(n > qblk, -1.0, imp)
    imp = jnp.where(n >= nb, -2.0, imp)
    rank = jnp.zeros((nq, nbp), F32)
    for m in range(nb):
        col = imp[:, m:m + 1]
        beats = (col > imp) | ((col == imp) & (n > m))
        rank = rank + jnp.where(beats, 1.0, 0.0)
    return jnp.where((rank < n_sel) & (imp > -0.5), 1.0, 0.0)


def _compressed_branch(qs, kcvc, tq_col, tq, nb):
    nbp = kcvc.shape[0]
    kc = kcvc[:, :KV_W].astype(BF16)
    vc = kcvc[:, KV_W:].astype(BF16)
    s = _dot_nt(qs, kc)
    n = lax.broadcasted_iota(jnp.int32, (1, nbp), 1)
    mask = (n * BLOCK + (BLOCK - 1) <= tq_col) & (n < nb)
    p, l = _masked_exp(s, mask)
    p = p * _safe_inv(l)
    o_c = _dot(p.astype(BF16), vc)
    imps = []
    for g in range(N_KV):
        acc = p[(g * GROUP_SIZE) * tq:(g * GROUP_SIZE + 1) * tq]
        for r in range(1, GROUP_SIZE):
            acc = acc + p[(g * GROUP_SIZE + r) * tq:(g * GROUP_SIZE + r + 1) * tq]
        imps.append(acc)
    return o_c, imps


def _block_mask(selstack, k0, tk):
    nbp = selstack.shape[1]
    kblk = (k0 + lax.broadcasted_iota(jnp.int32, (nbp, tk), 1)) // BLOCK
    e = jnp.where(kblk == lax.broadcasted_iota(jnp.int32, (nbp, tk), 0), 1.0, 0.0).astype(BF16)
    return _dot(selstack, e) > 0.5


def _gate_cols(gt, tq, branch):
    cols = []
    for g in range(N_KV):
        for r in range(GROUP_SIZE):
            j = branch * N_HEADS + g * GROUP_SIZE + r
            cols.append(gt[:, j:j + 1])
    return jnp.concatenate(cols, axis=0)


def _unstack_heads(o, tq):
    lane = lax.broadcasted_iota(jnp.int32, (tq, LANES), 1)
    chunks = []
    for r in range(GROUP_SIZE):
        a = o[r * tq:(r + 1) * tq]
        b = o[(GROUP_SIZE + r) * tq:(GROUP_SIZE + r + 1) * tq]
        chunks.append(jnp.where(lane < HEAD_DIM, a, b))
    return chunks


def _select_blocks_t(imp, qblk_row, nb, n_sel):
    nq, nbp = imp.shape
    nbr = (nb + 7) // 8 * 8
    x = imp.T[0:nbr, :]
    n = lax.broadcasted_iota(jnp.int32, (nbr, nq), 0)
    forced = (n == 0) | (n == qblk_row) | (n == qblk_row - 1)
    x = jnp.where(forced, FORCED_SCORE, x)
    x = jnp.where(n > qblk_row, -1.0, x)
    x = jnp.where(n >= nb, -2.0, x)
    rank = jnp.zeros((nbr, nq), F32)
    for m in range(nb):
        row = x[m:m + 1, :]
        beats = (row > x) | ((row == x) & (n > m))
        rank = rank + jnp.where(beats, 1.0, 0.0)
    sel = jnp.where((rank < n_sel) & (x > -0.5), 1.0, 0.0)
    if nbr < nbp:
        sel = jnp.concatenate([sel, jnp.zeros((nbp - nbr, nq), F32)], axis=0)
    return sel.T


def _nsa_prompt_kernel(q_ref, kvt_ref, wint_ref, kcvc_ref, gate_ref, e_ref, cbias_ref, wbias_ref,
                       o_ref, kaug_sc, vsel_sc, waug_sc, vwin_sc, s_sc, m_sc, l_sc, acc_sc, sel_sc,
                       *, tq, nb, n_sel):
    i = pl.program_id(1)
    q0 = pl.multiple_of(i * tq, tq)
    rows = N_HEADS * tq
    wk = WINDOW + tq
    tk = 2 * LANES
    neg = jnp.asarray(NEG_INF, F32)

    @pl.when(i == 0)
    def _():
        kaug_sc[0:KV_W, :] = kvt_ref[0:KV_W, :].astype(BF16)
        kaug_sc[KV_W:2 * KV_W, :] = e_ref[...]
        vsel_sc[...] = kvt_ref[KV_W:2 * KV_W, :].astype(BF16)
        waug_sc[...] = jnp.zeros(waug_sc.shape, BF16)
        waug_sc[0:KV_W, WINDOW:] = wint_ref[0:KV_W, :].astype(BF16)
        waug_sc[KV_W:KV_W + 16, 0:WINDOW] = jnp.ones((16, WINDOW), BF16)
        vwin_sc[:, 0:WINDOW] = jnp.zeros((KV_W, WINDOW), BF16)
        vwin_sc[:, WINDOW:] = wint_ref[KV_W:2 * KV_W, :].astype(BF16)

    qs = _stack_queries(q_ref[...], tq)
    t_loc = lax.broadcasted_iota(jnp.int32, (rows, 1), 0) % tq
    tq_col = q0 + t_loc

    o_c, imps = _compressed_branch(qs, kcvc_ref[...], tq_col, tq, nb)

    sel_sc[...] = jnp.ones(sel_sc.shape, F32)

    @pl.when(q0 + tq > n_sel * BLOCK)
    def _():
        qblk_row = (q0 + lax.broadcasted_iota(jnp.int32, (1, tq), 1)) // BLOCK
        for g in range(N_KV):
            sel_sc[g] = _select_blocks_t(imps[g], qblk_row, nb, n_sel)

    selb = [((sel_sc[g] - 1.0) * (-NEG_INF)).astype(BF16) for g in range(N_KV)]
    selstack = jnp.concatenate([selb[g] for g in range(N_KV) for _ in range(GROUP_SIZE)], axis=0)
    qaug = jnp.concatenate([qs, selstack], axis=1)

    n_full = q0 // tk
    par = (q0 // tq) % 2
    m_sc[...] = jnp.full(m_sc.shape, neg, F32)

    def score_tile(k0, w, bias):
        s = _dot(qaug, kaug_sc[:, pl.ds(k0, w)])
        if bias is not None:
            s = s + bias
        s_sc[:, pl.ds(k0, w)] = s
        mx = s[:, :LANES]
        for c in range(1, w // LANES):
            mx = jnp.maximum(mx, s[:, c * LANES:(c + 1) * LANES])
        m_sc[...] = jnp.maximum(m_sc[...], mx)

    def pass1(j, carry):
        score_tile(pl.multiple_of(j * 2 * tk, 2 * tk), 2 * tk, None)
        return carry

    lax.fori_loop(0, n_full // 2, pass1, 0)

    @pl.when(n_full % 2 == 1)
    def _():
        score_tile(pl.multiple_of((n_full - 1) * tk, tk), tk, None)

    score_tile(pl.multiple_of(n_full * tk, tk), tk, cbias_ref[par])
    m_b = jnp.broadcast_to(jnp.max(m_sc[...], axis=-1, keepdims=True), (rows, LANES))

    l_sc[...] = jnp.zeros(l_sc.shape, F32)
    acc_sc[...] = jnp.zeros(acc_sc.shape, F32)

    def pv_tile(k0, w):
        s = s_sc[:, pl.ds(k0, w)]
        ps = [jnp.exp(s[:, c * LANES:(c + 1) * LANES] - m_b) for c in range(w // LANES)]
        tot = ps[0]
        for pc in ps[1:]:
            tot = tot + pc
        l_sc[...] = l_sc[...] + tot
        p = jnp.concatenate(ps, axis=1).astype(BF16)
        acc_sc[...] = acc_sc[...] + _dot_nt(p, vsel_sc[:, pl.ds(k0, w)])

    def pass2(j, carry):
        pv_tile(pl.multiple_of(j * 2 * tk, 2 * tk), 2 * tk)
        return carry

    lax.fori_loop(0, (n_full + 1) // 2, pass2, 0)

    @pl.when(n_full % 2 == 0)
    def _():
        pv_tile(pl.multiple_of(n_full * tk, tk), tk)
    o_s = acc_sc[...] * _safe_inv(jnp.sum(l_sc[...], axis=-1, keepdims=True))

    lane = lax.broadcasted_iota(jnp.int32, (rows, LANES), 1)
    padcol = jnp.where(lane == 0, neg, 0.0).astype(BF16)
    qaug_w = jnp.concatenate([qs, padcol], axis=1)
    s = _dot(qaug_w, waug_sc[:, pl.ds(q0, wk)]) + wbias_ref[...]
    p = jnp.exp(s - jnp.max(s, axis=-1, keepdims=True))
    l = jnp.sum(p, axis=-1, keepdims=True)
    o_w = _dot_nt(p.astype(BF16), vwin_sc[:, pl.ds(q0, wk)]) * _safe_inv(l)

    gt = jax.nn.sigmoid(gate_ref[...])
    o = _gate_cols(gt, tq, 0) * o_c + _gate_cols(gt, tq, 1) * o_s + _gate_cols(gt, tq, 2) * o_w
    for r, ch in enumerate(_unstack_heads(o, tq)):
        o_ref[:, r * LANES:(r + 1) * LANES] = ch


def _attn_constants(t, tq):
    rows = N_HEADS * tq
    t_loc = np.arange(rows)[:, None] % tq
    k = np.arange(2 * LANES)[None, :]
    cb = np.stack([np.where(k <= t_loc, 0.0, NEG_INF), np.where(k - LANES <= t_loc, 0.0, NEG_INF)])
    kr = np.arange(WINDOW + tq)[None, :]
    wb = np.where((kr - WINDOW <= t_loc) & (kr > t_loc), 0.0, NEG_INF)
    e = (np.arange(t)[None, :] // BLOCK == np.arange(LANES)[:, None]).astype(np.float32)
    return jnp.asarray(e, BF16), jnp.asarray(cb, F32), jnp.asarray(wb, F32)


def _nsa_prompt(q2d, kvt, wint, kcvc, gate2d, consts, nb_batch, t, tq):
    nb = t // BLOCK
    nq = t // tq
    nbp = kcvc.shape[1]
    rows = N_HEADS * tq
    wk = WINDOW + tq
    assert nbp == LANES and t % (2 * LANES) == 0
    e, cb, wb = consts
    kern = functools.partial(_nsa_prompt_kernel, tq=tq, nb=nb, n_sel=min(N_SELECT, nb))
    return pl.pallas_call(
        kern,
        grid=(nb_batch, nq),
        in_specs=[pl.BlockSpec((tq, D_NSA), lambda b, i: (b * nq + i, 0)),
                  pl.BlockSpec((None, 2 * KV_W, t), lambda b, i: (b, 1, 0)),
                  pl.BlockSpec((None, 2 * KV_W, t), lambda b, i: (b, 0, 0)),
                  pl.BlockSpec((None, nbp, 2 * KV_W), lambda b, i: (b, 0, 0)),
                  pl.BlockSpec((tq, LANES), lambda b, i: (b * nq + i, 0)),
                  pl.BlockSpec((LANES, t), lambda b, i: (0, 0)),
                  pl.BlockSpec((2, rows, 2 * LANES), lambda b, i: (0, 0, 0)),
                  pl.BlockSpec((rows, wk), lambda b, i: (0, 0))],
        out_specs=pl.BlockSpec((tq, D_NSA), lambda b, i: (b * nq + i, 0)),
        out_shape=jax.ShapeDtypeStruct((nb_batch * t, D_NSA), F32),
        scratch_shapes=[pltpu.VMEM((2 * KV_W, t), BF16), pltpu.VMEM((KV_W, t), BF16),
                        pltpu.VMEM((2 * KV_W, WINDOW + t), BF16), pltpu.VMEM((KV_W, WINDOW + t), BF16),
                        pltpu.VMEM((rows, t), F32), pltpu.VMEM((rows, LANES), F32),
                        pltpu.VMEM((rows, LANES), F32), pltpu.VMEM((rows, KV_W), F32),
                        pltpu.VMEM((N_KV, tq, LANES), F32)],
        compiler_params=_cparams(("arbitrary", "arbitrary")),
        name="nsa_prompt",
    )(q2d, kvt, wint, kcvc, gate2d, e, cb, wb)


def _page_copy(cache_ref, slabs, sem_ref, pt_ref, layer, b, p, slot, s0, page, s):
    return pltpu.make_async_copy(
        cache_ref.at[pt_ref[b, p], layer, s0 + s],
        slabs[s].at[slot, :, pl.ds(p * page, page)],
        sem_ref.at[slot])


def _gather_pages(cache_ref, slabs, sem_ref, pt_ref, layer, s0, n_pages, page):
    b = pl.program_id(0)
    nb_batch = pl.num_programs(0)
    slot = b % 2

    def copies(bb, sl):
        return [_page_copy(cache_ref, slabs, sem_ref, pt_ref, layer, bb, p, sl, s0, page, s)
                for p in range(n_pages) for s in range(2)]

    @pl.when(b == 0)
    def _():
        for cp in copies(0, 0):
            cp.start()

    @pl.when(b + 1 < nb_batch)
    def _():
        for cp in copies(b + 1, 1 - slot):
            cp.start()

    for cp in copies(b, slot):
        cp.wait()
    return slot


def _cmp_sample_kernel(pt_ref, cache_ref, new_ref, pe_ref, w1_ref, w2_ref, cos_ref, sa_ref, sb_ref,
                       o_ref, stagek_ref, stagev_ref, rowk_ref, rowv_ref, sem_ref,
                       *, layer, n_pages, page, nbk):
    stages = (stagek_ref, stagev_ref)
    rowm = (rowk_ref, rowv_ref)
    slot = _gather_pages(cache_ref, stages, sem_ref, pt_ref, layer, 0, n_pages, page)
    bpp = page // BLOCK
    nb_past = n_pages * bpp
    unroll = next(k for k in (8, 4, 2, 1) if n_pages % k == 0)
    for s in range(2):
        def xpose(i, carry):
            for k in range(unroll):
                p = i * unroll + k
                c0 = pl.multiple_of(p * page, page)
                tile = stages[s][slot, :, pl.ds(c0, page)].T
                for h in range(bpp):
                    r0 = pl.multiple_of((p * bpp + h) * CMP_PITCH, 8)
                    rowm[s][pl.ds(r0, BLOCK), :] = tile[h * BLOCK:(h + 1) * BLOCK, :]
            return carry

        lax.fori_loop(0, n_pages // unroll, xpose, 0)
        tail0 = nb_past * CMP_PITCH
        tail = rowm[s].shape[0] - tail0
        rowm[s][pl.ds(tail0, tail), :] = jnp.zeros((tail, KV_W), F32)
        rowm[s][pl.ds(tail0, 8), :] = new_ref[:, s * KV_W:(s + 1) * KV_W]

    def load_p(s, p):
        return rowm[s][pl.ds(p, nbk, stride=CMP_PITCH), :]

    res = _compress(load_p, nbk, pe_ref, w1_ref, w2_ref,
                    cos_ref[0:nbk, :], sa_ref[0:nbk, :], sb_ref[0:nbk, :])
    o_ref[...] = jnp.zeros(o_ref.shape, F32)
    o_ref[0:nbk, :] = res


def _cmp_sample(page_table, cache5, newcmp, layer, nbp, pe2, w1bd, w2bd, ctabs, t_new):
    nb_batch, n_pages = page_table.shape
    page = cache5.shape[4]
    past = n_pages * page
    nbk = ((past + t_new + BLOCK - 1) // BLOCK + 7) // 8 * 8
    slab_rows = nbk * CMP_PITCH
    assert page % BLOCK == 0 and t_new <= 8
    full = lambda shape: pl.BlockSpec(shape, lambda b, pt: (0,) * len(shape))
    kern = functools.partial(_cmp_sample_kernel, layer=layer, n_pages=n_pages, page=page, nbk=nbk)
    return pl.pallas_call(
        kern,
        grid_spec=pltpu.PrefetchScalarGridSpec(
            num_scalar_prefetch=1,
            grid=(nb_batch,),
            in_specs=[pl.BlockSpec(memory_space=pl.ANY),
                      pl.BlockSpec((None, 8, 2 * KV_W), lambda b, pt: (b, 0, 0)),
                      full((2, BLOCK, 1, KV_W)), full((2, BLOCK, KV_W, KV_W)),
                      full((2, KV_W, KV_W)), full((nbp, LANES)), full((nbp, LANES)),
                      full((nbp, LANES))],
            out_specs=pl.BlockSpec((None, nbp, 2 * KV_W), lambda b, pt: (b, 0, 0)),
            scratch_shapes=[pltpu.VMEM((2, KV_W, past), F32), pltpu.VMEM((2, KV_W, past), F32),
                            pltpu.VMEM((slab_rows, KV_W), F32), pltpu.VMEM((slab_rows, KV_W), F32),
                            pltpu.SemaphoreType.DMA((2,))]),
        out_shape=jax.ShapeDtypeStruct((nb_batch, nbp, 2 * KV_W), F32),
        compiler_params=_cparams(("arbitrary",)),
        name="cmp_sample",
    )(page_table, cache5, newcmp, pe2, w1bd, w2bd, *ctabs)


def _nsa_sample_kernel(pt_ref, cache_ref, q_ref, newt_ref, wnewt_ref, swint_ref, kcvc_ref, gate_ref,
                       o_ref, wout_ref, slabk_ref, slabv_ref, wslab_ref, sem_ref,
                       *, layer, n_pages, page, t_new, nb, n_sel):
    tq = 8
    past = n_pages * page
    rows = N_HEADS * tq
    slabs = (slabk_ref, slabv_ref)
    slot = _gather_pages(cache_ref, slabs, sem_ref, pt_ref, layer, 2, n_pages, page)
    nk = slabk_ref.shape[2]
    for s in range(2):
        slabs[s][slot, :, pl.ds(past, LANES)] = newt_ref[s]

    qs = _stack_queries(q_ref[...], tq)
    t_loc = lax.broadcasted_iota(jnp.int32, (rows, 1), 0) % tq
    tq_col = past + t_loc

    o_c, imps = _compressed_branch(qs, kcvc_ref[...], tq_col, tq, nb)
    qblk = (past + lax.broadcasted_iota(jnp.int32, (tq, 1), 0)) // BLOCK
    sels = [_select_blocks(imps[g], qblk, nb, n_sel).astype(BF16) for g in range(N_KV)]
    selstack = jnp.concatenate([sels[g] for g in range(N_KV) for _ in range(GROUP_SIZE)], axis=0)

    s = _dot(qs, slabk_ref[slot].astype(BF16))
    kpos = lax.broadcasted_iota(jnp.int32, (1, nk), 1)
    p, l = _masked_exp(s, _block_mask(selstack, 0, nk) & (kpos <= tq_col))
    o_s = _dot_nt(p.astype(BF16), slabv_ref[slot].astype(BF16)) * _safe_inv(l)

    wbuf = swint_ref.shape[2]
    wk = wslab_ref.shape[2]
    wslab_ref[:, :, 0:wbuf] = swint_ref[...]
    wslab_ref[:, :, wbuf:wk] = wnewt_ref[...]
    s = _dot(qs, wslab_ref[0].astype(BF16))
    kpos = past - wbuf + lax.broadcasted_iota(jnp.int32, (1, wk), 1)
    p, l = _masked_exp(s, (kpos <= tq_col) & (kpos > tq_col - WINDOW) & (kpos >= 0))
    o_w = _dot_nt(p.astype(BF16), wslab_ref[1].astype(BF16)) * _safe_inv(l)
    wout_ref[...] = wslab_ref[:, :, t_new:t_new + wbuf]

    gt = jax.nn.sigmoid(gate_ref[...])
    o = _gate_cols(gt, tq, 0) * o_c + _gate_cols(gt, tq, 1) * o_s + _gate_cols(gt, tq, 2) * o_w
    for r, ch in enumerate(_unstack_heads(o, tq)):
        o_ref[:, r * LANES:(r + 1) * LANES] = ch


def _nsa_sample(page_table, cache5, q3, newselt, wnewt, state_wint, kcvc, gate3, layer, t_new):
    nb_batch, n_pages = page_table.shape
    page = cache5.shape[4]
    past = n_pages * page
    nb = (past + t_new + BLOCK - 1) // BLOCK
    nk = past + LANES
    wbuf = state_wint.shape[4]
    wk = wbuf + LANES
    nbp = kcvc.shape[1]
    kern = functools.partial(_nsa_sample_kernel, layer=layer, n_pages=n_pages, page=page,
                             t_new=t_new, nb=nb, n_sel=min(N_SELECT, nb))
    per_b = lambda d1, d2: pl.BlockSpec((None, d1, d2), lambda b, pt: (b, 0, 0))
    per_b4 = lambda d1, d2, d3: pl.BlockSpec((None, d1, d2, d3), lambda b, pt: (b, 0, 0, 0))
    return pl.pallas_call(
        kern,
        grid_spec=pltpu.PrefetchScalarGridSpec(
            num_scalar_prefetch=1,
            grid=(nb_batch,),
            in_specs=[pl.BlockSpec(memory_space=pl.ANY),
                      per_b(8, D_NSA), per_b4(2, KV_W, LANES), per_b4(2, KV_W, LANES),
                      pl.BlockSpec((None, None, 2, KV_W, wbuf), lambda b, pt: (b, layer, 0, 0, 0)),
                      per_b(nbp, 2 * KV_W), per_b(8, LANES)],
            out_specs=[per_b(8, D_NSA), per_b4(2, KV_W, wbuf)],
            scratch_shapes=[pltpu.VMEM((2, KV_W, nk), F32), pltpu.VMEM((2, KV_W, nk), F32),
                            pltpu.VMEM((2, KV_W, wk), F32), pltpu.SemaphoreType.DMA((2,))]),
        out_shape=[jax.ShapeDtypeStruct((nb_batch, 8, D_NSA), F32),
                   jax.ShapeDtypeStruct((nb_batch, 2, KV_W, wbuf), F32)],
        compiler_params=_cparams(("arbitrary",)),
        name="nsa_sample",
    )(page_table, cache5, q3, newselt, wnewt, state_wint, kcvc, gate3)


def _s5_setup_kernel(are_ref, aim_ref, ldt_ref, bre_ref, bim_ref, cre_ref, cim_ref,
                     ckr_ref, cki_ref, bkr_ref, bki_ref, m_ref, pwr_ref, pwi_ref):
    nk = pwr_ref.shape[1]

    def body(g, carry):
        ar = are_ref[g]
        ai = aim_ref[g]
        dt = jnp.exp(ldt_ref[g])
        mag = jnp.exp(ar * dt)
        abr = mag * jnp.cos(ai * dt)
        abi = mag * jnp.sin(ai * dt)
        den = ar * ar + ai * ai
        nr = abr - 1.0
        e_re = (nr * ar + abi * ai) / den
        e_im = (abi * ar - nr * ai) / den
        bre = bre_ref[g]
        bim = bim_ref[g]
        bbr = e_re * bre - e_im * bim
        bbi = e_re * bim + e_im * bre
        kk = lax.broadcasted_iota(jnp.int32, (nk, S5_STATE), 0).astype(F32)
        pmag = jnp.exp(kk * (ar * dt))
        pwr = pmag * jnp.cos(kk * (ai * dt))
        pwi = pmag * jnp.sin(kk * (ai * dt))
        pwr_ref[g] = pwr
        pwi_ref[g] = pwi
        cre = cre_ref[g]
        cim = cim_ref[g]
        for k in range(S5_CHUNK + 1):
            wr = pwr[k:k + 1, :]
            wi = pwi[k:k + 1, :]
            ckr_ref[g, k * S5_GROUP:(k + 1) * S5_GROUP, :] = cre * wr - cim * wi
            cki_ref[g, k * S5_GROUP:(k + 1) * S5_GROUP, :] = -(cre * wi + cim * wr)
            if k < S5_CHUNK:
                bkr_ref[g, k * S5_GROUP:(k + 1) * S5_GROUP, :] = bbr * wr - bbi * wi
                bki_ref[g, k * S5_GROUP:(k + 1) * S5_GROUP, :] = bbr * wi + bbi * wr
        nl = S5_CHUNK * S5_GROUP
        hp = lax.Precision.HIGHEST
        m_ref[g] = (lax.dot_general(ckr_ref[g, 0:nl, :], bbr, (((1,), (1,)), ((), ())),
                                    precision=hp, preferred_element_type=F32)
                    + lax.dot_general(cki_ref[g, 0:nl, :], bbi, (((1,), (1,)), ((), ())),
                                      precision=hp, preferred_element_type=F32))
        return carry

    lax.fori_loop(0, S5_GROUPS, body, 0)


def _s5_setup(a_re, a_im, log_dt, b_re, b_im, c_re, c_im):
    g, p, c = S5_GROUPS, S5_STATE, S5_GROUP
    nk = 24
    outs = [(DEPTH, g, (S5_CHUNK + 1) * c, p), (DEPTH, g, (S5_CHUNK + 1) * c, p),
            (DEPTH, g, S5_CHUNK * c, p), (DEPTH, g, S5_CHUNK * c, p),
            (DEPTH, g, S5_CHUNK * c, c), (DEPTH, g, nk, p), (DEPTH, g, nk, p)]
    lay = lambda s: pl.BlockSpec((None,) + s[1:], lambda l: (l,) + (0,) * (len(s) - 1))
    ins = [a_re.reshape(DEPTH, g, 1, p), a_im.reshape(DEPTH, g, 1, p),
           jnp.broadcast_to(log_dt[:, :, None, None], (DEPTH, g, 1, p)),
           b_re.transpose(0, 1, 3, 2), b_im.transpose(0, 1, 3, 2), c_re, c_im]
    return pl.pallas_call(
        _s5_setup_kernel,
        grid=(DEPTH,),
        in_specs=[lay(x.shape) for x in ins],
        out_specs=[lay(s) for s in outs],
        out_shape=[jax.ShapeDtypeStruct(s, F32) for s in outs],
        compiler_params=_cparams(("arbitrary",)),
        name="s5_setup",
    )(*ins)


def _s5_chunk_kernel(u_ref, h0_ref, kmat_ref, smat_ref, ymat_ref, a1_ref, a2_ref,
                     y_ref, hl_ref, s_sc, hp_sc, *, gb, nj, nbt):
    for gi in range(gb):
        s_sc[gi] = _dot(u_ref[gi], smat_ref[gi])

    def step(j, hs):
        r0 = pl.multiple_of(j * nbt, nbt)
        new = []
        for gi in range(gb):
            h = hs[gi]
            hp_sc[gi, pl.ds(r0, nbt), :] = h
            new.append(a1_ref[gi] * h + a2_ref[gi] * pltpu.roll(h, S5_STATE, 1)
                       + s_sc[gi, pl.ds(r0, nbt), :])
        return tuple(new)

    hs = lax.fori_loop(0, nj, step, tuple(h0_ref[gi] for gi in range(gb)))
    for gi in range(gb):
        hl_ref[gi] = hs[gi]
        y_ref[gi] = _dot(u_ref[gi], kmat_ref[gi]) + _dot(hp_sc[gi].astype(BF16), ymat_ref[gi])


def _s5_chunk(u_g, h0_g, kmat, smat, ymat, a1, a2, nbt):
    g, rows, w = u_g.shape
    nj = rows // nbt
    gb = 8
    st = 2 * S5_STATE
    blk = lambda d1, d2: pl.BlockSpec((gb, d1, d2), lambda i: (i, 0, 0))
    kern = functools.partial(_s5_chunk_kernel, gb=gb, nj=nj, nbt=nbt)
    return pl.pallas_call(
        kern,
        grid=(g // gb,),
        in_specs=[blk(rows, w), blk(nbt, st), blk(w, w), blk(w, st), blk(st, w), blk(1, st), blk(1, st)],
        out_specs=[blk(rows, w), blk(nbt, st)],
        out_shape=[jax.ShapeDtypeStruct((g, rows, w), F32), jax.ShapeDtypeStruct((g, nbt, st), F32)],
        scratch_shapes=[pltpu.VMEM((gb, rows, st), F32), pltpu.VMEM((gb, rows, st), F32)],
        compiler_params=_cparams(("arbitrary",)),
        name="s5_chunk",
    )(u_g, h0_g, kmat, smat, ymat, a1, a2)


def _s5_rows_kernel(u_ref, w_ref, sw_ref, yw_ref, a1_ref, a2_ref, y_ref, hl_ref,
                    xr_sc, sg_sc, sgs_sc, hp_sc, hcat_sc, *, nbl, nj):
    L = S5_CHUNK
    mr = nbl * nj
    ng = LANES // S5_GROUP
    st = 2 * S5_STATE
    for s in range(L):
        xr_sc[:, (L - 1 - s) * LANES:(L - s) * LANES] = u_ref[pl.ds(s, mr, stride=L), :].astype(BF16)
    sall = _dot(xr_sc[...], sw_ref[...])
    for g in range(ng):
        s_g = sall[:, g * st:(g + 1) * st]
        sg_sc[g] = s_g
        sgs_sc[g] = pltpu.roll(s_g, S5_STATE, 1)

    def step(j, carry):
        hs, hss = carry
        new, news = [], []
        for g in range(ng):
            h, hsw = hs[g], hss[g]
            a1, a2 = a1_ref[g], a2_ref[g]
            hp_sc.at[g][pl.ds(j, nbl, stride=nj), :] = h
            new.append(a1 * h + a2 * hsw + sg_sc.at[g][pl.ds(j, nbl, stride=nj), :])
            news.append(a1 * hsw - a2 * h + sgs_sc.at[g][pl.ds(j, nbl, stride=nj), :])
        return tuple(new), tuple(news)

    zero = tuple(jnp.zeros((nbl, st), F32) for _ in range(ng))
    hs, _ = lax.fori_loop(0, nj, step, (zero, zero))
    for g in range(ng):
        hl_ref[g] = hs[g]
        hcat_sc[:, g * st:(g + 1) * st] = hp_sc[g].astype(BF16)
    for tp in range(L // 2):
        t1 = 2 * tp + 1
        y = (_dot(xr_sc[:, (L - 1 - t1) * LANES:], w_ref[0:(t1 + 1) * LANES, :])
             + _dot(hcat_sc[...], yw_ref[:, (t1 - 1) * LANES:(t1 + 1) * LANES]))
        y_ref[pl.ds(t1 - 1, mr, stride=L), :] = y[:, :LANES]
        y_ref[pl.ds(t1, mr, stride=L), :] = y[:, LANES:]


def _s5_rows(u4, w, sw, yw, a1, a2, nb_batch, t):
    nch, rows, _ = u4.shape
    nsplit = 2
    nbl = nb_batch // nsplit
    nj = t // S5_CHUNK
    rb = nbl * t
    mr = nbl * nj
    ng = LANES // S5_GROUP
    st = 2 * S5_STATE
    kern = functools.partial(_s5_rows_kernel, nbl=nbl, nj=nj)
    per_c = lambda *s: pl.BlockSpec((None,) + s, lambda c, h: (c,) + (0,) * len(s))
    return pl.pallas_call(
        kern,
        grid=(nch, nsplit),
        in_specs=[pl.BlockSpec((None, rb, LANES), lambda c, h: (c, h, 0)),
                  per_c(S5_CHUNK * LANES, 2 * LANES), per_c(S5_CHUNK * LANES, ng * st),
                  per_c(ng * st, S5_CHUNK * LANES), per_c(ng, 1, st), per_c(ng, 1, st)],
        out_specs=[pl.BlockSpec((None, rb, LANES), lambda c, h: (c, h, 0)),
                   pl.BlockSpec((None, ng, None, nbl, st), lambda c, h: (c, 0, h, 0, 0))],
        out_shape=[jax.ShapeDtypeStruct((nch, rows, LANES), F32),
                   jax.ShapeDtypeStruct((nch, ng, nsplit, nbl, st), F32)],
        scratch_shapes=[pltpu.VMEM((mr, S5_CHUNK * LANES), BF16), pltpu.VMEM((ng, mr, st), F32),
                        pltpu.VMEM((ng, mr, st), F32), pltpu.VMEM((ng, mr, st), F32),
                        pltpu.VMEM((mr, ng * st), BF16)],
        compiler_params=_cparams(("arbitrary", "arbitrary")),
        name="s5_rows",
    )(u4, w, sw, yw, a1, a2)


def _post_kernel(x_ref, gmod_ref, yssm_ref, u_ref, zs5_ref, o_ref, znsa_ref, merge_ref,
                 d_ref, gluw_ref, glub_ref, ws5_ref, wnsa_ref, wo_ref, xo_ref):
    y = jnp.concatenate([yssm_ref[ch] + d_ref[ch] * u_ref[ch] for ch in range(D_S5 // LANES)], axis=1)
    y = 0.5 * y * (1.0 + jnp.tanh(math.sqrt(2.0 / math.pi) * (y + 0.044715 * (y * y * y))))
    y = y * jax.nn.sigmoid(_dot(y.astype(BF16), gluw_ref[...]) + glub_ref[...])
    y = y * _silu(zs5_ref[...])
    b_s5 = _dot(y.astype(BF16), ws5_ref[...])
    b_nsa = _dot((o_ref[...] * _silu(znsa_ref[...])).astype(BF16), wnsa_ref[...])
    m = jax.nn.sigmoid(merge_ref[...])
    mix = m[:, :D_MODEL] * b_s5 + m[:, D_MODEL:] * b_nsa
    xo_ref[...] = x_ref[...] + gmod_ref[...] * _dot(mix.astype(BF16), wo_ref[...])


def _post(x2d, gmod, yssm, u, zs5, o, znsa, merge, d, gluw, glub, ws5, wnsa, wo,
          *, tm, tiles_per_b, per_row_mod):
    rows = x2d.shape[0]
    if per_row_mod:
        mod_spec = pl.BlockSpec((tm, D_MODEL), lambda i: (i, 0))
    else:
        mod_spec = pl.BlockSpec((None, 1, D_MODEL), lambda i: (i // tiles_per_b, 0, 0))
    row = lambda w: pl.BlockSpec((tm, w), lambda i: (i, 0))
    full = lambda a, b: pl.BlockSpec((a, b), lambda i: (0, 0))
    nch = D_S5 // LANES
    chunked = pl.BlockSpec((nch, tm, LANES), lambda i: (0, i, 0))
    return pl.pallas_call(
        _post_kernel,
        grid=(rows // tm,),
        in_specs=[row(D_MODEL), mod_spec, chunked, chunked, row(512), row(512), row(512), row(2048),
                  pl.BlockSpec((nch, 1, LANES), lambda i: (0, 0, 0)),
                  full(512, 512), full(1, 512), full(512, D_MODEL), full(512, D_MODEL),
                  full(D_MODEL, D_MODEL)],
        out_specs=row(D_MODEL),
        out_shape=jax.ShapeDtypeStruct((rows, D_MODEL), F32),
        compiler_params=_cparams(("arbitrary",)),
        name="post",
    )(x2d, gmod, yssm, u, zs5, o, znsa, merge, d, gluw, glub, ws5, wnsa, wo)


def _final_norm_kernel(x_ref, g_ref, o_ref):
    x = x_ref[...]
    o_ref[...] = x * lax.rsqrt(jnp.mean(x * x, axis=-1, keepdims=True) + RMS_EPS) * g_ref[...]


def _final_norm(x2d, g, tm):
    rows = x2d.shape[0]
    return pl.pallas_call(
        _final_norm_kernel,
        grid=(rows // tm,),
        in_specs=[pl.BlockSpec((tm, D_MODEL), lambda i: (i, 0)), pl.BlockSpec((1, D_MODEL), lambda i: (0, 0))],
        out_specs=pl.BlockSpec((tm, D_MODEL), lambda i: (i, 0)),
        out_shape=jax.ShapeDtypeStruct((rows, D_MODEL), F32),
        compiler_params=_cparams(("arbitrary",)),
        name="final_norm",
    )(x2d, g)


def _head_perm():
    idx = [HEAD_DIM * (GROUP_SIZE * g + r) + d
           for r in range(GROUP_SIZE) for g in range(N_KV) for d in range(HEAD_DIM)]
    return np.asarray(idx, np.int32)


def _rope_tables(pos, width=LANES):
    inv = ROPE_THETA ** (-jnp.arange(ROT_HALF, dtype=F32) / ROT_HALF)
    ang = pos.astype(F32)[:, None] * inv[None, :]
    cos, sin = jnp.cos(ang), jnp.sin(ang)
    n = pos.shape[0]
    one = jnp.ones((n, HEAD_DIM - 2 * ROT_HALF), F32)
    zero8 = jnp.zeros((n, ROT_HALF), F32)
    zero = jnp.zeros((n, HEAD_DIM - 2 * ROT_HALF), F32)
    c = jnp.concatenate([cos, cos, one], axis=1)
    sa = jnp.concatenate([-sin, zero8, zero], axis=1)
    sb = jnp.concatenate([zero8, sin, zero], axis=1)
    rep = width // HEAD_DIM
    return tuple(jnp.tile(t, (1, rep)) for t in (c, sa, sb))


def _blockdiag2(w):
    z = jnp.zeros_like(w)
    return jnp.concatenate([jnp.concatenate([w, z], -1), jnp.concatenate([z, w], -1)], -2)


def _s5_matrices(tabs, l, t_eff):
    ckr, cki, bkr, bki, m, pwr, pwi = [t[l] for t in tabs]
    g, c, L = S5_GROUPS, S5_GROUP, S5_CHUNK
    mk = m.reshape(g, L, c, c)
    s_idx = np.arange(L)[:, None]
    t_idx = np.arange(L)[None, :]
    lag = np.clip(t_idx - s_idx, 0, L - 1)
    kfull = mk[:, lag]
    kfull = jnp.where((t_idx >= s_idx)[None, :, :, None, None], kfull, 0.0)
    kmat = kfull.transpose(0, 1, 4, 2, 3).reshape(g, L * c, L * c).astype(BF16)
    ck = jnp.concatenate([ckr, cki], axis=-1)
    ymat = ck[:, c:, :].transpose(0, 2, 1).astype(BF16)
    bk = jnp.concatenate([bkr, bki], axis=-1).reshape(g, L, c, 2 * S5_STATE)
    sm = bk[:, t_eff - 1::-1] if t_eff == L else bk[:, np.arange(t_eff - 1, -1, -1)]
    if t_eff < L:
        sm = jnp.concatenate([sm, jnp.zeros((g, L - t_eff, c, 2 * S5_STATE), F32)], axis=1)
    smat = sm.reshape(g, L * c, 2 * S5_STATE).astype(BF16)
    ar = pwr[:, t_eff][:, None, :]
    ai = pwi[:, t_eff][:, None, :]
    a1 = jnp.concatenate([ar, ar], axis=-1)
    a2 = jnp.concatenate([-ai, ai], axis=-1)
    return kmat, smat, ymat, a1, a2


def _s5_row_weights(tabs):
    ckr, cki, bkr, bki, m, pwr, pwi = tabs
    c, L, st = S5_GROUP, S5_CHUNK, 2 * S5_STATE
    ng = LANES // c
    nch = S5_GROUPS // ng
    eye = jnp.eye(ng, dtype=BF16)
    mk = m.astype(BF16).reshape(DEPTH, nch, ng, L, c, c)
    w = jnp.einsum('dhgloi,gq->dhlgiqo', mk, eye).reshape(DEPTH, nch, L * LANES, LANES)
    w_shift = jnp.concatenate([jnp.zeros_like(w[:, :, :LANES]), w[:, :, :-LANES]], axis=2)
    w2 = jnp.concatenate([w_shift, w], axis=-1)
    bk = jnp.concatenate([bkr, bki], axis=-1).astype(BF16).reshape(DEPTH, nch, ng, L, c, st)
    sw = jnp.einsum('dhgkcp,gq->dhkgcqp', bk, eye).reshape(DEPTH, nch, L * LANES, ng * st)
    ck = jnp.concatenate([ckr, cki], axis=-1).astype(BF16).reshape(DEPTH, nch, ng, L + 1, c, st)[:, :, :, 1:]
    yw = jnp.einsum('dhgtcp,gq->dhgptqc', ck, eye).reshape(DEPTH, nch, ng * st, L * LANES)
    ar = pwr[:, :, L].reshape(DEPTH, nch, ng, 1, S5_STATE)
    ai = pwi[:, :, L].reshape(DEPTH, nch, ng, 1, S5_STATE)
    a1 = jnp.concatenate([ar, ar], axis=-1)
    a2 = jnp.concatenate([-ai, ai], axis=-1)
    return w2, sw, yw, a1, a2


def _all_layer_weights(w_in, cmp_pe, cmp_w1, cmp_w2, s5_glu_w, w_s5_out, w_nsa_out, w_o, perm):
    gate_w = jnp.pad(w_in[:, :, 2304:2328], ((0, 0), (0, 0), (0, LANES - 3 * N_HEADS)))
    wp = jnp.concatenate([w_in[:, :, :1024], w_in[:, :, 1024:1536][:, :, perm], w_in[:, :, 1536:2304],
                          w_in[:, :, 2328:2840][:, :, perm], w_in[:, :, 2840:], gate_w], axis=2).astype(BF16)
    pe2 = jnp.concatenate([cmp_pe, cmp_pe], axis=-1)[:, :, :, None, :]
    w1bd = _blockdiag2(cmp_w1.astype(BF16).reshape(DEPTH, 2, BLOCK, HEAD_DIM, HEAD_DIM))
    w2bd = _blockdiag2(cmp_w2.astype(BF16))
    wt = w_in[:, :, 1536:2304].transpose(0, 2, 1).astype(BF16)
    return dict(wp=wp, wt=wt, pe2=pe2, w1bd=w1bd, w2bd=w2bd, gluw=s5_glu_w.astype(BF16),
                ws5=w_s5_out.astype(BF16), wnsa=w_nsa_out[:, perm, :].astype(BF16),
                wo=w_o.astype(BF16))


def _to_groups(u2d, nbt, nj):
    u5 = u2d.reshape(nbt, nj, S5_CHUNK, S5_GROUPS, S5_GROUP)
    return u5.transpose(3, 1, 0, 2, 4).reshape(S5_GROUPS, nj * nbt, S5_CHUNK * S5_GROUP)


def _from_groups(y_g, nbt, nj):
    y5 = y_g.reshape(S5_GROUPS, nj, nbt, S5_CHUNK, S5_GROUP)
    return y5.transpose(2, 1, 3, 0, 4).reshape(nbt * nj * S5_CHUNK, D_S5)


def _state_to_groups(h):
    return h.transpose(2, 0, 1, 3).reshape(S5_GROUPS, h.shape[0], 2 * S5_STATE)


def _state_from_groups(hg):
    g, b, _ = hg.shape
    return hg.reshape(g, b, 2, S5_STATE).transpose(1, 2, 0, 3)


def kernel(x_prompt, x_sample, c_prompt, c_sample, cache_kv, page_table, state_win, state_ssm, ada_w, ada_b, norm_g, w_in, s5_a_re, s5_a_im, s5_log_dt, s5_b_re, s5_b_im, s5_c_re, s5_c_im, s5_d, s5_glu_w, s5_glu_b, cmp_pe, cmp_w1, cmp_w2, w_s5_out, w_nsa_out, w_o, final_g):
    bp, tp, _ = x_prompt.shape
    bs, ts, _ = x_sample.shape
    n_pool, _, page = cache_kv.shape[:3]
    n_pages = page_table.shape[1]
    past = n_pages * page
    wbuf = state_win.shape[2]
    assert tp % S5_CHUNK == 0 and tp % 128 == 0 and ts <= 8 and bs % 8 == 0 and bp % 8 == 0

    perm = _head_perm()
    mod = _ada_mod(jnp.concatenate([c_prompt, c_sample], axis=0), ada_w, ada_b)
    s5tabs = _s5_setup(s5_a_re, s5_a_im, s5_log_dt, s5_b_re, s5_b_im, s5_c_re, s5_c_im)

    tabs_p = _rope_tables(jnp.arange(tp))
    rs = bs * ts
    tabs_s = _rope_tables(jnp.tile(past + jnp.arange(ts), bs))
    nb_p = tp // BLOCK
    nbp_p = (nb_p + LANES - 1) // LANES * LANES
    ctabs_p = _rope_tables(jnp.arange(nbp_p) * BLOCK + (BLOCK - 1))
    nb_s = (past + ts + BLOCK - 1) // BLOCK
    nbp_s = ((nb_s + 7) // 8 * 8 + LANES - 1) // LANES * LANES
    ctabs_s = _rope_tables(jnp.arange(nbp_s) * BLOCK + (BLOCK - 1))

    cache5 = cache_kv.transpose(0, 1, 3, 4, 5, 2).reshape(n_pool, DEPTH, 4, KV_W, page)
    state_wint = state_win.transpose(0, 1, 3, 4, 5, 2).reshape(bs, DEPTH, 2, KV_W, wbuf)
    tabs_pt = tuple(tb.T for tb in tabs_p)
    tm_p = 512 if tp % 512 == 0 else 256
    tq_p = 128
    nj_p = tp // S5_CHUNK
    consts = _attn_constants(tp, tq_p)
    keep = min(WINDOW, tp)

    xp = x_prompt.reshape(bp * tp, D_MODEL)
    xs = x_sample.reshape(rs, D_MODEL)
    kv_p, kv_s, win_p, win_s, ssm_p, ssm_s = [], [], [], [], [], []

    def pad_rows(a3):
        return jnp.pad(a3, ((0, 0), (0, 8 - ts), (0, 0)))

    def to_chunks(a2):
        return a2.reshape(a2.shape[0], D_S5 // LANES, LANES).transpose(1, 0, 2)

    def new_tiles(a3):
        a4 = a3.reshape(bs, ts, 2, KV_W).transpose(0, 2, 3, 1)
        return jnp.pad(a4, ((0, 0), (0, 0), (0, 0), (0, LANES - ts)))

    lw_all = _all_layer_weights(w_in, cmp_pe, cmp_w1, cmp_w2, s5_glu_w, w_s5_out, w_nsa_out, w_o, perm)
    s5w_all = _s5_row_weights(s5tabs)

    for l in range(DEPTH):
        lw = {name: val[l] for name, val in lw_all.items()}
        g_row = norm_g[l][None, :]
        d_row = s5_d[l].reshape(D_S5 // LANES, 1, LANES)
        glub = s5_glu_b[l][None, :]
        mp, ms = mod[l, :bp], mod[l, bp:]
        post_w = (d_row, lw['gluw'], glub, lw['ws5'], lw['wnsa'], lw['wo'])

        shift, scale, gmod = [mp[:, k * D_MODEL:(k + 1) * D_MODEL][:, None, :] for k in range(3)]
        u, zs5, q, kvt, wint, cmp, znsa, merge, gate = _inproj_prompt(
            xp, shift, scale, g_row, lw['wp'], lw['wt'], tabs_p, tabs_pt, tm=tm_p, nb_batch=bp, t=tp)
        kcvc = _cmp_prompt(cmp, bp, tp, nbp_p, lw['pe2'], lw['w1bd'], lw['w2bd'], ctabs_p)
        o = _nsa_prompt(q, kvt, wint, kcvc, gate, consts, bp, tp, tq_p)
        yssm, hl = _s5_rows(u, *[a[l] for a in s5w_all], bp, tp)
        xp = _post(xp, gmod, yssm, u, zs5, o, znsa, merge, *post_w,
                   tm=tm_p, tiles_per_b=tp // tm_p, per_row_mod=False)
        kv_p.append(kvt)
        win_p.append(wint[:, :, tp - keep:])
        ssm_p.append(_state_from_groups(hl.reshape(S5_GROUPS, bp, 2 * S5_STATE)))

        shift, scale, gmod = [jnp.repeat(ms[:, k * D_MODEL:(k + 1) * D_MODEL], ts, axis=0) for k in range(3)]
        u, zs5, q, kv, win, znsa, merge, gate = _inproj(
            xs, shift, scale, g_row, lw['wp'], tabs_s, tm=rs, tiles_per_b=1, per_row_mod=True)
        kv3 = kv.reshape(bs, ts, 4 * KV_W)
        kcvc = _cmp_sample(page_table, cache5, pad_rows(kv3[:, :, :2 * KV_W]), l, nbp_s,
                           lw['pe2'], lw['w1bd'], lw['w2bd'], ctabs_s, ts)
        o8, wst = _nsa_sample(page_table, cache5, pad_rows(q.reshape(bs, ts, D_NSA)),
                              new_tiles(kv3[:, :, 2 * KV_W:]), new_tiles(win.reshape(bs, ts, 2 * KV_W)),
                              state_wint, kcvc, pad_rows(gate.reshape(bs, ts, LANES)), l, ts)
        o = o8[:, :ts].reshape(rs, D_NSA)
        mats = _s5_matrices(s5tabs, l, ts)
        u_pad = jnp.pad(u.reshape(bs, ts, D_S5), ((0, 0), (0, S5_CHUNK - ts), (0, 0)))
        y_g, hl = _s5_chunk(_to_groups(u_pad.reshape(bs * S5_CHUNK, D_S5), bs, 1).astype(BF16),
                            _state_to_groups(state_ssm[:, l]), *mats, nbt=bs)
        yssm = _from_groups(y_g, bs, 1).reshape(bs, S5_CHUNK, D_S5)[:, :ts].reshape(rs, D_S5)
        xs = _post(xs, gmod, to_chunks(yssm), to_chunks(u), zs5, o, znsa, merge, *post_w,
                   tm=rs, tiles_per_b=1, per_row_mod=True)
        kv_s.append(kv.reshape(bs, ts, 4, N_KV, HEAD_DIM))
        win_s.append(wst)
        ssm_s.append(_state_from_groups(hl))

    def from_t(parts, n_streams):
        a = jnp.stack(parts, axis=1)
        nbt, keys = a.shape[0], a.shape[-1]
        return a.reshape(nbt, DEPTH, n_streams, N_KV, HEAD_DIM, keys).transpose(0, 1, 5, 2, 3, 4)

    fg = final_g[None, :]
    y_prompt = _final_norm(xp, fg, tm_p).reshape(bp, tp, D_MODEL)
    y_sample = _final_norm(xs, fg, rs).reshape(bs, ts, D_MODEL)
    return (y_prompt, y_sample, from_t(kv_p, 4), jnp.stack(kv_s, axis=1),
            from_t(win_p, 2), from_t(win_s, 2),
            jnp.stack(ssm_p, axis=1), jnp.stack(ssm_s, axis=1))
```

```python
import functools
import math

import numpy as np
import jax
import jax.numpy as jnp
from jax import lax
from jax.experimental import pallas as pl
from jax.experimental.pallas import tpu as pltpu

F32 = jnp.float32
BF16 = jnp.bfloat16

D_MODEL = 1024
DEPTH = 4
D_S5 = 512
S5_GROUP = 16
S5_GROUPS = 32
S5_STATE = 64
D_NSA = 512
HEAD_DIM = 64
N_HEADS = 8
N_KV = 2
GROUP_SIZE = 4
BLOCK = 64
N_SELECT = 16
WINDOW = 512
ROT_HALF = 8
ROPE_THETA = 500000.0
RMS_EPS = 1e-6
NEG_INF = -1e30
FORCED_SCORE = 1e4

LANES = 128
S5_CHUNK = 16
KV_W = N_KV * HEAD_DIM
N_PROJ = 4992
CMP_PITCH = 72
VMEM_LIMIT = 56 * 1024 * 1024


def _dot(a, b):
    return jnp.dot(a, b, preferred_element_type=F32)


def _dot_nt(a, b):
    return lax.dot_general(a, b, (((1,), (1,)), ((), ())), preferred_element_type=F32)


def _silu(x):
    return x * jax.nn.sigmoid(x)


def _rope128(x, c, sa, sb):
    return x * c + pltpu.roll(x, LANES - ROT_HALF, 1) * sa + pltpu.roll(x, ROT_HALF, 1) * sb


def _masked_exp(s, mask):
    s = jnp.where(mask, s, NEG_INF)
    m = jnp.max(s, axis=-1, keepdims=True)
    p = jnp.where(mask, jnp.exp(s - m), 0.0)
    return p, jnp.sum(p, axis=-1, keepdims=True)


def _safe_inv(l):
    return jnp.where(l > 0.0, 1.0 / l, 0.0)


def _cparams(sem):
    return pltpu.CompilerParams(dimension_semantics=sem, vmem_limit_bytes=VMEM_LIMIT)


def _ada_kernel(c_ref, w_ref, b_ref, o_ref):
    c = c_ref[...]
    o_ref[...] = _dot(_silu(c).astype(BF16), w_ref[...].astype(BF16)) + b_ref[...]


def _ada_mod(c_all, ada_w, ada_b):
    nc = c_all.shape[0]
    tn = 1024
    return pl.pallas_call(
        _ada_kernel,
        grid=(DEPTH, 3 * D_MODEL // tn),
        in_specs=[pl.BlockSpec((nc, D_MODEL), lambda l, n: (0, 0)),
                  pl.BlockSpec((None, D_MODEL, tn), lambda l, n: (l, 0, n)),
                  pl.BlockSpec((None, 1, tn), lambda l, n: (l, 0, n))],
        out_specs=pl.BlockSpec((None, nc, tn), lambda l, n: (l, 0, n)),
        out_shape=jax.ShapeDtypeStruct((DEPTH, nc, 3 * D_MODEL), F32),
        compiler_params=_cparams(("arbitrary", "arbitrary")),
        name="ada_mod",
    )(c_all, ada_w, ada_b.reshape(DEPTH, 1, 3 * D_MODEL))


def _inproj_kernel(x_ref, shift_ref, scale_ref, g_ref, w_ref, cos_ref, sa_ref, sb_ref,
                   u_ref, zs5_ref, q_ref, kv_ref, win_ref, znsa_ref, merge_ref, gate_ref):
    x = x_ref[...]
    h = x * lax.rsqrt(jnp.mean(x * x, axis=-1, keepdims=True) + RMS_EPS) * g_ref[...]
    h = h * (1.0 + scale_ref[...]) + shift_ref[...]
    hb = h.astype(BF16)

    def mm(lo, hi):
        return _dot(hb, w_ref[:, lo:hi])

    c, sa, sb = cos_ref[...], sa_ref[...], sb_ref[...]
    u_ref[...] = mm(0, 512)
    zs5_ref[...] = mm(512, 1024).astype(BF16)
    for r in range(GROUP_SIZE):
        lo = 1024 + r * LANES
        q = _rope128(mm(lo, lo + LANES), c, sa, sb) * (HEAD_DIM ** -0.5)
        q_ref[:, r * LANES:(r + 1) * LANES] = q.astype(BF16)
    kv_ref[:, 0:256] = mm(1536, 1792)
    kv_ref[:, 256:384] = _rope128(mm(1792, 1920), c, sa, sb)
    kv_ref[:, 384:512] = mm(1920, 2048)
    win_ref[:, 0:128] = _rope128(mm(2048, 2176), c, sa, sb)
    win_ref[:, 128:256] = mm(2176, 2304)
    znsa_ref[...] = mm(2304, 2816).astype(BF16)
    merge_ref[...] = mm(2816, 4864).astype(BF16)
    gate_ref[...] = mm(4864, 4992)


def _inproj(x2d, shift, scale, g, wp, tabs, *, tm, tiles_per_b, per_row_mod):
    rows = x2d.shape[0]
    nt = rows // tm
    n_tab = tabs[0].shape[0] // tm
    if per_row_mod:
        mod_spec = pl.BlockSpec((tm, D_MODEL), lambda i: (i, 0))
    else:
        mod_spec = pl.BlockSpec((None, 1, D_MODEL), lambda i: (i // tiles_per_b, 0, 0))
    tab_spec = pl.BlockSpec((tm, LANES), lambda i: (i % n_tab, 0))

    def row_spec(w):
        return pl.BlockSpec((tm, w), lambda i: (i, 0))

    widths = (512, 512, 512, 512, 256, 512, 2048, 128)
    dtypes = (F32, BF16, BF16, F32, F32, BF16, BF16, F32)
    return pl.pallas_call(
        _inproj_kernel,
        grid=(nt,),
        in_specs=[row_spec(D_MODEL), mod_spec, mod_spec,
                  pl.BlockSpec((1, D_MODEL), lambda i: (0, 0)),
                  pl.BlockSpec((D_MODEL, N_PROJ), lambda i: (0, 0)),
                  tab_spec, tab_spec, tab_spec],
        out_specs=[row_spec(w) for w in widths],
        out_shape=[jax.ShapeDtypeStruct((rows, w), d) for w, d in zip(widths, dtypes)],
        compiler_params=_cparams(("arbitrary",)),
        name="inproj",
    )(x2d, shift, scale, g, wp, *tabs)


def _rope128_t(x, c, sa, sb):
    return x * c + pltpu.roll(x, KV_W - ROT_HALF, 0) * sa + pltpu.roll(x, ROT_HALF, 0) * sb


def _inproj_prompt_kernel(x_ref, shift_ref, scale_ref, g_ref, w_ref, wt_ref, cos_ref, sa_ref, sb_ref,
                          cost_ref, sat_ref, sbt_ref,
                          u_ref, zs5_ref, q_ref, kvt_ref, wint_ref, cmp_ref, znsa_ref, merge_ref, gate_ref):
    x = x_ref[...]
    h = x * lax.rsqrt(jnp.mean(x * x, axis=-1, keepdims=True) + RMS_EPS) * g_ref[...]
    h = h * (1.0 + scale_ref[...]) + shift_ref[...]
    hb = h.astype(BF16)

    def mm(lo, hi):
        return _dot(hb, w_ref[:, lo:hi])

    c, sa, sb = cos_ref[...], sa_ref[...], sb_ref[...]
    for ch in range(D_S5 // LANES):
        u_ref[ch] = mm(ch * LANES, (ch + 1) * LANES)
    zs5_ref[...] = mm(512, 1024).astype(BF16)
    for r in range(GROUP_SIZE):
        lo = 1024 + r * LANES
        q = _rope128(mm(lo, lo + LANES), c, sa, sb) * (HEAD_DIM ** -0.5)
        q_ref[:, r * LANES:(r + 1) * LANES] = q.astype(BF16)
    cmp_ref[...] = mm(1536, 1792)
    znsa_ref[...] = mm(2304, 2816).astype(BF16)
    merge_ref[...] = mm(2816, 4864).astype(BF16)
    gate_ref[...] = mm(4864, 4992)

    ct, sat, sbt = cost_ref[...], sat_ref[...], sbt_ref[...]

    def mmt(s):
        return _dot_nt(wt_ref[s * KV_W:(s + 1) * KV_W, :], hb)

    kvt_ref[0:KV_W, :] = mmt(0)
    kvt_ref[KV_W:2 * KV_W, :] = mmt(1)
    kvt_ref[2 * KV_W:3 * KV_W, :] = _rope128_t(mmt(2), ct, sat, sbt)
    kvt_ref[3 * KV_W:4 * KV_W, :] = mmt(3)
    wint_ref[0:KV_W, :] = _rope128_t(mmt(4), ct, sat, sbt)
    wint_ref[KV_W:2 * KV_W, :] = mmt(5)


def _inproj_prompt(x2d, shift, scale, g, wp, wt, tabs, tabs_t, *, tm, nb_batch, t):
    rows = x2d.shape[0]
    tpb = t // tm
    mod_spec = pl.BlockSpec((None, 1, D_MODEL), lambda i: (i // tpb, 0, 0))
    tab_spec = pl.BlockSpec((tm, LANES), lambda i: (i % tpb, 0))
    tabt_spec = pl.BlockSpec((KV_W, tm), lambda i: (0, i % tpb))
    row = lambda w: pl.BlockSpec((tm, w), lambda i: (i, 0))
    tr = lambda h: pl.BlockSpec((None, h, tm), lambda i: (i // tpb, 0, i % tpb))
    u_spec = pl.BlockSpec((D_S5 // LANES, tm, LANES), lambda i: (0, i, 0))
    outs = [((D_S5 // LANES, rows, LANES), F32, u_spec), ((rows, 512), BF16, row(512)),
            ((rows, 512), BF16, row(512)),
            ((nb_batch, 4 * KV_W, t), F32, tr(4 * KV_W)), ((nb_batch, 2 * KV_W, t), F32, tr(2 * KV_W)),
            ((rows, 256), F32, row(256)), ((rows, 512), BF16, row(512)), ((rows, 2048), BF16, row(2048)),
            ((rows, 128), F32, row(128))]
    return pl.pallas_call(
        _inproj_prompt_kernel,
        grid=(rows // tm,),
        in_specs=[row(D_MODEL), mod_spec, mod_spec,
                  pl.BlockSpec((1, D_MODEL), lambda i: (0, 0)),
                  pl.BlockSpec((D_MODEL, N_PROJ), lambda i: (0, 0)),
                  pl.BlockSpec((6 * KV_W, D_MODEL), lambda i: (0, 0)),
                  tab_spec, tab_spec, tab_spec, tabt_spec, tabt_spec, tabt_spec],
        out_specs=[o[2] for o in outs],
        out_shape=[jax.ShapeDtypeStruct(o[0], o[1]) for o in outs],
        compiler_params=_cparams(("arbitrary",)),
        name="inproj_prompt",
    )(x2d, shift, scale, g, wp, wt, *tabs, *tabs_t)


def _compress(load_p, nbk, pe_ref, w1_ref, w2_ref, c, sa, sb, xcat_sc):
    rows = xcat_sc.shape[0]
    outs = []
    for s in range(2):
        for p in range(BLOCK):
            xp = load_p(s, p) + pe_ref[s, p]
            if rows > nbk:
                xp = jnp.concatenate([xp, jnp.zeros((rows - nbk, KV_W), F32)], axis=0)
            xcat_sc[:, p * KV_W:(p + 1) * KV_W] = xp.astype(BF16)
        acc = _dot(xcat_sc[...], w1_ref[s])[0:nbk]
        outs.append(_dot(_silu(acc).astype(BF16), w2_ref[s]))
    return jnp.concatenate([_rope128(outs[0], c, sa, sb), outs[1]], axis=1)


def _cmp_prompt_kernel(k_ref, v_ref, pe_ref, w1_ref, w2_ref, cos_ref, sa_ref, sb_ref, o_ref, xcat_sc,
                       *, nbk):
    def load_p(s, p):
        return (k_ref, v_ref)[s][pl.ds(p, nbk, stride=BLOCK), :]

    res = _compress(load_p, nbk, pe_ref, w1_ref, w2_ref,
                    cos_ref[0:nbk, :], sa_ref[0:nbk, :], sb_ref[0:nbk, :], xcat_sc)
    o_ref[...] = jnp.zeros(o_ref.shape, F32)
    o_ref[0:nbk, :] = res


def _cmp_prompt(kv2d, nb_batch, t, nbp, pe2, w1bd, w2bd, ctabs):
    nbk = t // BLOCK
    full = lambda shape: pl.BlockSpec(shape, lambda b: (0,) * len(shape))
    return pl.pallas_call(
        functools.partial(_cmp_prompt_kernel, nbk=nbk),
        grid=(nb_batch,),
        in_specs=[pl.BlockSpec((t, KV_W), lambda b: (b, 0)), pl.BlockSpec((t, KV_W), lambda b: (b, 1)),
                  full((2, BLOCK, 1, KV_W)), full((2, BLOCK * KV_W, KV_W)),
                  full((2, KV_W, KV_W)), full((nbp, LANES)), full((nbp, LANES)), full((nbp, LANES))],
        out_specs=pl.BlockSpec((None, nbp, 2 * KV_W), lambda b: (b, 0, 0)),
        out_shape=jax.ShapeDtypeStruct((nb_batch, nbp, 2 * KV_W), F32),
        scratch_shapes=[pltpu.VMEM(((nbk + 15) // 16 * 16, BLOCK * KV_W), BF16)],
        compiler_params=_cparams(("arbitrary",)),
        name="cmp_prompt",
    )(kv2d, kv2d, pe2, w1bd, w2bd, *ctabs)


def _stack_queries(qt, tq):
    lane = lax.broadcasted_iota(jnp.int32, (tq, LANES), 1)
    qt = qt.astype(F32)
    blocks = []
    for g in range(N_KV):
        keep = (lane < HEAD_DIM) if g == 0 else (lane >= HEAD_DIM)
        for r in range(GROUP_SIZE):
            blocks.append(jnp.where(keep, qt[:, r * LANES:(r + 1) * LANES], 0.0))
    return jnp.concatenate(blocks, axis=0).astype(BF16)


def _select_blocks(imp, qblk, nb, n_sel):
    nq, nbp = imp.shape
    n = lax.broadcasted_iota(jnp.int32, (nq, nbp), 1)
    forced = (n == 0) | (n == qblk) | (n == qblk - 1)
    imp = jnp.where(forced, FORCED_SCORE, imp)
    imp = jnp.where(n > qblk, -1.0, imp)
    imp = jnp.where(n >= nb, -2.0, imp)
    rank = jnp.zeros((nq, nbp), F32)
    for m in range(nb):
        col = imp[:, m:m + 1]
        beats = (col > imp) | ((col == imp) & (n > m))
        rank = rank + jnp.where(beats, 1.0, 0.0)
    return jnp.where((rank < n_sel) & (imp > -0.5), 1.0, 0.0)


def _compressed_branch(qs, kcvc, tq_col, tq, nb):
    nbp = kcvc.shape[0]
    kc = kcvc[:, :KV_W].astype(BF16)
    vc = kcvc[:, KV_W:].astype(BF16)
    s = _dot_nt(qs, kc)
    n = lax.broadcasted_iota(jnp.int32, (1, nbp), 1)
    mask = (n * BLOCK + (BLOCK - 1) <= tq_col) & (n < nb)
    p, l = _masked_exp(s, mask)
    p = p * _safe_inv(l)
    o_c = _dot(p.astype(BF16), vc)
    imps = []
    for g in range(N_KV):
        acc = p[(g * GROUP_SIZE) * tq:(g * GROUP_SIZE + 1) * tq]
        for r in range(1, GROUP_SIZE):
            acc = acc + p[(g * GROUP_SIZE + r) * tq:(g * GROUP_SIZE + r + 1) * tq]
        imps.append(acc)
    return o_c, imps


def _block_mask(selstack, k0, tk):
    nbp = selstack.shape[1]
    kblk = (k0 + lax.broadcasted_iota(jnp.int32, (nbp, tk), 1)) // BLOCK
    e = jnp.where(kblk == lax.broadcasted_iota(jnp.int32, (nbp, tk), 0), 1.0, 0.0).astype(BF16)
    return _dot(selstack, e) > 0.5


def _gate_cols(gt, tq, branch):
    cols = []
    for g in range(N_KV):
        for r in range(GROUP_SIZE):
            j = branch * N_HEADS + g * GROUP_SIZE + r
            cols.append(gt[:, j:j + 1])
    return jnp.concatenate(cols, axis=0)


def _unstack_heads(o, tq):
    lane = lax.broadcasted_iota(jnp.int32, (tq, LANES), 1)
    chunks = []
    for r in range(GROUP_SIZE):
        a = o[r * tq:(r + 1) * tq]
        b = o[(GROUP_SIZE + r) * tq:(GROUP_SIZE + r + 1) * tq]
        chunks.append(jnp.where(lane < HEAD_DIM, a, b))
    return chunks


def _select_blocks_t(imp, qblk_row, nb, n_sel):
    nq, nbp = imp.shape
    nbr = (nb + 7) // 8 * 8
    x = imp.T[0:nbr, :]
    n = lax.broadcasted_iota(jnp.int32, (nbr, nq), 0)
    forced = (n == 0) | (n == qblk_row) | (n == qblk_row - 1)
    x = jnp.where(forced, FORCED_SCORE, x)
    x = jnp.where(n > qblk_row, -1.0, x)
    x = jnp.where(n >= nb, -2.0, x)
    rank = jnp.zeros((nbr, nq), F32)
    for m in range(nb):
        row = x[m:m + 1, :]
        beats = (row > x) | ((row == x) & (n > m))
        rank = rank + jnp.where(beats, 1.0, 0.0)
    sel = jnp.where((rank < n_sel) & (x > -0.5), 1.0, 0.0)
    if nbr < nbp:
        sel = jnp.concatenate([sel, jnp.zeros((nbp - nbr, nq), F32)], axis=0)
    return sel.T


def _nsa_prompt_kernel(q_ref, kvt_ref, wint_ref, kcvc_ref, gate_ref, e_ref, cbias_ref, wbias_ref,
                       o_ref, kaug_sc, vsel_sc, waug_sc, vwin_sc, s_sc, m_sc, l_sc, acc_sc, sel_sc,
                       *, tq, nb, n_sel):
    i = pl.program_id(1)
    q0 = pl.multiple_of(i * tq, tq)
    rows = N_HEADS * tq
    wk = WINDOW + tq
    tk = 2 * LANES
    neg = jnp.asarray(NEG_INF, F32)

    @pl.when(i == 0)
    def _():
        kaug_sc[0:KV_W, :] = kvt_ref[0:KV_W, :].astype(BF16)
        kaug_sc[KV_W:2 * KV_W, :] = e_ref[...]
        vsel_sc[...] = kvt_ref[KV_W:2 * KV_W, :].astype(BF16)
        waug_sc[...] = jnp.zeros(waug_sc.shape, BF16)
        waug_sc[0:KV_W, WINDOW:] = wint_ref[0:KV_W, :].astype(BF16)
        waug_sc[KV_W:KV_W + 16, 0:WINDOW] = jnp.ones((16, WINDOW), BF16)
        vwin_sc[:, 0:WINDOW] = jnp.zeros((KV_W, WINDOW), BF16)
        vwin_sc[:, WINDOW:] = wint_ref[KV_W:2 * KV_W, :].astype(BF16)

    qs = _stack_queries(q_ref[...], tq)
    t_loc = lax.broadcasted_iota(jnp.int32, (rows, 1), 0) % tq
    tq_col = q0 + t_loc

    o_c, imps = _compressed_branch(qs, kcvc_ref[...], tq_col, tq, nb)

    sel_sc[...] = jnp.ones(sel_sc.shape, F32)

    @pl.when(q0 + tq > n_sel * BLOCK)
    def _():
        qblk_row = (q0 + lax.broadcasted_iota(jnp.int32, (1, tq), 1)) // BLOCK
        for g in range(N_KV):
            sel_sc[g] = _select_blocks_t(imps[g], qblk_row, nb, n_sel)

    selb = [((sel_sc[g] - 1.0) * (-NEG_INF)).astype(BF16) for g in range(N_KV)]
    selstack = jnp.concatenate([selb[g] for g in range(N_KV) for _ in range(GROUP_SIZE)], axis=0)
    qaug = jnp.concatenate([qs, selstack], axis=1)

    n_full = q0 // tk
    par = (q0 // tq) % 2
    m_sc[...] = jnp.full(m_sc.shape, neg, F32)

    def score_tile(k0, w, bias):
        s = _dot(qaug, kaug_sc[:, pl.ds(k0, w)])
        if bias is not None:
            s = s + bias
        s_sc[:, pl.ds(k0, w)] = s
        mx = s[:, :LANES]
        for c in range(1, w // LANES):
            mx = jnp.maximum(mx, s[:, c * LANES:(c + 1) * LANES])
        m_sc[...] = jnp.maximum(m_sc[...], mx)

    def pass1(j, carry):
        score_tile(pl.multiple_of(j * 2 * tk, 2 * tk), 2 * tk, None)
        return carry

    lax.fori_loop(0, n_full // 2, pass1, 0)

    @pl.when(n_full % 2 == 1)
    def _():
        score_tile(pl.multiple_of((n_full - 1) * tk, tk), tk, None)

    score_tile(pl.multiple_of(n_full * tk, tk), tk, cbias_ref[par])
    m_b = jnp.broadcast_to(jnp.max(m_sc[...], axis=-1, keepdims=True), (rows, LANES))

    l_sc[...] = jnp.zeros(l_sc.shape, F32)
    acc_sc[...] = jnp.zeros(acc_sc.shape, F32)

    def pv_tile(k0, w):
        s = s_sc[:, pl.ds(k0, w)]
        ps = [jnp.exp(s[:, c * LANES:(c + 1) * LANES] - m_b) for c in range(w // LANES)]
        tot = ps[0]
        for pc in ps[1:]:
            tot = tot + pc
        l_sc[...] = l_sc[...] + tot
        p = jnp.concatenate(ps, axis=1).astype(BF16)
        acc_sc[...] = acc_sc[...] + _dot_nt(p, vsel_sc[:, pl.ds(k0, w)])

    def pass2(j, carry):
        pv_tile(pl.multiple_of(j * 2 * tk, 2 * tk), 2 * tk)
        return carry

    lax.fori_loop(0, (n_full + 1) // 2, pass2, 0)

    @pl.when(n_full % 2 == 0)
    def _():
        pv_tile(pl.multiple_of(n_full * tk, tk), tk)
    o_s = acc_sc[...] * _safe_inv(jnp.sum(l_sc[...], axis=-1, keepdims=True))

    lane = lax.broadcasted_iota(jnp.int32, (rows, LANES), 1)
    padcol = jnp.where(lane == 0, neg, 0.0).astype(BF16)
    qaug_w = jnp.concatenate([qs, padcol], axis=1)
    s = _dot(qaug_w, waug_sc[:, pl.ds(q0, wk)]) + wbias_ref[...]
    p = jnp.exp(s - jnp.max(s, axis=-1, keepdims=True))
    l = jnp.sum(p, axis=-1, keepdims=True)
    o_w = _dot_nt(p.astype(BF16), vwin_sc[:, pl.ds(q0, wk)]) * _safe_inv(l)

    gt = jax.nn.sigmoid(gate_ref[...])
    o = _gate_cols(gt, tq, 0) * o_c + _gate_cols(gt, tq, 1) * o_s + _gate_cols(gt, tq, 2) * o_w
    for r, ch in enumerate(_unstack_heads(o, tq)):
        o_ref[:, r * LANES:(r + 1) * LANES] = ch


def _attn_constants(t, tq):
    rows = N_HEADS * tq
    t_loc = np.arange(rows)[:, None] % tq
    k = np.arange(2 * LANES)[None, :]
    cb = np.stack([np.where(k <= t_loc, 0.0, NEG_INF), np.where(k - LANES <= t_loc, 0.0, NEG_INF)])
    kr = np.arange(WINDOW + tq)[None, :]
    wb = np.where((kr - WINDOW <= t_loc) & (kr > t_loc), 0.0, NEG_INF)
    e = (np.arange(t)[None, :] // BLOCK == np.arange(LANES)[:, None]).astype(np.float32)
    return jnp.asarray(e, BF16), jnp.asarray(cb, F32), jnp.asarray(wb, F32)


def _nsa_prompt(q2d, kvt, wint, kcvc, gate2d, consts, nb_batch, t, tq):
    nb = t // BLOCK
    nq = t // tq
    nbp = kcvc.shape[1]
    rows = N_HEADS * tq
    wk = WINDOW + tq
    assert nbp == LANES and t % (2 * LANES) == 0
    e, cb, wb = consts
    kern = functools.partial(_nsa_prompt_kernel, tq=tq, nb=nb, n_sel=min(N_SELECT, nb))
    return pl.pallas_call(
        kern,
        grid=(nb_batch, nq),
        in_specs=[pl.BlockSpec((tq, D_NSA), lambda b, i: (b * nq + i, 0)),
                  pl.BlockSpec((None, 2 * KV_W, t), lambda b, i: (b, 1, 0)),
                  pl.BlockSpec((None, 2 * KV_W, t), lambda b, i: (b, 0, 0)),
                  pl.BlockSpec((None, nbp, 2 * KV_W), lambda b, i: (b, 0, 0)),
                  pl.BlockSpec((tq, LANES), lambda b, i: (b * nq + i, 0)),
                  pl.BlockSpec((LANES, t), lambda b, i: (0, 0)),
                  pl.BlockSpec((2, rows, 2 * LANES), lambda b, i: (0, 0, 0)),
                  pl.BlockSpec((rows, wk), lambda b, i: (0, 0))],
        out_specs=pl.BlockSpec((tq, D_NSA), lambda b, i: (b * nq + i, 0)),
        out_shape=jax.ShapeDtypeStruct((nb_batch * t, D_NSA), F32),
        scratch_shapes=[pltpu.VMEM((2 * KV_W, t), BF16), pltpu.VMEM((KV_W, t), BF16),
                        pltpu.VMEM((2 * KV_W, WINDOW + t), BF16), pltpu.VMEM((KV_W, WINDOW + t), BF16),
                        pltpu.VMEM((rows, t), F32), pltpu.VMEM((rows, LANES), F32),
                        pltpu.VMEM((rows, LANES), F32), pltpu.VMEM((rows, KV_W), F32),
                        pltpu.VMEM((N_KV, tq, LANES), F32)],
        compiler_params=_cparams(("arbitrary", "arbitrary")),
        name="nsa_prompt",
    )(q2d, kvt, wint, kcvc, gate2d, e, cb, wb)


def _page_copy(cache_ref, slabs, sem_ref, pt_ref, layer, b, p, slot, s0, page, s):
    return pltpu.make_async_copy(
        cache_ref.at[pt_ref[b, p], layer, s0 + s],
        slabs[s].at[slot, :, pl.ds(p * page, page)],
        sem_ref.at[slot])


def _gather_pages(cache_ref, slabs, sem_ref, pt_ref, layer, s0, n_pages, page):
    b = pl.program_id(0)
    nb_batch = pl.num_programs(0)
    slot = b % 2

    def copies(bb, sl):
        return [_page_copy(cache_ref, slabs, sem_ref, pt_ref, layer, bb, p, sl, s0, page, s)
                for p in range(n_pages) for s in range(2)]

    @pl.when(b == 0)
    def _():
        for cp in copies(0, 0):
            cp.start()

    @pl.when(b + 1 < nb_batch)
    def _():
        for cp in copies(b + 1, 1 - slot):
            cp.start()

    for cp in copies(b, slot):
        cp.wait()
    return slot


def _cmp_sample_kernel(pt_ref, cache_ref, new_ref, pe_ref, w1_ref, w2_ref, cos_ref, sa_ref, sb_ref,
                       o_ref, stagek_ref, stagev_ref, rowk_ref, rowv_ref, xcat_sc, sem_ref,
                       *, layer, n_pages, page, nbk, nbs):
    stages = (stagek_ref, stagev_ref)
    rowm = (rowk_ref, rowv_ref)
    i = pl.program_id(0)
    n_steps = pl.num_programs(0)

    def copies(step):
        return [_page_copy(cache_ref, stages, sem_ref, pt_ref, layer, step * nbs + bl, p, bl, 0, page, s)
                for bl in range(nbs) for p in range(n_pages) for s in range(2)]

    @pl.when(i == 0)
    def _():
        for cp in copies(0):
            cp.start()

    for cp in copies(i):
        cp.wait()

    bpp = page // BLOCK
    nb_past = n_pages * bpp
    unroll = next(k for k in (8, 4, 2, 1) if n_pages % k == 0)
    rows_b = nbk * CMP_PITCH
    for s in range(2):
        for bl in range(nbs):
            def xpose(it, carry):
                for k in range(unroll):
                    p = it * unroll + k
                    c0 = pl.multiple_of(p * page, page)
                    tile = stages[s][bl, :, pl.ds(c0, page)].T
                    for h in range(bpp):
                        r0 = pl.multiple_of(bl * rows_b + (p * bpp + h) * CMP_PITCH, 8)
                        rowm[s][pl.ds(r0, BLOCK), :] = tile[h * BLOCK:(h + 1) * BLOCK, :]
                return carry

            lax.fori_loop(0, n_pages // unroll, xpose, 0)
            tail0 = bl * rows_b + nb_past * CMP_PITCH
            tail = (bl + 1) * rows_b - tail0
            rowm[s][pl.ds(tail0, tail), :] = jnp.zeros((tail, KV_W), F32)
            rowm[s][pl.ds(tail0, 8), :] = new_ref[bl, :, s * KV_W:(s + 1) * KV_W]

    @pl.when(i + 1 < n_steps)
    def _():
        for cp in copies(i + 1):
            cp.start()

    def load_p(s, p):
        return rowm[s][pl.ds(p, nbs * nbk, stride=CMP_PITCH), :]

    tabs = [jnp.concatenate([t_ref[0:nbk, :]] * nbs, axis=0) for t_ref in (cos_ref, sa_ref, sb_ref)]
    res = _compress(load_p, nbs * nbk, pe_ref, w1_ref, w2_ref, *tabs, xcat_sc)
    o_ref[...] = jnp.zeros(o_ref.shape, F32)
    for bl in range(nbs):
        o_ref[bl, 0:nbk, :] = res[bl * nbk:(bl + 1) * nbk]


def _cmp_sample(page_table, cache5, newcmp, layer, nbp, pe2, w1bd, w2bd, ctabs, t_new):
    nb_batch, n_pages = page_table.shape
    page = cache5.shape[4]
    past = n_pages * page
    nbk = ((past + t_new + BLOCK - 1) // BLOCK + 7) // 8 * 8
    nbs = 2
    slab_rows = nbs * nbk * CMP_PITCH
    assert page % BLOCK == 0 and t_new <= 8 and nb_batch % nbs == 0
    full = lambda shape: pl.BlockSpec(shape, lambda b, pt: (0,) * len(shape))
    kern = functools.partial(_cmp_sample_kernel, layer=layer, n_pages=n_pages, page=page, nbk=nbk,
                             nbs=nbs)
    return pl.pallas_call(
        kern,
        grid_spec=pltpu.PrefetchScalarGridSpec(
            num_scalar_prefetch=1,
            grid=(nb_batch // nbs,),
            in_specs=[pl.BlockSpec(memory_space=pl.ANY),
                      pl.BlockSpec((nbs, 8, 2 * KV_W), lambda b, pt: (b, 0, 0)),
                      full((2, BLOCK, 1, KV_W)), full((2, BLOCK * KV_W, KV_W)),
                      full((2, KV_W, KV_W)), full((nbp, LANES)), full((nbp, LANES)),
                      full((nbp, LANES))],
            out_specs=pl.BlockSpec((nbs, nbp, 2 * KV_W), lambda b, pt: (b, 0, 0)),
            scratch_shapes=[pltpu.VMEM((nbs, KV_W, past), F32), pltpu.VMEM((nbs, KV_W, past), F32),
                            pltpu.VMEM((slab_rows, KV_W), F32), pltpu.VMEM((slab_rows, KV_W), F32),
                            pltpu.VMEM(((nbs * nbk + 15) // 16 * 16, BLOCK * KV_W), BF16),
                            pltpu.SemaphoreType.DMA((nbs,))]),
        out_shape=jax.ShapeDtypeStruct((nb_batch, nbp, 2 * KV_W), F32),
        compiler_params=_cparams(("arbitrary",)),
        name="cmp_sample",
    )(page_table, cache5, newcmp, pe2, w1bd, w2bd, *ctabs)


def _nsa_sample_kernel(pt_ref, cache_ref, q_ref, newt_ref, wnewt_ref, swint_ref, kcvc_ref, gate_ref,
                       o_ref, wout_ref, slabk_ref, slabv_ref, wslab_ref, sem_ref,
                       *, layer, n_pages, page, t_new, nb, n_sel):
    tq = 8
    past = n_pages * page
    rows = N_HEADS * tq
    slabs = (slabk_ref, slabv_ref)
    slot = _gather_pages(cache_ref, slabs, sem_ref, pt_ref, layer, 2, n_pages, page)
    nk = slabk_ref.shape[2]
    for s in range(2):
        slabs[s][slot, :, pl.ds(past, LANES)] = newt_ref[s]

    qs = _stack_queries(q_ref[...], tq)
    t_loc = lax.broadcasted_iota(jnp.int32, (rows, 1), 0) % tq
    tq_col = past + t_loc

    o_c, imps = _compressed_branch(qs, kcvc_ref[...], tq_col, tq, nb)
    qblk = (past + lax.broadcasted_iota(jnp.int32, (tq, 1), 0)) // BLOCK
    sels = [_select_blocks(imps[g], qblk, nb, n_sel).astype(BF16) for g in range(N_KV)]
    selstack = jnp.concatenate([sels[g] for g in range(N_KV) for _ in range(GROUP_SIZE)], axis=0)

    s = _dot(qs, slabk_ref[slot].astype(BF16))
    kpos = lax.broadcasted_iota(jnp.int32, (1, nk), 1)
    p, l = _masked_exp(s, _block_mask(selstack, 0, nk) & (kpos <= tq_col))
    o_s = _dot_nt(p.astype(BF16), slabv_ref[slot].astype(BF16)) * _safe_inv(l)

    wbuf = swint_ref.shape[2]
    wk = wslab_ref.shape[2]
    wslab_ref[:, :, 0:wbuf] = swint_ref[...]
    wslab_ref[:, :, wbuf:wk] = wnewt_ref[...]
    s = _dot(qs, wslab_ref[0].astype(BF16))
    kpos = past - wbuf + lax.broadcasted_iota(jnp.int32, (1, wk), 1)
    p, l = _masked_exp(s, (kpos <= tq_col) & (kpos > tq_col - WINDOW) & (kpos >= 0))
    o_w = _dot_nt(p.astype(BF16), wslab_ref[1].astype(BF16)) * _safe_inv(l)
    wout_ref[...] = wslab_ref[:, :, t_new:t_new + wbuf]

    gt = jax.nn.sigmoid(gate_ref[...])
    o = _gate_cols(gt, tq, 0) * o_c + _gate_cols(gt, tq, 1) * o_s + _gate_cols(gt, tq, 2) * o_w
    for r, ch in enumerate(_unstack_heads(o, tq)):
        o_ref[:, r * LANES:(r + 1) * LANES] = ch


def _nsa_sample(page_table, cache5, q3, newselt, wnewt, state_wint, kcvc, gate3, layer, t_new):
    nb_batch, n_pages = page_table.shape
    page = cache5.shape[4]
    past = n_pages * page
    nb = (past + t_new + BLOCK - 1) // BLOCK
    nk = past + LANES
    wbuf = state_wint.shape[4]
    wk = wbuf + LANES
    nbp = kcvc.shape[1]
    kern = functools.partial(_nsa_sample_kernel, layer=layer, n_pages=n_pages, page=page,
                             t_new=t_new, nb=nb, n_sel=min(N_SELECT, nb))
    per_b = lambda d1, d2: pl.BlockSpec((None, d1, d2), lambda b, pt: (b, 0, 0))
    per_b4 = lambda d1, d2, d3: pl.BlockSpec((None, d1, d2, d3), lambda b, pt: (b, 0, 0, 0))
    return pl.pallas_call(
        kern,
        grid_spec=pltpu.PrefetchScalarGridSpec(
            num_scalar_prefetch=1,
            grid=(nb_batch,),
            in_specs=[pl.BlockSpec(memory_space=pl.ANY),
                      per_b(8, D_NSA), per_b4(2, KV_W, LANES), per_b4(2, KV_W, LANES),
                      pl.BlockSpec((None, None, 2, KV_W, wbuf), lambda b, pt: (b, layer, 0, 0, 0)),
                      per_b(nbp, 2 * KV_W), per_b(8, LANES)],
            out_specs=[per_b(8, D_NSA), per_b4(2, KV_W, wbuf)],
            scratch_shapes=[pltpu.VMEM((2, KV_W, nk), F32), pltpu.VMEM((2, KV_W, nk), F32),
                            pltpu.VMEM((2, KV_W, wk), F32), pltpu.SemaphoreType.DMA((2,))]),
        out_shape=[jax.ShapeDtypeStruct((nb_batch, 8, D_NSA), F32),
                   jax.ShapeDtypeStruct((nb_batch, 2, KV_W, wbuf), F32)],
        compiler_params=_cparams(("arbitrary",)),
        name="nsa_sample",
    )(page_table, cache5, q3, newselt, wnewt, state_wint, kcvc, gate3)


def _s5_setup_kernel(are_ref, aim_ref, ldt_ref, bre_ref, bim_ref, cre_ref, cim_ref,
                     ckr_ref, cki_ref, bkr_ref, bki_ref, m_ref, pwr_ref, pwi_ref):
    nk = pwr_ref.shape[1]

    def body(g, carry):
        ar = are_ref[g]
        ai = aim_ref[g]
        dt = jnp.exp(ldt_ref[g])
        mag = jnp.exp(ar * dt)
        abr = mag * jnp.cos(ai * dt)
        abi = mag * jnp.sin(ai * dt)
        den = ar * ar + ai * ai
        nr = abr - 1.0
        e_re = (nr * ar + abi * ai) / den
        e_im = (abi * ar - nr * ai) / den
        bre = bre_ref[g]
        bim = bim_ref[g]
        bbr = e_re * bre - e_im * bim
        bbi = e_re * bim + e_im * bre
        kk = lax.broadcasted_iota(jnp.int32, (nk, S5_STATE), 0).astype(F32)
        pmag = jnp.exp(kk * (ar * dt))
        pwr = pmag * jnp.cos(kk * (ai * dt))
        pwi = pmag * jnp.sin(kk * (ai * dt))
        pwr_ref[g] = pwr
        pwi_ref[g] = pwi
        cre = cre_ref[g]
        cim = cim_ref[g]
        for k in range(S5_CHUNK + 1):
            wr = pwr[k:k + 1, :]
            wi = pwi[k:k + 1, :]
            ckr_ref[g, k * S5_GROUP:(k + 1) * S5_GROUP, :] = cre * wr - cim * wi
            cki_ref[g, k * S5_GROUP:(k + 1) * S5_GROUP, :] = -(cre * wi + cim * wr)
            if k < S5_CHUNK:
                bkr_ref[g, k * S5_GROUP:(k + 1) * S5_GROUP, :] = bbr * wr - bbi * wi
                bki_ref[g, k * S5_GROUP:(k + 1) * S5_GROUP, :] = bbr * wi + bbi * wr
        nl = S5_CHUNK * S5_GROUP
        hp = lax.Precision.HIGHEST
        m_ref[g] = (lax.dot_general(ckr_ref[g, 0:nl, :], bbr, (((1,), (1,)), ((), ())),
                                    precision=hp, preferred_element_type=F32)
                    + lax.dot_general(cki_ref[g, 0:nl, :], bbi, (((1,), (1,)), ((), ())),
                                      precision=hp, preferred_element_type=F32))
        return carry

    lax.fori_loop(0, S5_GROUPS, body, 0)


def _s5_setup(a_re, a_im, log_dt, b_re, b_im, c_re, c_im):
    g, p, c = S5_GROUPS, S5_STATE, S5_GROUP
    nk = 24
    outs = [(DEPTH, g, (S5_CHUNK + 1) * c, p), (DEPTH, g, (S5_CHUNK + 1) * c, p),
            (DEPTH, g, S5_CHUNK * c, p), (DEPTH, g, S5_CHUNK * c, p),
            (DEPTH, g, S5_CHUNK * c, c), (DEPTH, g, nk, p), (DEPTH, g, nk, p)]
    lay = lambda s: pl.BlockSpec((None,) + s[1:], lambda l: (l,) + (0,) * (len(s) - 1))
    ins = [a_re.reshape(DEPTH, g, 1, p), a_im.reshape(DEPTH, g, 1, p),
           jnp.broadcast_to(log_dt[:, :, None, None], (DEPTH, g, 1, p)),
           b_re.transpose(0, 1, 3, 2), b_im.transpose(0, 1, 3, 2), c_re, c_im]
    return pl.pallas_call(
        _s5_setup_kernel,
        grid=(DEPTH,),
        in_specs=[lay(x.shape) for x in ins],
        out_specs=[lay(s) for s in outs],
        out_shape=[jax.ShapeDtypeStruct(s, F32) for s in outs],
        compiler_params=_cparams(("arbitrary",)),
        name="s5_setup",
    )(*ins)


def _s5_chunk_kernel(u_ref, h0_ref, kmat_ref, smat_ref, ymat_ref, a1_ref, a2_ref,
                     y_ref, hl_ref, s_sc, hp_sc, *, gb, nj, nbt):
    for gi in range(gb):
        s_sc[gi] = _dot(u_ref[gi], smat_ref[gi])

    def step(j, hs):
        r0 = pl.multiple_of(j * nbt, nbt)
        new = []
        for gi in range(gb):
            h = hs[gi]
            hp_sc[gi, pl.ds(r0, nbt), :] = h
            new.append(a1_ref[gi] * h + a2_ref[gi] * pltpu.roll(h, S5_STATE, 1)
                       + s_sc[gi, pl.ds(r0, nbt), :])
        return tuple(new)

    hs = lax.fori_loop(0, nj, step, tuple(h0_ref[gi] for gi in range(gb)))
    for gi in range(gb):
        hl_ref[gi] = hs[gi]
        y_ref[gi] = _dot(u_ref[gi], kmat_ref[gi]) + _dot(hp_sc[gi].astype(BF16), ymat_ref[gi])


def _s5_chunk(u_g, h0_g, kmat, smat, ymat, a1, a2, nbt):
    g, rows, w = u_g.shape
    nj = rows // nbt
    gb = 8
    st = 2 * S5_STATE
    blk = lambda d1, d2: pl.BlockSpec((gb, d1, d2), lambda i: (i, 0, 0))
    kern = functools.partial(_s5_chunk_kernel, gb=gb, nj=nj, nbt=nbt)
    return pl.pallas_call(
        kern,
        grid=(g // gb,),
        in_specs=[blk(rows, w), blk(nbt, st), blk(w, w), blk(w, st), blk(st, w), blk(1, st), blk(1, st)],
        out_specs=[blk(rows, w), blk(nbt, st)],
        out_shape=[jax.ShapeDtypeStruct((g, rows, w), F32), jax.ShapeDtypeStruct((g, nbt, st), F32)],
        scratch_shapes=[pltpu.VMEM((gb, rows, st), F32), pltpu.VMEM((gb, rows, st), F32)],
        compiler_params=_cparams(("arbitrary",)),
        name="s5_chunk",
    )(u_g, h0_g, kmat, smat, ymat, a1, a2)


def _s5_rows_kernel(u_ref, w_ref, sw_ref, yw_ref, a1_ref, a2_ref, y_ref, hl_ref,
                    xr_sc, sg_sc, sgs_sc, hp_sc, hcat_sc, *, nbl, nj):
    L = S5_CHUNK
    mr = nbl * nj
    ng = LANES // S5_GROUP
    st = 2 * S5_STATE
    for s in range(L):
        xr_sc[:, (L - 1 - s) * LANES:(L - s) * LANES] = u_ref[pl.ds(s, mr, stride=L), :].astype(BF16)
    sall = _dot(xr_sc[...], sw_ref[...])
    for g in range(ng):
        s_g = sall[:, g * st:(g + 1) * st]
        sg_sc[g] = s_g
        sgs_sc[g] = pltpu.roll(s_g, S5_STATE, 1)

    def step(j, carry):
        hs, hss = carry
        new, news = [], []
        for g in range(ng):
            h, hsw = hs[g], hss[g]
            a1, a2 = a1_ref[g], a2_ref[g]
            hp_sc.at[g][pl.ds(j, nbl, stride=nj), :] = h
            new.append(a1 * h + a2 * hsw + sg_sc.at[g][pl.ds(j, nbl, stride=nj), :])
            news.append(a1 * hsw - a2 * h + sgs_sc.at[g][pl.ds(j, nbl, stride=nj), :])
        return tuple(new), tuple(news)

    zero = tuple(jnp.zeros((nbl, st), F32) for _ in range(ng))
    hs, _ = lax.fori_loop(0, nj, step, (zero, zero))
    for g in range(ng):
        hl_ref[g] = hs[g]
        hcat_sc[:, g * st:(g + 1) * st] = hp_sc[g].astype(BF16)
    for tp in range(L // 2):
        t1 = 2 * tp + 1
        y = (_dot(xr_sc[:, (L - 1 - t1) * LANES:], w_ref[0:(t1 + 1) * LANES, :])
             + _dot(hcat_sc[...], yw_ref[:, (t1 - 1) * LANES:(t1 + 1) * LANES]))
        y_ref[pl.ds(t1 - 1, mr, stride=L), :] = y[:, :LANES]
        y_ref[pl.ds(t1, mr, stride=L), :] = y[:, LANES:]


def _s5_rows(u4, w, sw, yw, a1, a2, nb_batch, t):
    nch, rows, _ = u4.shape
    nsplit = 2
    nbl = nb_batch // nsplit
    nj = t // S5_CHUNK
    rb = nbl * t
    mr = nbl * nj
    ng = LANES // S5_GROUP
    st = 2 * S5_STATE
    kern = functools.partial(_s5_rows_kernel, nbl=nbl, nj=nj)
    per_c = lambda *s: pl.BlockSpec((None,) + s, lambda c, h: (c,) + (0,) * len(s))
    return pl.pallas_call(
        kern,
        grid=(nch, nsplit),
        in_specs=[pl.BlockSpec((None, rb, LANES), lambda c, h: (c, h, 0)),
                  per_c(S5_CHUNK * LANES, 2 * LANES), per_c(S5_CHUNK * LANES, ng * st),
                  per_c(ng * st, S5_CHUNK * LANES), per_c(ng, 1, st), per_c(ng, 1, st)],
        out_specs=[pl.BlockSpec((None, rb, LANES), lambda c, h: (c, h, 0)),
                   pl.BlockSpec((None, ng, None, nbl, st), lambda c, h: (c, 0, h, 0, 0))],
        out_shape=[jax.ShapeDtypeStruct((nch, rows, LANES), F32),
                   jax.ShapeDtypeStruct((nch, ng, nsplit, nbl, st), F32)],
        scratch_shapes=[pltpu.VMEM((mr, S5_CHUNK * LANES), BF16), pltpu.VMEM((ng, mr, st), F32),
                        pltpu.VMEM((ng, mr, st), F32), pltpu.VMEM((ng, mr, st), F32),
                        pltpu.VMEM((mr, ng * st), BF16)],
        compiler_params=_cparams(("arbitrary", "arbitrary")),
        name="s5_rows",
    )(u4, w, sw, yw, a1, a2)


def _post_kernel(x_ref, gmod_ref, yssm_ref, u_ref, zs5_ref, o_ref, znsa_ref, merge_ref,
                 d_ref, gluw_ref, glub_ref, ws5_ref, wnsa_ref, wo_ref, xo_ref):
    y = jnp.concatenate([yssm_ref[ch] + d_ref[ch] * u_ref[ch] for ch in range(D_S5 // LANES)], axis=1)
    y = 0.5 * y * (1.0 + jnp.tanh(math.sqrt(2.0 / math.pi) * (y + 0.044715 * (y * y * y))))
    y = y * jax.nn.sigmoid(_dot(y.astype(BF16), gluw_ref[...]) + glub_ref[...])
    y = y * _silu(zs5_ref[...].astype(F32))
    b_s5 = _dot(y.astype(BF16), ws5_ref[...])
    b_nsa = _dot((o_ref[...] * _silu(znsa_ref[...].astype(F32))).astype(BF16), wnsa_ref[...])
    m = jax.nn.sigmoid(merge_ref[...].astype(F32))
    mix = m[:, :D_MODEL] * b_s5 + m[:, D_MODEL:] * b_nsa
    xo_ref[...] = x_ref[...] + gmod_ref[...] * _dot(mix.astype(BF16), wo_ref[...])


def _post(x2d, gmod, yssm, u, zs5, o, znsa, merge, d, gluw, glub, ws5, wnsa, wo,
          *, tm, tiles_per_b, per_row_mod):
    rows = x2d.shape[0]
    if per_row_mod:
        mod_spec = pl.BlockSpec((tm, D_MODEL), lambda i: (i, 0))
    else:
        mod_spec = pl.BlockSpec((None, 1, D_MODEL), lambda i: (i // tiles_per_b, 0, 0))
    row = lambda w: pl.BlockSpec((tm, w), lambda i: (i, 0))
    full = lambda a, b: pl.BlockSpec((a, b), lambda i: (0, 0))
    nch = D_S5 // LANES
    chunked = pl.BlockSpec((nch, tm, LANES), lambda i: (0, i, 0))
    return pl.pallas_call(
        _post_kernel,
        grid=(rows // tm,),
        in_specs=[row(D_MODEL), mod_spec, chunked, chunked, row(512), row(512), row(512), row(2048),
                  pl.BlockSpec((nch, 1, LANES), lambda i: (0, 0, 0)),
                  full(512, 512), full(1, 512), full(512, D_MODEL), full(512, D_MODEL),
                  full(D_MODEL, D_MODEL)],
        out_specs=row(D_MODEL),
        out_shape=jax.ShapeDtypeStruct((rows, D_MODEL), F32),
        compiler_params=_cparams(("arbitrary",)),
        name="post",
    )(x2d, gmod, yssm, u, zs5, o, znsa, merge, d, gluw, glub, ws5, wnsa, wo)


def _final_norm_kernel(x_ref, g_ref, o_ref):
    x = x_ref[...]
    o_ref[...] = x * lax.rsqrt(jnp.mean(x * x, axis=-1, keepdims=True) + RMS_EPS) * g_ref[...]


def _final_norm(x2d, g, tm):
    rows = x2d.shape[0]
    return pl.pallas_call(
        _final_norm_kernel,
        grid=(rows // tm,),
        in_specs=[pl.BlockSpec((tm, D_MODEL), lambda i: (i, 0)), pl.BlockSpec((1, D_MODEL), lambda i: (0, 0))],
        out_specs=pl.BlockSpec((tm, D_MODEL), lambda i: (i, 0)),
        out_shape=jax.ShapeDtypeStruct((rows, D_MODEL), F32),
        compiler_params=_cparams(("arbitrary",)),
        name="final_norm",
    )(x2d, g)


def _head_perm():
    idx = [HEAD_DIM * (GROUP_SIZE * g + r) + d
           for r in range(GROUP_SIZE) for g in range(N_KV) for d in range(HEAD_DIM)]
    return np.asarray(idx, np.int32)


def _rope_tables(pos, width=LANES):
    inv = ROPE_THETA ** (-jnp.arange(ROT_HALF, dtype=F32) / ROT_HALF)
    ang = pos.astype(F32)[:, None] * inv[None, :]
    cos, sin = jnp.cos(ang), jnp.sin(ang)
    n = pos.shape[0]
    one = jnp.ones((n, HEAD_DIM - 2 * ROT_HALF), F32)
    zero8 = jnp.zeros((n, ROT_HALF), F32)
    zero = jnp.zeros((n, HEAD_DIM - 2 * ROT_HALF), F32)
    c = jnp.concatenate([cos, cos, one], axis=1)
    sa = jnp.concatenate([-sin, zero8, zero], axis=1)
    sb = jnp.concatenate([zero8, sin, zero], axis=1)
    rep = width // HEAD_DIM
    return tuple(jnp.tile(t, (1, rep)) for t in (c, sa, sb))


def _blockdiag2(w):
    z = jnp.zeros_like(w)
    return jnp.concatenate([jnp.concatenate([w, z], -1), jnp.concatenate([z, w], -1)], -2)


def _s5_matrices(tabs, l, t_eff):
    ckr, cki, bkr, bki, m, pwr, pwi = [t[l] for t in tabs]
    g, c, L = S5_GROUPS, S5_GROUP, S5_CHUNK
    mk = m.reshape(g, L, c, c)
    s_idx = np.arange(L)[:, None]
    t_idx = np.arange(L)[None, :]
    lag = np.clip(t_idx - s_idx, 0, L - 1)
    kfull = mk[:, lag]
    kfull = jnp.where((t_idx >= s_idx)[None, :, :, None, None], kfull, 0.0)
    kmat = kfull.transpose(0, 1, 4, 2, 3).reshape(g, L * c, L * c).astype(BF16)
    ck = jnp.concatenate([ckr, cki], axis=-1)
    ymat = ck[:, c:, :].transpose(0, 2, 1).astype(BF16)
    bk = jnp.concatenate([bkr, bki], axis=-1).reshape(g, L, c, 2 * S5_STATE)
    sm = bk[:, t_eff - 1::-1] if t_eff == L else bk[:, np.arange(t_eff - 1, -1, -1)]
    if t_eff < L:
        sm = jnp.concatenate([sm, jnp.zeros((g, L - t_eff, c, 2 * S5_STATE), F32)], axis=1)
    smat = sm.reshape(g, L * c, 2 * S5_STATE).astype(BF16)
    ar = pwr[:, t_eff][:, None, :]
    ai = pwi[:, t_eff][:, None, :]
    a1 = jnp.concatenate([ar, ar], axis=-1)
    a2 = jnp.concatenate([-ai, ai], axis=-1)
    return kmat, smat, ymat, a1, a2


def _s5_row_weights(tabs):
    ckr, cki, bkr, bki, m, pwr, pwi = tabs
    c, L, st = S5_GROUP, S5_CHUNK, 2 * S5_STATE
    ng = LANES // c
    nch = S5_GROUPS // ng
    eye = jnp.eye(ng, dtype=BF16)
    mk = m.astype(BF16).reshape(DEPTH, nch, ng, L, c, c)
    w = jnp.einsum('dhgloi,gq->dhlgiqo', mk, eye).reshape(DEPTH, nch, L * LANES, LANES)
    w_shift = jnp.concatenate([jnp.zeros_like(w[:, :, :LANES]), w[:, :, :-LANES]], axis=2)
    w2 = jnp.concatenate([w_shift, w], axis=-1)
    bk = jnp.concatenate([bkr, bki], axis=-1).astype(BF16).reshape(DEPTH, nch, ng, L, c, st)
    sw = jnp.einsum('dhgkcp,gq->dhkgcqp', bk, eye).reshape(DEPTH, nch, L * LANES, ng * st)
    ck = jnp.concatenate([ckr, cki], axis=-1).astype(BF16).reshape(DEPTH, nch, ng, L + 1, c, st)[:, :, :, 1:]
    yw = jnp.einsum('dhgtcp,gq->dhgptqc', ck, eye).reshape(DEPTH, nch, ng * st, L * LANES)
    ar = pwr[:, :, L].reshape(DEPTH, nch, ng, 1, S5_STATE)
    ai = pwi[:, :, L].reshape(DEPTH, nch, ng, 1, S5_STATE)
    a1 = jnp.concatenate([ar, ar], axis=-1)
    a2 = jnp.concatenate([-ai, ai], axis=-1)
    return w2, sw, yw, a1, a2


def _all_layer_weights(w_in, cmp_pe, cmp_w1, cmp_w2, s5_glu_w, w_s5_out, w_nsa_out, w_o, perm):
    gate_w = jnp.pad(w_in[:, :, 2304:2328], ((0, 0), (0, 0), (0, LANES - 3 * N_HEADS)))
    wp = jnp.concatenate([w_in[:, :, :1024], w_in[:, :, 1024:1536][:, :, perm], w_in[:, :, 1536:2304],
                          w_in[:, :, 2328:2840][:, :, perm], w_in[:, :, 2840:], gate_w], axis=2).astype(BF16)
    pe2 = jnp.concatenate([cmp_pe, cmp_pe], axis=-1)[:, :, :, None, :]
    w1bd = _blockdiag2(cmp_w1.astype(BF16).reshape(DEPTH, 2, BLOCK, HEAD_DIM, HEAD_DIM))
    w1bd = w1bd.reshape(DEPTH, 2, BLOCK * KV_W, KV_W)
    w2bd = _blockdiag2(cmp_w2.astype(BF16))
    wt = w_in[:, :, 1536:2304].transpose(0, 2, 1).astype(BF16)
    return dict(wp=wp, wt=wt, pe2=pe2, w1bd=w1bd, w2bd=w2bd, gluw=s5_glu_w.astype(BF16),
                ws5=w_s5_out.astype(BF16), wnsa=w_nsa_out[:, perm, :].astype(BF16),
                wo=w_o.astype(BF16))


def _to_groups(u2d, nbt, nj):
    u5 = u2d.reshape(nbt, nj, S5_CHUNK, S5_GROUPS, S5_GROUP)
    return u5.transpose(3, 1, 0, 2, 4).reshape(S5_GROUPS, nj * nbt, S5_CHUNK * S5_GROUP)


def _from_groups(y_g, nbt, nj):
    y5 = y_g.reshape(S5_GROUPS, nj, nbt, S5_CHUNK, S5_GROUP)
    return y5.transpose(2, 1, 3, 0, 4).reshape(nbt * nj * S5_CHUNK, D_S5)


def _state_to_groups(h):
    return h.transpose(2, 0, 1, 3).reshape(S5_GROUPS, h.shape[0], 2 * S5_STATE)


def _state_from_groups(hg):
    g, b, _ = hg.shape
    return hg.reshape(g, b, 2, S5_STATE).transpose(1, 2, 0, 3)


def kernel(x_prompt, x_sample, c_prompt, c_sample, cache_kv, page_table, state_win, state_ssm, ada_w, ada_b, norm_g, w_in, s5_a_re, s5_a_im, s5_log_dt, s5_b_re, s5_b_im, s5_c_re, s5_c_im, s5_d, s5_glu_w, s5_glu_b, cmp_pe, cmp_w1, cmp_w2, w_s5_out, w_nsa_out, w_o, final_g):
    bp, tp, _ = x_prompt.shape
    bs, ts, _ = x_sample.shape
    n_pool, _, page = cache_kv.shape[:3]
    n_pages = page_table.shape[1]
    past = n_pages * page
    wbuf = state_win.shape[2]
    assert tp % S5_CHUNK == 0 and tp % 128 == 0 and ts <= 8 and bs % 8 == 0 and bp % 8 == 0

    perm = _head_perm()
    mod = _ada_mod(jnp.concatenate([c_prompt, c_sample], axis=0), ada_w, ada_b)
    s5tabs = _s5_setup(s5_a_re, s5_a_im, s5_log_dt, s5_b_re, s5_b_im, s5_c_re, s5_c_im)

    tabs_p = _rope_tables(jnp.arange(tp))
    rs = bs * ts
    tabs_s = _rope_tables(jnp.tile(past + jnp.arange(ts), bs))
    nb_p = tp // BLOCK
    nbp_p = (nb_p + LANES - 1) // LANES * LANES
    ctabs_p = _rope_tables(jnp.arange(nbp_p) * BLOCK + (BLOCK - 1))
    nb_s = (past + ts + BLOCK - 1) // BLOCK
    nbp_s = ((nb_s + 7) // 8 * 8 + LANES - 1) // LANES * LANES
    ctabs_s = _rope_tables(jnp.arange(nbp_s) * BLOCK + (BLOCK - 1))

    cache5 = cache_kv.transpose(0, 1, 3, 4, 5, 2).reshape(n_pool, DEPTH, 4, KV_W, page)
    state_wint = state_win.transpose(0, 1, 3, 4, 5, 2).reshape(bs, DEPTH, 2, KV_W, wbuf)
    tabs_pt = tuple(tb.T for tb in tabs_p)
    tm_p = 512 if tp % 512 == 0 else 256
    tq_p = 128
    nj_p = tp // S5_CHUNK
    consts = _attn_constants(tp, tq_p)
    keep = min(WINDOW, tp)

    xp = x_prompt.reshape(bp * tp, D_MODEL)
    xs = x_sample.reshape(rs, D_MODEL)
    kv_p, kv_s, win_p, win_s, ssm_p, ssm_s = [], [], [], [], [], []

    def pad_rows(a3):
        return jnp.pad(a3, ((0, 0), (0, 8 - ts), (0, 0)))

    def to_chunks(a2):
        return a2.reshape(a2.shape[0], D_S5 // LANES, LANES).transpose(1, 0, 2)

    def new_tiles(a3):
        a4 = a3.reshape(bs, ts, 2, KV_W).transpose(0, 2, 3, 1)
        return jnp.pad(a4, ((0, 0), (0, 0), (0, 0), (0, LANES - ts)))

    lw_all = _all_layer_weights(w_in, cmp_pe, cmp_w1, cmp_w2, s5_glu_w, w_s5_out, w_nsa_out, w_o, perm)
    s5w_all = _s5_row_weights(s5tabs)

    for l in range(DEPTH):
        lw = {name: val[l] for name, val in lw_all.items()}
        g_row = norm_g[l][None, :]
        d_row = s5_d[l].reshape(D_S5 // LANES, 1, LANES)
        glub = s5_glu_b[l][None, :]
        mp, ms = mod[l, :bp], mod[l, bp:]
        post_w = (d_row, lw['gluw'], glub, lw['ws5'], lw['wnsa'], lw['wo'])

        shift, scale, gmod = [mp[:, k * D_MODEL:(k + 1) * D_MODEL][:, None, :] for k in range(3)]
        u, zs5, q, kvt, wint, cmp, znsa, merge, gate = _inproj_prompt(
            xp, shift, scale, g_row, lw['wp'], lw['wt'], tabs_p, tabs_pt, tm=tm_p, nb_batch=bp, t=tp)
        kcvc = _cmp_prompt(cmp, bp, tp, nbp_p, lw['pe2'], lw['w1bd'], lw['w2bd'], ctabs_p)
        o = _nsa_prompt(q, kvt, wint, kcvc, gate, consts, bp, tp, tq_p)
        yssm, hl = _s5_rows(u, *[a[l] for a in s5w_all], bp, tp)
        xp = _post(xp, gmod, yssm, u, zs5, o, znsa, merge, *post_w,
                   tm=tm_p, tiles_per_b=tp // tm_p, per_row_mod=False)
        kv_p.append(kvt)
        win_p.append(wint[:, :, tp - keep:])
        ssm_p.append(_state_from_groups(hl.reshape(S5_GROUPS, bp, 2 * S5_STATE)))

        shift, scale, gmod = [jnp.repeat(ms[:, k * D_MODEL:(k + 1) * D_MODEL], ts, axis=0) for k in range(3)]
        u, zs5, q, kv, win, znsa, merge, gate = _inproj(
            xs, shift, scale, g_row, lw['wp'], tabs_s, tm=rs, tiles_per_b=1, per_row_mod=True)
        kv3 = kv.reshape(bs, ts, 4 * KV_W)
        kcvc = _cmp_sample(page_table, cache5, pad_rows(kv3[:, :, :2 * KV_W]), l, nbp_s,
                           lw['pe2'], lw['w1bd'], lw['w2bd'], ctabs_s, ts)
        o8, wst = _nsa_sample(page_table, cache5, pad_rows(q.reshape(bs, ts, D_NSA)),
                              new_tiles(kv3[:, :, 2 * KV_W:]), new_tiles(win.reshape(bs, ts, 2 * KV_W)),
                              state_wint, kcvc, pad_rows(gate.reshape(bs, ts, LANES)), l, ts)
        o = o8[:, :ts].reshape(rs, D_NSA)
        mats = _s5_matrices(s5tabs, l, ts)
        u_pad = jnp.pad(u.reshape(bs, ts, D_S5), ((0, 0), (0, S5_CHUNK - ts), (0, 0)))
        y_g, hl = _s5_chunk(_to_groups(u_pad.reshape(bs * S5_CHUNK, D_S5), bs, 1).astype(BF16),
                            _state_to_groups(state_ssm[:, l]), *mats, nbt=bs)
        yssm = _from_groups(y_g, bs, 1).reshape(bs, S5_CHUNK, D_S5)[:, :ts].reshape(rs, D_S5)
        xs = _post(xs, gmod, to_chunks(yssm), to_chunks(u), zs5, o, znsa, merge, *post_w,
                   tm=rs, tiles_per_b=1, per_row_mod=True)
        kv_s.append(kv.reshape(bs, ts, 4, N_KV, HEAD_DIM))
        win_s.append(wst)
        ssm_s.append(_state_from_groups(hl))

    def from_t(parts, n_streams):
        a = jnp.stack(parts, axis=1)
        nbt, keys = a.shape[0], a.shape[-1]
        return a.reshape(nbt, DEPTH, n_streams, N_KV, HEAD_DIM, keys).transpose(0, 1, 5, 2, 3, 4)

    fg = final_g[None, :]
    y_prompt = _final_norm(xp, fg, tm_p).reshape(bp, tp, D_MODEL)
    y_sample = _final_norm(xs, fg, rs).reshape(bs, ts, D_MODEL)
    return (y_prompt, y_sample, from_t(kv_p, 4), jnp.stack(kv_s, axis=1),
            from_t(win_p, 2), from_t(win_s, 2),
            jnp.stack(ssm_p, axis=1), jnp.stack(ssm_s, axis=1))
```

```python
import functools
import math

import numpy as np
import jax
import jax.numpy as jnp
from jax import lax
from jax.experimental import pallas as pl
from jax.experimental.pallas import tpu as pltpu

F32 = jnp.float32
BF16 = jnp.bfloat16

D_MODEL = 1024
DEPTH = 4
D_S5 = 512
S5_GROUP = 16
S5_GROUPS = 32
S5_STATE = 64
D_NSA = 512
HEAD_DIM = 64
N_HEADS = 8
N_KV = 2
GROUP_SIZE = 4
BLOCK = 64
N_SELECT = 16
WINDOW = 512
ROT_HALF = 8
ROPE_THETA = 500000.0
RMS_EPS = 1e-6
NEG_INF = -1e30
FORCED_SCORE = 1e4

LANES = 128
S5_CHUNK = 16
KV_W = N_KV * HEAD_DIM
N_PROJ = 4992
CMP_PITCH = 72
VMEM_LIMIT = 56 * 1024 * 1024


def _dot(a, b):
    return jnp.dot(a, b, preferred_element_type=F32)


def _dot_nt(a, b):
    return lax.dot_general(a, b, (((1,), (1,)), ((), ())), preferred_element_type=F32)


def _silu(x):
    return x * jax.nn.sigmoid(x)


def _rope128(x, c, sa, sb):
    return x * c + pltpu.roll(x, LANES - ROT_HALF, 1) * sa + pltpu.roll(x, ROT_HALF, 1) * sb


def _masked_exp(s, mask):
    s = jnp.where(mask, s, NEG_INF)
    m = jnp.max(s, axis=-1, keepdims=True)
    p = jnp.where(mask, jnp.exp(s - m), 0.0)
    return p, jnp.sum(p, axis=-1, keepdims=True)


def _safe_inv(l):
    return jnp.where(l > 0.0, 1.0 / l, 0.0)


def _cparams(sem):
    return pltpu.CompilerParams(dimension_semantics=sem, vmem_limit_bytes=VMEM_LIMIT)


def _ada_kernel(c_ref, w_ref, b_ref, o_ref):
    c = c_ref[...]
    o_ref[...] = _dot(_silu(c).astype(BF16), w_ref[...].astype(BF16)) + b_ref[...]


def _ada_mod(c_all, ada_w, ada_b):
    nc = c_all.shape[0]
    tn = 1024
    return pl.pallas_call(
        _ada_kernel,
        grid=(DEPTH, 3 * D_MODEL // tn),
        in_specs=[pl.BlockSpec((nc, D_MODEL), lambda l, n: (0, 0)),
                  pl.BlockSpec((None, D_MODEL, tn), lambda l, n: (l, 0, n)),
                  pl.BlockSpec((None, 1, tn), lambda l, n: (l, 0, n))],
        out_specs=pl.BlockSpec((None, nc, tn), lambda l, n: (l, 0, n)),
        out_shape=jax.ShapeDtypeStruct((DEPTH, nc, 3 * D_MODEL), F32),
        compiler_params=_cparams(("arbitrary", "arbitrary")),
        name="ada_mod",
    )(c_all, ada_w, ada_b.reshape(DEPTH, 1, 3 * D_MODEL))


def _inproj_kernel(x_ref, shift_ref, scale_ref, g_ref, w_ref, cos_ref, sa_ref, sb_ref,
                   u_ref, zs5_ref, q_ref, kv_ref, win_ref, znsa_ref, merge_ref, gate_ref):
    x = x_ref[...]
    h = x * lax.rsqrt(jnp.mean(x * x, axis=-1, keepdims=True) + RMS_EPS) * g_ref[...]
    h = h * (1.0 + scale_ref[...]) + shift_ref[...]
    hb = h.astype(BF16)

    def mm(lo, hi):
        return _dot(hb, w_ref[:, lo:hi])

    c, sa, sb = cos_ref[...], sa_ref[...], sb_ref[...]
    u_ref[...] = mm(0, 512)
    zs5_ref[...] = mm(512, 1024).astype(BF16)
    for r in range(GROUP_SIZE):
        lo = 1024 + r * LANES
        q = _rope128(mm(lo, lo + LANES), c, sa, sb) * (HEAD_DIM ** -0.5)
        q_ref[:, r * LANES:(r + 1) * LANES] = q.astype(BF16)
    kv_ref[:, 0:256] = mm(1536, 1792)
    kv_ref[:, 256:384] = _rope128(mm(1792, 1920), c, sa, sb)
    kv_ref[:, 384:512] = mm(1920, 2048)
    win_ref[:, 0:128] = _rope128(mm(2048, 2176), c, sa, sb)
    win_ref[:, 128:256] = mm(2176, 2304)
    znsa_ref[...] = mm(2304, 2816).astype(BF16)
    merge_ref[...] = mm(2816, 4864).astype(BF16)
    gate_ref[...] = mm(4864, 4992)


def _inproj(x2d, shift, scale, g, wp, tabs, *, tm, tiles_per_b, per_row_mod):
    rows = x2d.shape[0]
    nt = rows // tm
    n_tab = tabs[0].shape[0] // tm
    if per_row_mod:
        mod_spec = pl.BlockSpec((tm, D_MODEL), lambda i: (i, 0))
    else:
        mod_spec = pl.BlockSpec((None, 1, D_MODEL), lambda i: (i // tiles_per_b, 0, 0))
    tab_spec = pl.BlockSpec((tm, LANES), lambda i: (i % n_tab, 0))

    def row_spec(w):
        return pl.BlockSpec((tm, w), lambda i: (i, 0))

    widths = (512, 512, 512, 512, 256, 512, 2048, 128)
    dtypes = (F32, BF16, BF16, F32, F32, BF16, BF16, F32)
    return pl.pallas_call(
        _inproj_kernel,
        grid=(nt,),
        in_specs=[row_spec(D_MODEL), mod_spec, mod_spec,
                  pl.BlockSpec((1, D_MODEL), lambda i: (0, 0)),
                  pl.BlockSpec((D_MODEL, N_PROJ), lambda i: (0, 0)),
                  tab_spec, tab_spec, tab_spec],
        out_specs=[row_spec(w) for w in widths],
        out_shape=[jax.ShapeDtypeStruct((rows, w), d) for w, d in zip(widths, dtypes)],
        compiler_params=_cparams(("arbitrary",)),
        name="inproj",
    )(x2d, shift, scale, g, wp, *tabs)


def _rope128_t(x, c, sa, sb):
    return x * c + pltpu.roll(x, KV_W - ROT_HALF, 0) * sa + pltpu.roll(x, ROT_HALF, 0) * sb


def _inproj_prompt_kernel(x_ref, shift_ref, scale_ref, g_ref, w_ref, wt_ref, cos_ref, sa_ref, sb_ref,
                          cost_ref, sat_ref, sbt_ref,
                          u_ref, zs5_ref, q_ref, kvt_ref, wint_ref, cmp_ref, znsa_ref, merge_ref, gate_ref):
    x = x_ref[...]
    h = x * lax.rsqrt(jnp.mean(x * x, axis=-1, keepdims=True) + RMS_EPS) * g_ref[...]
    h = h * (1.0 + scale_ref[...]) + shift_ref[...]
    hb = h.astype(BF16)

    def mm(lo, hi):
        return _dot(hb, w_ref[:, lo:hi])

    c, sa, sb = cos_ref[...], sa_ref[...], sb_ref[...]
    for pair in range(D_S5 // (2 * LANES)):
        up = mm(pair * 2 * LANES, (pair + 1) * 2 * LANES)
        u_ref[2 * pair] = up[:, :LANES]
        u_ref[2 * pair + 1] = up[:, LANES:]
    zs5_ref[...] = mm(512, 1024).astype(BF16)
    for pair in range(GROUP_SIZE // 2):
        lo = 1024 + pair * 2 * LANES
        qp = mm(lo, lo + 2 * LANES)
        for k in range(2):
            q = _rope128(qp[:, k * LANES:(k + 1) * LANES], c, sa, sb) * (HEAD_DIM ** -0.5)
            q_ref[:, (2 * pair + k) * LANES:(2 * pair + k + 1) * LANES] = q.astype(BF16)
    cmp_ref[...] = mm(1536, 1792)
    znsa_ref[...] = mm(2304, 2816).astype(BF16)
    merge_ref[...] = mm(2816, 4864).astype(BF16)
    gate_ref[...] = mm(4864, 4992)

    ct, sat, sbt = cost_ref[...], sat_ref[...], sbt_ref[...]
    kvxt = _dot_nt(wt_ref[...], hb)

    def mmt(s):
        return kvxt[s * KV_W:(s + 1) * KV_W, :]

    kvt_ref[0:KV_W, :] = mmt(0)
    kvt_ref[KV_W:2 * KV_W, :] = mmt(1)
    kvt_ref[2 * KV_W:3 * KV_W, :] = _rope128_t(mmt(2), ct, sat, sbt)
    kvt_ref[3 * KV_W:4 * KV_W, :] = mmt(3)
    wint_ref[0:KV_W, :] = _rope128_t(mmt(4), ct, sat, sbt)
    wint_ref[KV_W:2 * KV_W, :] = mmt(5)


def _inproj_prompt(x2d, shift, scale, g, wp, wt, tabs, tabs_t, *, tm, nb_batch, t):
    rows = x2d.shape[0]
    tpb = t // tm
    mod_spec = pl.BlockSpec((None, 1, D_MODEL), lambda i: (i // tpb, 0, 0))
    tab_spec = pl.BlockSpec((tm, LANES), lambda i: (i % tpb, 0))
    tabt_spec = pl.BlockSpec((KV_W, tm), lambda i: (0, i % tpb))
    row = lambda w: pl.BlockSpec((tm, w), lambda i: (i, 0))
    tr = lambda h: pl.BlockSpec((None, h, tm), lambda i: (i // tpb, 0, i % tpb))
    u_spec = pl.BlockSpec((D_S5 // LANES, tm, LANES), lambda i: (0, i, 0))
    outs = [((D_S5 // LANES, rows, LANES), F32, u_spec), ((rows, 512), BF16, row(512)),
            ((rows, 512), BF16, row(512)),
            ((nb_batch, 4 * KV_W, t), F32, tr(4 * KV_W)), ((nb_batch, 2 * KV_W, t), F32, tr(2 * KV_W)),
            ((rows, 256), F32, row(256)), ((rows, 512), BF16, row(512)), ((rows, 2048), BF16, row(2048)),
            ((rows, 128), F32, row(128))]
    return pl.pallas_call(
        _inproj_prompt_kernel,
        grid=(rows // tm,),
        in_specs=[row(D_MODEL), mod_spec, mod_spec,
                  pl.BlockSpec((1, D_MODEL), lambda i: (0, 0)),
                  pl.BlockSpec((D_MODEL, N_PROJ), lambda i: (0, 0)),
                  pl.BlockSpec((6 * KV_W, D_MODEL), lambda i: (0, 0)),
                  tab_spec, tab_spec, tab_spec, tabt_spec, tabt_spec, tabt_spec],
        out_specs=[o[2] for o in outs],
        out_shape=[jax.ShapeDtypeStruct(o[0], o[1]) for o in outs],
        compiler_params=_cparams(("arbitrary",)),
        name="inproj_prompt",
    )(x2d, shift, scale, g, wp, wt, *tabs, *tabs_t)


def _compress(load_p, nbk, pe_ref, w1_ref, w2_ref, c, sa, sb, xcat_sc):
    rows = xcat_sc.shape[0]
    outs = []
    for s in range(2):
        for p in range(BLOCK):
            xp = load_p(s, p) + pe_ref[s, p]
            if rows > nbk:
                xp = jnp.concatenate([xp, jnp.zeros((rows - nbk, KV_W), F32)], axis=0)
            xcat_sc[:, p * KV_W:(p + 1) * KV_W] = xp.astype(BF16)
        acc = _dot(xcat_sc[...], w1_ref[s])[0:nbk]
        outs.append(_dot(_silu(acc).astype(BF16), w2_ref[s]))
    return jnp.concatenate([_rope128(outs[0], c, sa, sb), outs[1]], axis=1)


def _cmp_prompt_kernel(k_ref, v_ref, pe_ref, w1_ref, w2_ref, cos_ref, sa_ref, sb_ref, o_ref, xcat_sc,
                       *, nbk):
    def load_p(s, p):
        return (k_ref, v_ref)[s][pl.ds(p, nbk, stride=BLOCK), :]

    res = _compress(load_p, nbk, pe_ref, w1_ref, w2_ref,
                    cos_ref[0:nbk, :], sa_ref[0:nbk, :], sb_ref[0:nbk, :], xcat_sc)
    o_ref[...] = jnp.zeros(o_ref.shape, F32)
    o_ref[0:nbk, :] = res


def _cmp_prompt(kv2d, nb_batch, t, nbp, pe2, w1bd, w2bd, ctabs):
    nbk = t // BLOCK
    full = lambda shape: pl.BlockSpec(shape, lambda b: (0,) * len(shape))
    return pl.pallas_call(
        functools.partial(_cmp_prompt_kernel, nbk=nbk),
        grid=(nb_batch,),
        in_specs=[pl.BlockSpec((t, KV_W), lambda b: (b, 0)), pl.BlockSpec((t, KV_W), lambda b: (b, 1)),
                  full((2, BLOCK, 1, KV_W)), full((2, BLOCK * KV_W, KV_W)),
                  full((2, KV_W, KV_W)), full((nbp, LANES)), full((nbp, LANES)), full((nbp, LANES))],
        out_specs=pl.BlockSpec((None, nbp, 2 * KV_W), lambda b: (b, 0, 0)),
        out_shape=jax.ShapeDtypeStruct((nb_batch, nbp, 2 * KV_W), F32),
        scratch_shapes=[pltpu.VMEM(((nbk + 15) // 16 * 16, BLOCK * KV_W), BF16)],
        compiler_params=_cparams(("arbitrary",)),
        name="cmp_prompt",
    )(kv2d, kv2d, pe2, w1bd, w2bd, *ctabs)


def _stack_queries(qt, tq):
    lane = lax.broadcasted_iota(jnp.int32, (tq, LANES), 1)
    qt = qt.astype(F32)
    blocks = []
    for g in range(N_KV):
        keep = (lane < HEAD_DIM) if g == 0 else (lane >= HEAD_DIM)
        for r in range(GROUP_SIZE):
            blocks.append(jnp.where(keep, qt[:, r * LANES:(r + 1) * LANES], 0.0))
    return jnp.concatenate(blocks, axis=0).astype(BF16)


def _select_blocks(imp, qblk, nb, n_sel):
    nq, nbp = imp.shape
    n = lax.broadcasted_iota(jnp.int32, (nq, nbp), 1)
    forced = (n == 0) | (n == qblk) | (n == qblk - 1)
    imp = jnp.where(forced, FORCED_SCORE, imp)
    imp = jnp.where(n > qblk, -1.0, imp)
    imp = jnp.where(n >= nb, -2.0, imp)
    rank = jnp.zeros((nq, nbp), F32)
    for m in range(nb):
        col = imp[:, m:m + 1]
        beats = (col > imp) | ((col == imp) & (n > m))
        rank = rank + jnp.where(beats, 1.0, 0.0)
    return jnp.where((rank < n_sel) & (imp > -0.5), 1.0, 0.0)


def _compressed_branch(qs, kcvc, tq_col, tq, nb):
    nbp = kcvc.shape[0]
    kc = kcvc[:, :KV_W].astype(BF16)
    vc = kcvc[:, KV_W:].astype(BF16)
    s = _dot_nt(qs, kc)
    n = lax.broadcasted_iota(jnp.int32, (1, nbp), 1)
    mask = (n * BLOCK + (BLOCK - 1) <= tq_col) & (n < nb)
    p, l = _masked_exp(s, mask)
    p = p * _safe_inv(l)
    o_c = _dot(p.astype(BF16), vc)
    imps = []
    for g in range(N_KV):
        acc = p[(g * GROUP_SIZE) * tq:(g * GROUP_SIZE + 1) * tq]
        for r in range(1, GROUP_SIZE):
            acc = acc + p[(g * GROUP_SIZE + r) * tq:(g * GROUP_SIZE + r + 1) * tq]
        imps.append(acc)
    return o_c, imps


def _block_mask(selstack, k0, tk):
    nbp = selstack.shape[1]
    kblk = (k0 + lax.broadcasted_iota(jnp.int32, (nbp, tk), 1)) // BLOCK
    e = jnp.where(kblk == lax.broadcasted_iota(jnp.int32, (nbp, tk), 0), 1.0, 0.0).astype(BF16)
    return _dot(selstack, e) > 0.5


def _gated_heads(gt, branches, tq):
    lane = lax.broadcasted_iota(jnp.int32, (tq, LANES), 1)
    chunks = []
    for r in range(GROUP_SIZE):
        halves = []
        for g in range(N_KV):
            h = g * GROUP_SIZE + r
            acc = None
            for br, ob in enumerate(branches):
                term = gt[:, br * N_HEADS + h:br * N_HEADS + h + 1] * ob[h * tq:(h + 1) * tq]
                acc = term if acc is None else acc + term
            halves.append(acc)
        chunks.append(jnp.where(lane < HEAD_DIM, halves[0], halves[1]))
    return chunks


def _select_blocks_t(imp, qblk_row, nb, n_sel):
    nq, nbp = imp.shape
    nbr = (nb + 7) // 8 * 8
    x = imp.T[0:nbr, :]
    n = lax.broadcasted_iota(jnp.int32, (nbr, nq), 0)
    forced = (n == 0) | (n == qblk_row) | (n == qblk_row - 1)
    x = jnp.where(forced, FORCED_SCORE, x)
    x = jnp.where(n > qblk_row, -1.0, x)
    x = jnp.where(n >= nb, -2.0, x)
    rank = jnp.zeros((nbr, nq), F32)
    for m in range(nb):
        row = x[m:m + 1, :]
        beats = (row > x) | ((row == x) & (n > m))
        rank = rank + jnp.where(beats, 1.0, 0.0)
    sel = jnp.where((rank < n_sel) & (x > -0.5), 1.0, 0.0)
    if nbr < nbp:
        sel = jnp.concatenate([sel, jnp.zeros((nbp - nbr, nq), F32)], axis=0)
    return sel.T


def _nsa_prompt_kernel(q_ref, kvt_ref, wint_ref, kcvc_ref, gate_ref, e_ref, cbias_ref, wbias_ref,
                       o_ref, kaug_sc, vsel_sc, waug_sc, vwin_sc, s_sc, m_sc, l_sc, acc_sc, sel_sc,
                       *, tq, t, nb, n_sel):
    i = pl.program_id(1)
    q0 = pl.multiple_of(i * tq, tq)
    rows = N_HEADS * tq
    wk = WINDOW + tq
    tk = 2 * LANES
    neg = jnp.asarray(NEG_INF, F32)

    @pl.when(i == 0)
    def _():
        kaug_sc[0:KV_W, :] = kvt_ref[0:KV_W, :].astype(BF16)
        kaug_sc[KV_W:2 * KV_W, :] = e_ref[...]
        vsel_sc[...] = kvt_ref[KV_W:2 * KV_W, :].astype(BF16)
        waug_sc[...] = jnp.zeros(waug_sc.shape, BF16)
        waug_sc[0:KV_W, WINDOW:] = wint_ref[0:KV_W, :].astype(BF16)
        waug_sc[KV_W:KV_W + 16, 0:WINDOW] = jnp.ones((16, WINDOW), BF16)
        vwin_sc[:, 0:WINDOW] = jnp.zeros((KV_W, WINDOW), BF16)
        vwin_sc[:, WINDOW:] = wint_ref[KV_W:2 * KV_W, :].astype(BF16)

    qs = _stack_queries(q_ref[...], tq)
    t_loc = lax.broadcasted_iota(jnp.int32, (rows, 1), 0) % tq
    tq_col = q0 + t_loc

    o_c, imps = _compressed_branch(qs, kcvc_ref[...], tq_col, tq, nb)

    sel_sc[...] = jnp.ones(sel_sc.shape, F32)

    @pl.when(q0 + tq > n_sel * BLOCK)
    def _():
        qblk_row = (q0 + lax.broadcasted_iota(jnp.int32, (1, tq), 1)) // BLOCK
        for g in range(N_KV):
            sel_sc[g] = _select_blocks_t(imps[g], qblk_row, nb, n_sel)

    selb = [((sel_sc[g] - 1.0) * (-NEG_INF)).astype(BF16) for g in range(N_KV)]
    selstack = jnp.concatenate([selb[g] for g in range(N_KV) for _ in range(GROUP_SIZE)], axis=0)
    qaug = jnp.concatenate([qs, selstack], axis=1)

    lane = lax.broadcasted_iota(jnp.int32, (rows, LANES), 1)
    padcol = jnp.where(lane == 0, neg, 0.0).astype(BF16)
    qaug_w = jnp.concatenate([qs, padcol], axis=1)
    n_full = q0 // tk
    par = (q0 // tq) % 2
    m_sc[...] = jnp.full(m_sc.shape, neg, F32)

    def score_tile(k0, w, bias):
        s = _dot(qaug, kaug_sc[:, pl.ds(k0, w)])
        if bias is not None:
            s = s + bias
        s_sc[:, pl.ds(k0, w)] = s
        mx = s[:, :LANES]
        for c in range(1, w // LANES):
            mx = jnp.maximum(mx, s[:, c * LANES:(c + 1) * LANES])
        m_sc[...] = jnp.maximum(m_sc[...], mx)

    def pass1(j, carry):
        score_tile(pl.multiple_of(j * 2 * tk, 2 * tk), 2 * tk, None)
        return carry

    lax.fori_loop(0, n_full // 2, pass1, 0)

    @pl.when(n_full % 2 == 1)
    def _():
        score_tile(pl.multiple_of((n_full - 1) * tk, tk), tk, None)

    score_tile(pl.multiple_of(n_full * tk, tk), tk, cbias_ref[par])
    m_b = jnp.broadcast_to(jnp.max(m_sc[...], axis=-1, keepdims=True), (rows, LANES))

    l_sc[...] = jnp.zeros(l_sc.shape, F32)
    acc_sc[...] = jnp.zeros(acc_sc.shape, F32)

    def pv_tile(k0, w, m_b):
        s = s_sc[:, pl.ds(k0, w)]
        ps = [jnp.exp(s[:, c * LANES:(c + 1) * LANES] - m_b) for c in range(w // LANES)]
        tot = ps[0]
        for pc in ps[1:]:
            tot = tot + pc
        l_sc[...] = l_sc[...] + tot
        p = jnp.concatenate(ps, axis=1).astype(BF16)
        acc_sc[...] = acc_sc[...] + _dot_nt(p, vsel_sc[:, pl.ds(k0, w)])

    def pass2(j, carry):
        pv_tile(pl.multiple_of(j * 2 * tk, 2 * tk), 2 * tk, m_b)
        return carry

    lax.fori_loop(0, (n_full + 1) // 2, pass2, 0)

    @pl.when(n_full % 2 == 0)
    def _():
        pv_tile(pl.multiple_of(n_full * tk, tk), tk, m_b)

    o_s = acc_sc[...] / jnp.sum(l_sc[...], axis=-1, keepdims=True)

    s = _dot(qaug_w, waug_sc[:, pl.ds(q0, wk)]) + wbias_ref[...]
    p = jnp.exp(s - jnp.max(s, axis=-1, keepdims=True))
    o_w = _dot_nt(p.astype(BF16), vwin_sc[:, pl.ds(q0, wk)]) / jnp.sum(p, axis=-1, keepdims=True)

    for r, ch in enumerate(_gated_heads(jax.nn.sigmoid(gate_ref[...]), (o_c, o_s, o_w), tq)):
        o_ref[:, r * LANES:(r + 1) * LANES] = ch


def _attn_constants(t, tq):
    rows = N_HEADS * tq
    t_loc = np.arange(rows)[:, None] % tq
    k = np.arange(2 * LANES)[None, :]
    cb = np.stack([np.where(k <= t_loc, 0.0, NEG_INF), np.where(k - LANES <= t_loc, 0.0, NEG_INF)])
    kr = np.arange(WINDOW + tq)[None, :]
    wb = np.where((kr - WINDOW <= t_loc) & (kr > t_loc), 0.0, NEG_INF)
    e = (np.arange(t)[None, :] // BLOCK == np.arange(LANES)[:, None]).astype(np.float32)
    return jnp.asarray(e, BF16), jnp.asarray(cb, F32), jnp.asarray(wb, F32)


def _nsa_prompt(q2d, kvt, wint, kcvc, gate2d, consts, nb_batch, t, tq):
    nb = t // BLOCK
    nq = t // tq
    nbp = kcvc.shape[1]
    rows = N_HEADS * tq
    wk = WINDOW + tq
    assert nbp == LANES and t % (2 * LANES) == 0
    e, cb, wb = consts
    kern = functools.partial(_nsa_prompt_kernel, tq=tq, t=t, nb=nb, n_sel=min(N_SELECT, nb))
    return pl.pallas_call(
        kern,
        grid=(nb_batch, nq),
        in_specs=[pl.BlockSpec((tq, D_NSA), lambda b, i: (b * nq + i, 0)),
                  pl.BlockSpec((None, 2 * KV_W, t), lambda b, i: (b, 1, 0)),
                  pl.BlockSpec((None, 2 * KV_W, t), lambda b, i: (b, 0, 0)),
                  pl.BlockSpec((None, nbp, 2 * KV_W), lambda b, i: (b, 0, 0)),
                  pl.BlockSpec((tq, LANES), lambda b, i: (b * nq + i, 0)),
                  pl.BlockSpec((LANES, t), lambda b, i: (0, 0)),
                  pl.BlockSpec((2, rows, 2 * LANES), lambda b, i: (0, 0, 0)),
                  pl.BlockSpec((rows, wk), lambda b, i: (0, 0))],
        out_specs=pl.BlockSpec((tq, D_NSA), lambda b, i: (b * nq + i, 0)),
        out_shape=jax.ShapeDtypeStruct((nb_batch * t, D_NSA), F32),
        scratch_shapes=[pltpu.VMEM((2 * KV_W, t), BF16), pltpu.VMEM((KV_W, t), BF16),
                        pltpu.VMEM((2 * KV_W, WINDOW + t), BF16), pltpu.VMEM((KV_W, WINDOW + t), BF16),
                        pltpu.VMEM((rows, t), F32), pltpu.VMEM((rows, LANES), F32),
                        pltpu.VMEM((rows, LANES), F32), pltpu.VMEM((rows, KV_W), F32),
                        pltpu.VMEM((N_KV, tq, LANES), F32)],
        compiler_params=_cparams(("arbitrary", "arbitrary")),
        name="nsa_prompt",
    )(q2d, kvt, wint, kcvc, gate2d, e, cb, wb)


def _page_copy(cache_ref, slabs, sem_ref, pt_ref, layer, b, p, slot, s0, page, s):
    return pltpu.make_async_copy(
        cache_ref.at[pt_ref[b, p], layer, s0 + s],
        slabs[s].at[slot, :, pl.ds(p * page, page)],
        sem_ref.at[slot])


def _gather_pages(cache_ref, slabs, sem_ref, pt_ref, layer, s0, n_pages, page):
    b = pl.program_id(0)
    nb_batch = pl.num_programs(0)
    slot = b % 2

    def copies(bb, sl):
        return [_page_copy(cache_ref, slabs, sem_ref, pt_ref, layer, bb, p, sl, s0, page, s)
                for p in range(n_pages) for s in range(2)]

    @pl.when(b == 0)
    def _():
        for cp in copies(0, 0):
            cp.start()

    @pl.when(b + 1 < nb_batch)
    def _():
        for cp in copies(b + 1, 1 - slot):
            cp.start()

    for cp in copies(b, slot):
        cp.wait()
    return slot


def _cmp_sample_kernel(pt_ref, cache_ref, new_ref, pe_ref, w1_ref, w2_ref, cos_ref, sa_ref, sb_ref,
                       o_ref, stagek_ref, stagev_ref, rowk_ref, rowv_ref, xcat_sc, sem_ref,
                       *, layer, n_pages, page, nbk, nbs):
    stages = (stagek_ref, stagev_ref)
    rowm = (rowk_ref, rowv_ref)
    i = pl.program_id(0)
    n_steps = pl.num_programs(0)

    def copies(step):
        return [_page_copy(cache_ref, stages, sem_ref, pt_ref, layer, step * nbs + bl, p, bl, 0, page, s)
                for bl in range(nbs) for p in range(n_pages) for s in range(2)]

    @pl.when(i == 0)
    def _():
        for cp in copies(0):
            cp.start()

    for cp in copies(i):
        cp.wait()

    bpp = page // BLOCK
    nb_past = n_pages * bpp
    unroll = next(k for k in (8, 4, 2, 1) if n_pages % k == 0)
    rows_b = nbk * CMP_PITCH
    for s in range(2):
        for bl in range(nbs):
            def xpose(it, carry):
                for k in range(unroll):
                    p = it * unroll + k
                    c0 = pl.multiple_of(p * page, page)
                    tile = stages[s][bl, :, pl.ds(c0, page)].T
                    for h in range(bpp):
                        r0 = pl.multiple_of(bl * rows_b + (p * bpp + h) * CMP_PITCH, 8)
                        rowm[s][pl.ds(r0, BLOCK), :] = tile[h * BLOCK:(h + 1) * BLOCK, :]
                return carry

            lax.fori_loop(0, n_pages // unroll, xpose, 0)
            tail0 = bl * rows_b + nb_past * CMP_PITCH
            tail = (bl + 1) * rows_b - tail0
            rowm[s][pl.ds(tail0, tail), :] = jnp.zeros((tail, KV_W), F32)
            rowm[s][pl.ds(tail0, 8), :] = new_ref[bl, :, s * KV_W:(s + 1) * KV_W]

    @pl.when(i + 1 < n_steps)
    def _():
        for cp in copies(i + 1):
            cp.start()

    def load_p(s, p):
        return rowm[s][pl.ds(p, nbs * nbk, stride=CMP_PITCH), :]

    tabs = [jnp.concatenate([t_ref[0:nbk, :]] * nbs, axis=0) for t_ref in (cos_ref, sa_ref, sb_ref)]
    res = _compress(load_p, nbs * nbk, pe_ref, w1_ref, w2_ref, *tabs, xcat_sc)
    o_ref[...] = jnp.zeros(o_ref.shape, F32)
    for bl in range(nbs):
        o_ref[bl, 0:nbk, :] = res[bl * nbk:(bl + 1) * nbk]


def _cmp_sample(page_table, cache5, newcmp, layer, nbp, pe2, w1bd, w2bd, ctabs, t_new):
    nb_batch, n_pages = page_table.shape
    page = cache5.shape[4]
    past = n_pages * page
    nbk = ((past + t_new + BLOCK - 1) // BLOCK + 7) // 8 * 8
    nbs = 2
    slab_rows = nbs * nbk * CMP_PITCH
    assert page % BLOCK == 0 and t_new <= 8 and nb_batch % nbs == 0
    full = lambda shape: pl.BlockSpec(shape, lambda b, pt: (0,) * len(shape))
    kern = functools.partial(_cmp_sample_kernel, layer=layer, n_pages=n_pages, page=page, nbk=nbk,
                             nbs=nbs)
    return pl.pallas_call(
        kern,
        grid_spec=pltpu.PrefetchScalarGridSpec(
            num_scalar_prefetch=1,
            grid=(nb_batch // nbs,),
            in_specs=[pl.BlockSpec(memory_space=pl.ANY),
                      pl.BlockSpec((nbs, 8, 2 * KV_W), lambda b, pt: (b, 0, 0)),
                      full((2, BLOCK, 1, KV_W)), full((2, BLOCK * KV_W, KV_W)),
                      full((2, KV_W, KV_W)), full((nbp, LANES)), full((nbp, LANES)),
                      full((nbp, LANES))],
            out_specs=pl.BlockSpec((nbs, nbp, 2 * KV_W), lambda b, pt: (b, 0, 0)),
            scratch_shapes=[pltpu.VMEM((nbs, KV_W, past), F32), pltpu.VMEM((nbs, KV_W, past), F32),
                            pltpu.VMEM((slab_rows, KV_W), F32), pltpu.VMEM((slab_rows, KV_W), F32),
                            pltpu.VMEM(((nbs * nbk + 15) // 16 * 16, BLOCK * KV_W), BF16),
                            pltpu.SemaphoreType.DMA((nbs,))]),
        out_shape=jax.ShapeDtypeStruct((nb_batch, nbp, 2 * KV_W), F32),
        compiler_params=_cparams(("arbitrary",)),
        name="cmp_sample",
    )(page_table, cache5, newcmp, pe2, w1bd, w2bd, *ctabs)


def _nsa_sample_kernel(pt_ref, cache_ref, q_ref, newt_ref, wnewt_ref, swint_ref, kcvc_ref, gate_ref,
                       o_ref, wout_ref, slabk_ref, slabv_ref, wslab_ref, sem_ref,
                       *, layer, n_pages, page, t_new, nb, n_sel):
    tq = 8
    past = n_pages * page
    rows = N_HEADS * tq
    slabs = (slabk_ref, slabv_ref)
    slot = _gather_pages(cache_ref, slabs, sem_ref, pt_ref, layer, 2, n_pages, page)
    nk = slabk_ref.shape[2]
    for s in range(2):
        slabs[s][slot, :, pl.ds(past, LANES)] = newt_ref[s]

    qs = _stack_queries(q_ref[...], tq)
    t_loc = lax.broadcasted_iota(jnp.int32, (rows, 1), 0) % tq
    tq_col = past + t_loc

    o_c, imps = _compressed_branch(qs, kcvc_ref[...], tq_col, tq, nb)
    qblk = (past + lax.broadcasted_iota(jnp.int32, (tq, 1), 0)) // BLOCK
    sels = [_select_blocks(imps[g], qblk, nb, n_sel).astype(BF16) for g in range(N_KV)]
    selstack = jnp.concatenate([sels[g] for g in range(N_KV) for _ in range(GROUP_SIZE)], axis=0)

    s = _dot(qs, slabk_ref[slot].astype(BF16))
    kpos = lax.broadcasted_iota(jnp.int32, (1, nk), 1)
    p, l = _masked_exp(s, _block_mask(selstack, 0, nk) & (kpos <= tq_col))
    o_s = _dot_nt(p.astype(BF16), slabv_ref[slot].astype(BF16)) * _safe_inv(l)

    wbuf = swint_ref.shape[2]
    wk = wslab_ref.shape[2]
    wslab_ref[:, :, 0:wbuf] = swint_ref[...]
    wslab_ref[:, :, wbuf:wk] = wnewt_ref[...]
    s = _dot(qs, wslab_ref[0].astype(BF16))
    kpos = past - wbuf + lax.broadcasted_iota(jnp.int32, (1, wk), 1)
    p, l = _masked_exp(s, (kpos <= tq_col) & (kpos > tq_col - WINDOW) & (kpos >= 0))
    o_w = _dot_nt(p.astype(BF16), wslab_ref[1].astype(BF16)) * _safe_inv(l)
    wout_ref[...] = wslab_ref[:, :, t_new:t_new + wbuf]

    for r, ch in enumerate(_gated_heads(jax.nn.sigmoid(gate_ref[...]), (o_c, o_s, o_w), tq)):
        o_ref[:, r * LANES:(r + 1) * LANES] = ch


def _nsa_sample(page_table, cache5, q3, newselt, wnewt, state_wint, kcvc, gate3, layer, t_new):
    nb_batch, n_pages = page_table.shape
    page = cache5.shape[4]
    past = n_pages * page
    nb = (past + t_new + BLOCK - 1) // BLOCK
    nk = past + LANES
    wbuf = state_wint.shape[4]
    wk = wbuf + LANES
    nbp = kcvc.shape[1]
    kern = functools.partial(_nsa_sample_kernel, layer=layer, n_pages=n_pages, page=page,
                             t_new=t_new, nb=nb, n_sel=min(N_SELECT, nb))
    per_b = lambda d1, d2: pl.BlockSpec((None, d1, d2), lambda b, pt: (b, 0, 0))
    per_b4 = lambda d1, d2, d3: pl.BlockSpec((None, d1, d2, d3), lambda b, pt: (b, 0, 0, 0))
    return pl.pallas_call(
        kern,
        grid_spec=pltpu.PrefetchScalarGridSpec(
            num_scalar_prefetch=1,
            grid=(nb_batch,),
            in_specs=[pl.BlockSpec(memory_space=pl.ANY),
                      per_b(8, D_NSA), per_b4(2, KV_W, LANES), per_b4(2, KV_W, LANES),
                      pl.BlockSpec((None, None, 2, KV_W, wbuf), lambda b, pt: (b, layer, 0, 0, 0)),
                      per_b(nbp, 2 * KV_W), per_b(8, LANES)],
            out_specs=[per_b(8, D_NSA), per_b4(2, KV_W, wbuf)],
            scratch_shapes=[pltpu.VMEM((2, KV_W, nk), F32), pltpu.VMEM((2, KV_W, nk), F32),
                            pltpu.VMEM((2, KV_W, wk), F32), pltpu.SemaphoreType.DMA((2,))]),
        out_shape=[jax.ShapeDtypeStruct((nb_batch, 8, D_NSA), F32),
                   jax.ShapeDtypeStruct((nb_batch, 2, KV_W, wbuf), F32)],
        compiler_params=_cparams(("arbitrary",)),
        name="nsa_sample",
    )(page_table, cache5, q3, newselt, wnewt, state_wint, kcvc, gate3)


def _s5_setup_kernel(are_ref, aim_ref, ldt_ref, bre_ref, bim_ref, cre_ref, cim_ref,
                     ckr_ref, cki_ref, bkr_ref, bki_ref, m_ref, pwr_ref, pwi_ref):
    nk = pwr_ref.shape[1]

    def body(g, carry):
        ar = are_ref[g]
        ai = aim_ref[g]
        dt = jnp.exp(ldt_ref[g])
        mag = jnp.exp(ar * dt)
        abr = mag * jnp.cos(ai * dt)
        abi = mag * jnp.sin(ai * dt)
        den = ar * ar + ai * ai
        nr = abr - 1.0
        e_re = (nr * ar + abi * ai) / den
        e_im = (abi * ar - nr * ai) / den
        bre = bre_ref[g]
        bim = bim_ref[g]
        bbr = e_re * bre - e_im * bim
        bbi = e_re * bim + e_im * bre
        kk = lax.broadcasted_iota(jnp.int32, (nk, S5_STATE), 0).astype(F32)
        pmag = jnp.exp(kk * (ar * dt))
        pwr = pmag * jnp.cos(kk * (ai * dt))
        pwi = pmag * jnp.sin(kk * (ai * dt))
        pwr_ref[g] = pwr
        pwi_ref[g] = pwi
        cre = cre_ref[g]
        cim = cim_ref[g]
        for k in range(S5_CHUNK + 1):
            wr = pwr[k:k + 1, :]
            wi = pwi[k:k + 1, :]
            ckr_ref[g, k * S5_GROUP:(k + 1) * S5_GROUP, :] = cre * wr - cim * wi
            cki_ref[g, k * S5_GROUP:(k + 1) * S5_GROUP, :] = -(cre * wi + cim * wr)
            if k < S5_CHUNK:
                bkr_ref[g, k * S5_GROUP:(k + 1) * S5_GROUP, :] = bbr * wr - bbi * wi
                bki_ref[g, k * S5_GROUP:(k + 1) * S5_GROUP, :] = bbr * wi + bbi * wr
        nl = S5_CHUNK * S5_GROUP
        hp = lax.Precision.HIGHEST
        m_ref[g] = (lax.dot_general(ckr_ref[g, 0:nl, :], bbr, (((1,), (1,)), ((), ())),
                                    precision=hp, preferred_element_type=F32)
                    + lax.dot_general(cki_ref[g, 0:nl, :], bbi, (((1,), (1,)), ((), ())),
                                      precision=hp, preferred_element_type=F32))
        return carry

    lax.fori_loop(0, S5_GROUPS, body, 0)


def _s5_setup(a_re, a_im, log_dt, b_re, b_im, c_re, c_im):
    g, p, c = S5_GROUPS, S5_STATE, S5_GROUP
    nk = 24
    outs = [(DEPTH, g, (S5_CHUNK + 1) * c, p), (DEPTH, g, (S5_CHUNK + 1) * c, p),
            (DEPTH, g, S5_CHUNK * c, p), (DEPTH, g, S5_CHUNK * c, p),
            (DEPTH, g, S5_CHUNK * c, c), (DEPTH, g, nk, p), (DEPTH, g, nk, p)]
    lay = lambda s: pl.BlockSpec((None,) + s[1:], lambda l: (l,) + (0,) * (len(s) - 1))
    ins = [a_re.reshape(DEPTH, g, 1, p), a_im.reshape(DEPTH, g, 1, p),
           jnp.broadcast_to(log_dt[:, :, None, None], (DEPTH, g, 1, p)),
           b_re.transpose(0, 1, 3, 2), b_im.transpose(0, 1, 3, 2), c_re, c_im]
    return pl.pallas_call(
        _s5_setup_kernel,
        grid=(DEPTH,),
        in_specs=[lay(x.shape) for x in ins],
        out_specs=[lay(s) for s in outs],
        out_shape=[jax.ShapeDtypeStruct(s, F32) for s in outs],
        compiler_params=_cparams(("arbitrary",)),
        name="s5_setup",
    )(*ins)


def _s5_chunk_kernel(u_ref, h0_ref, kmat_ref, smat_ref, ymat_ref, a1_ref, a2_ref,
                     y_ref, hl_ref, s_sc, hp_sc, *, gb, nj, nbt):
    for gi in range(gb):
        s_sc[gi] = _dot(u_ref[gi], smat_ref[gi])

    def step(j, hs):
        r0 = pl.multiple_of(j * nbt, nbt)
        new = []
        for gi in range(gb):
            h = hs[gi]
            hp_sc[gi, pl.ds(r0, nbt), :] = h
            new.append(a1_ref[gi] * h + a2_ref[gi] * pltpu.roll(h, S5_STATE, 1)
                       + s_sc[gi, pl.ds(r0, nbt), :])
        return tuple(new)

    hs = lax.fori_loop(0, nj, step, tuple(h0_ref[gi] for gi in range(gb)))
    for gi in range(gb):
        hl_ref[gi] = hs[gi]
        y_ref[gi] = _dot(u_ref[gi], kmat_ref[gi]) + _dot(hp_sc[gi].astype(BF16), ymat_ref[gi])


def _s5_chunk(u_g, h0_g, kmat, smat, ymat, a1, a2, nbt):
    g, rows, w = u_g.shape
    nj = rows // nbt
    gb = 8
    st = 2 * S5_STATE
    blk = lambda d1, d2: pl.BlockSpec((gb, d1, d2), lambda i: (i, 0, 0))
    kern = functools.partial(_s5_chunk_kernel, gb=gb, nj=nj, nbt=nbt)
    return pl.pallas_call(
        kern,
        grid=(g // gb,),
        in_specs=[blk(rows, w), blk(nbt, st), blk(w, w), blk(w, st), blk(st, w), blk(1, st), blk(1, st)],
        out_specs=[blk(rows, w), blk(nbt, st)],
        out_shape=[jax.ShapeDtypeStruct((g, rows, w), F32), jax.ShapeDtypeStruct((g, nbt, st), F32)],
        scratch_shapes=[pltpu.VMEM((gb, rows, st), F32), pltpu.VMEM((gb, rows, st), F32)],
        compiler_params=_cparams(("arbitrary",)),
        name="s5_chunk",
    )(u_g, h0_g, kmat, smat, ymat, a1, a2)


def _s5_rows_kernel(u_ref, w_ref, sw_ref, yw_ref, a1_ref, a2_ref, y_ref, hl_ref,
                    xr_sc, sg_sc, sgs_sc, hp_sc, hcat_sc, *, nbl, nj):
    L = S5_CHUNK
    mr = nbl * nj
    ng = LANES // S5_GROUP
    st = 2 * S5_STATE
    for s in range(L):
        xr_sc[:, (L - 1 - s) * LANES:(L - s) * LANES] = u_ref[pl.ds(s, mr, stride=L), :].astype(BF16)
    sall = _dot(xr_sc[...], sw_ref[...])
    for g in range(ng):
        s_g = sall[:, g * st:(g + 1) * st]
        sg_sc[g] = s_g
        sgs_sc[g] = pltpu.roll(s_g, S5_STATE, 1)

    def step(j, carry):
        hs, hss = carry
        new, news = [], []
        for g in range(ng):
            h, hsw = hs[g], hss[g]
            a1, a2 = a1_ref[g], a2_ref[g]
            hp_sc.at[g][pl.ds(j, nbl, stride=nj), :] = h
            new.append(a1 * h + a2 * hsw + sg_sc.at[g][pl.ds(j, nbl, stride=nj), :])
            news.append(a1 * hsw - a2 * h + sgs_sc.at[g][pl.ds(j, nbl, stride=nj), :])
        return tuple(new), tuple(news)

    zero = tuple(jnp.zeros((nbl, st), F32) for _ in range(ng))
    hs, _ = lax.fori_loop(0, nj, step, (zero, zero))
    for g in range(ng):
        hl_ref[g] = hs[g]
        hcat_sc[:, g * st:(g + 1) * st] = hp_sc[g].astype(BF16)
    for tp in range(L // 2):
        t1 = 2 * tp + 1
        y = (_dot(xr_sc[:, (L - 1 - t1) * LANES:], w_ref[0:(t1 + 1) * LANES, :])
             + _dot(hcat_sc[...], yw_ref[:, (t1 - 1) * LANES:(t1 + 1) * LANES]))
        y_ref[pl.ds(t1 - 1, mr, stride=L), :] = y[:, :LANES]
        y_ref[pl.ds(t1, mr, stride=L), :] = y[:, LANES:]


def _s5_rows(u4, w, sw, yw, a1, a2, nb_batch, t):
    nch, rows, _ = u4.shape
    nsplit = 2
    nbl = nb_batch // nsplit
    nj = t // S5_CHUNK
    rb = nbl * t
    mr = nbl * nj
    ng = LANES // S5_GROUP
    st = 2 * S5_STATE
    kern = functools.partial(_s5_rows_kernel, nbl=nbl, nj=nj)
    per_c = lambda *s: pl.BlockSpec((None,) + s, lambda c, h: (c,) + (0,) * len(s))
    return pl.pallas_call(
        kern,
        grid=(nch, nsplit),
        in_specs=[pl.BlockSpec((None, rb, LANES), lambda c, h: (c, h, 0)),
                  per_c(S5_CHUNK * LANES, 2 * LANES), per_c(S5_CHUNK * LANES, ng * st),
                  per_c(ng * st, S5_CHUNK * LANES), per_c(ng, 1, st), per_c(ng, 1, st)],
        out_specs=[pl.BlockSpec((None, rb, LANES), lambda c, h: (c, h, 0)),
                   pl.BlockSpec((None, ng, None, nbl, st), lambda c, h: (c, 0, h, 0, 0))],
        out_shape=[jax.ShapeDtypeStruct((nch, rows, LANES), F32),
                   jax.ShapeDtypeStruct((nch, ng, nsplit, nbl, st), F32)],
        scratch_shapes=[pltpu.VMEM((mr, S5_CHUNK * LANES), BF16), pltpu.VMEM((ng, mr, st), F32),
                        pltpu.VMEM((ng, mr, st), F32), pltpu.VMEM((ng, mr, st), F32),
                        pltpu.VMEM((mr, ng * st), BF16)],
        compiler_params=_cparams(("arbitrary", "arbitrary")),
        name="s5_rows",
    )(u4, w, sw, yw, a1, a2)


def _post_kernel(x_ref, gmod_ref, yssm_ref, u_ref, zs5_ref, o_ref, znsa_ref, merge_ref,
                 d_ref, gluw_ref, glub_ref, ws5_ref, wnsa_ref, wo_ref, xo_ref):
    y = jnp.concatenate([yssm_ref[ch] + d_ref[ch] * u_ref[ch] for ch in range(D_S5 // LANES)], axis=1)
    y = 0.5 * y * (1.0 + jnp.tanh(math.sqrt(2.0 / math.pi) * (y + 0.044715 * (y * y * y))))
    y = y * jax.nn.sigmoid(_dot(y.astype(BF16), gluw_ref[...]) + glub_ref[...])
    y = y * _silu(zs5_ref[...].astype(F32))
    b_s5 = _dot(y.astype(BF16), ws5_ref[...])
    b_nsa = _dot((o_ref[...] * _silu(znsa_ref[...].astype(F32))).astype(BF16), wnsa_ref[...])
    m = jax.nn.sigmoid(merge_ref[...].astype(F32))
    mix = m[:, :D_MODEL] * b_s5 + m[:, D_MODEL:] * b_nsa
    xo_ref[...] = x_ref[...] + gmod_ref[...] * _dot(mix.astype(BF16), wo_ref[...])


def _post(x2d, gmod, yssm, u, zs5, o, znsa, merge, d, gluw, glub, ws5, wnsa, wo,
          *, tm, tiles_per_b, per_row_mod):
    rows = x2d.shape[0]
    if per_row_mod:
        mod_spec = pl.BlockSpec((tm, D_MODEL), lambda i: (i, 0))
    else:
        mod_spec = pl.BlockSpec((None, 1, D_MODEL), lambda i: (i // tiles_per_b, 0, 0))
    row = lambda w: pl.BlockSpec((tm, w), lambda i: (i, 0))
    full = lambda a, b: pl.BlockSpec((a, b), lambda i: (0, 0))
    nch = D_S5 // LANES
    chunked = pl.BlockSpec((nch, tm, LANES), lambda i: (0, i, 0))
    return pl.pallas_call(
        _post_kernel,
        grid=(rows // tm,),
        in_specs=[row(D_MODEL), mod_spec, chunked, chunked, row(512), row(512), row(512), row(2048),
                  pl.BlockSpec((nch, 1, LANES), lambda i: (0, 0, 0)),
                  full(512, 512), full(1, 512), full(512, D_MODEL), full(512, D_MODEL),
                  full(D_MODEL, D_MODEL)],
        out_specs=row(D_MODEL),
        out_shape=jax.ShapeDtypeStruct((rows, D_MODEL), F32),
        compiler_params=_cparams(("arbitrary",)),
        name="post",
    )(x2d, gmod, yssm, u, zs5, o, znsa, merge, d, gluw, glub, ws5, wnsa, wo)


def _final_norm_kernel(x_ref, g_ref, o_ref):
    x = x_ref[...]
    o_ref[...] = x * lax.rsqrt(jnp.mean(x * x, axis=-1, keepdims=True) + RMS_EPS) * g_ref[...]


def _final_norm(x2d, g, tm):
    rows = x2d.shape[0]
    return pl.pallas_call(
        _final_norm_kernel,
        grid=(rows // tm,),
        in_specs=[pl.BlockSpec((tm, D_MODEL), lambda i: (i, 0)), pl.BlockSpec((1, D_MODEL), lambda i: (0, 0))],
        out_specs=pl.BlockSpec((tm, D_MODEL), lambda i: (i, 0)),
        out_shape=jax.ShapeDtypeStruct((rows, D_MODEL), F32),
        compiler_params=_cparams(("arbitrary",)),
        name="final_norm",
    )(x2d, g)


def _head_perm():
    idx = [HEAD_DIM * (GROUP_SIZE * g + r) + d
           for r in range(GROUP_SIZE) for g in range(N_KV) for d in range(HEAD_DIM)]
    return np.asarray(idx, np.int32)


def _rope_tables(pos, width=LANES):
    inv = ROPE_THETA ** (-jnp.arange(ROT_HALF, dtype=F32) / ROT_HALF)
    ang = pos.astype(F32)[:, None] * inv[None, :]
    cos, sin = jnp.cos(ang), jnp.sin(ang)
    n = pos.shape[0]
    one = jnp.ones((n, HEAD_DIM - 2 * ROT_HALF), F32)
    zero8 = jnp.zeros((n, ROT_HALF), F32)
    zero = jnp.zeros((n, HEAD_DIM - 2 * ROT_HALF), F32)
    c = jnp.concatenate([cos, cos, one], axis=1)
    sa = jnp.concatenate([-sin, zero8, zero], axis=1)
    sb = jnp.concatenate([zero8, sin, zero], axis=1)
    rep = width // HEAD_DIM
    return tuple(jnp.tile(t, (1, rep)) for t in (c, sa, sb))


def _blockdiag2(w):
    z = jnp.zeros_like(w)
    return jnp.concatenate([jnp.concatenate([w, z], -1), jnp.concatenate([z, w], -1)], -2)


def _s5_matrices(tabs, l, t_eff):
    ckr, cki, bkr, bki, m, pwr, pwi = [t[l] for t in tabs]
    g, c, L = S5_GROUPS, S5_GROUP, S5_CHUNK
    mk = m.reshape(g, L, c, c)
    s_idx = np.arange(L)[:, None]
    t_idx = np.arange(L)[None, :]
    lag = np.clip(t_idx - s_idx, 0, L - 1)
    kfull = mk[:, lag]
    kfull = jnp.where((t_idx >= s_idx)[None, :, :, None, None], kfull, 0.0)
    kmat = kfull.transpose(0, 1, 4, 2, 3).reshape(g, L * c, L * c).astype(BF16)
    ck = jnp.concatenate([ckr, cki], axis=-1)
    ymat = ck[:, c:, :].transpose(0, 2, 1).astype(BF16)
    bk = jnp.concatenate([bkr, bki], axis=-1).reshape(g, L, c, 2 * S5_STATE)
    sm = bk[:, t_eff - 1::-1] if t_eff == L else bk[:, np.arange(t_eff - 1, -1, -1)]
    if t_eff < L:
        sm = jnp.concatenate([sm, jnp.zeros((g, L - t_eff, c, 2 * S5_STATE), F32)], axis=1)
    smat = sm.reshape(g, L * c, 2 * S5_STATE).astype(BF16)
    ar = pwr[:, t_eff][:, None, :]
    ai = pwi[:, t_eff][:, None, :]
    a1 = jnp.concatenate([ar, ar], axis=-1)
    a2 = jnp.concatenate([-ai, ai], axis=-1)
    return kmat, smat, ymat, a1, a2


def _s5_row_weights(tabs):
    ckr, cki, bkr, bki, m, pwr, pwi = tabs
    c, L, st = S5_GROUP, S5_CHUNK, 2 * S5_STATE
    ng = LANES // c
    nch = S5_GROUPS // ng
    eye = jnp.eye(ng, dtype=BF16)
    mk = m.astype(BF16).reshape(DEPTH, nch, ng, L, c, c)
    w = jnp.einsum('dhgloi,gq->dhlgiqo', mk, eye).reshape(DEPTH, nch, L * LANES, LANES)
    w_shift = jnp.concatenate([jnp.zeros_like(w[:, :, :LANES]), w[:, :, :-LANES]], axis=2)
    w2 = jnp.concatenate([w_shift, w], axis=-1)
    bk = jnp.concatenate([bkr, bki], axis=-1).astype(BF16).reshape(DEPTH, nch, ng, L, c, st)
    sw = jnp.einsum('dhgkcp,gq->dhkgcqp', bk, eye).reshape(DEPTH, nch, L * LANES, ng * st)
    ck = jnp.concatenate([ckr, cki], axis=-1).astype(BF16).reshape(DEPTH, nch, ng, L + 1, c, st)[:, :, :, 1:]
    yw = jnp.einsum('dhgtcp,gq->dhgptqc', ck, eye).reshape(DEPTH, nch, ng * st, L * LANES)
    ar = pwr[:, :, L].reshape(DEPTH, nch, ng, 1, S5_STATE)
    ai = pwi[:, :, L].reshape(DEPTH, nch, ng, 1, S5_STATE)
    a1 = jnp.concatenate([ar, ar], axis=-1)
    a2 = jnp.concatenate([-ai, ai], axis=-1)
    return w2, sw, yw, a1, a2


def _all_layer_weights(w_in, cmp_pe, cmp_w1, cmp_w2, s5_glu_w, w_s5_out, w_nsa_out, w_o, perm):
    gate_w = jnp.pad(w_in[:, :, 2304:2328], ((0, 0), (0, 0), (0, LANES - 3 * N_HEADS)))
    wp = jnp.concatenate([w_in[:, :, :1024], w_in[:, :, 1024:1536][:, :, perm], w_in[:, :, 1536:2304],
                          w_in[:, :, 2328:2840][:, :, perm], w_in[:, :, 2840:], gate_w], axis=2).astype(BF16)
    pe2 = jnp.concatenate([cmp_pe, cmp_pe], axis=-1)[:, :, :, None, :]
    w1bd = _blockdiag2(cmp_w1.astype(BF16).reshape(DEPTH, 2, BLOCK, HEAD_DIM, HEAD_DIM))
    w1bd = w1bd.reshape(DEPTH, 2, BLOCK * KV_W, KV_W)
    w2bd = _blockdiag2(cmp_w2.astype(BF16))
    wt = w_in[:, :, 1536:2304].transpose(0, 2, 1).astype(BF16)
    return dict(wp=wp, wt=wt, pe2=pe2, w1bd=w1bd, w2bd=w2bd, gluw=s5_glu_w.astype(BF16),
                ws5=w_s5_out.astype(BF16), wnsa=w_nsa_out[:, perm, :].astype(BF16),
                wo=w_o.astype(BF16))


def _to_groups(u2d, nbt, nj):
    u5 = u2d.reshape(nbt, nj, S5_CHUNK, S5_GROUPS, S5_GROUP)
    return u5.transpose(3, 1, 0, 2, 4).reshape(S5_GROUPS, nj * nbt, S5_CHUNK * S5_GROUP)


def _from_groups(y_g, nbt, nj):
    y5 = y_g.reshape(S5_GROUPS, nj, nbt, S5_CHUNK, S5_GROUP)
    return y5.transpose(2, 1, 3, 0, 4).reshape(nbt * nj * S5_CHUNK, D_S5)


def _state_to_groups(h):
    return h.transpose(2, 0, 1, 3).reshape(S5_GROUPS, h.shape[0], 2 * S5_STATE)


def _state_from_groups(hg):
    g, b, _ = hg.shape
    return hg.reshape(g, b, 2, S5_STATE).transpose(1, 2, 0, 3)


def kernel(x_prompt, x_sample, c_prompt, c_sample, cache_kv, page_table, state_win, state_ssm, ada_w, ada_b, norm_g, w_in, s5_a_re, s5_a_im, s5_log_dt, s5_b_re, s5_b_im, s5_c_re, s5_c_im, s5_d, s5_glu_w, s5_glu_b, cmp_pe, cmp_w1, cmp_w2, w_s5_out, w_nsa_out, w_o, final_g):
    bp, tp, _ = x_prompt.shape
    bs, ts, _ = x_sample.shape
    n_pool, _, page = cache_kv.shape[:3]
    n_pages = page_table.shape[1]
    past = n_pages * page
    wbuf = state_win.shape[2]
    assert tp % S5_CHUNK == 0 and tp % 128 == 0 and ts <= 8 and bs % 8 == 0 and bp % 8 == 0

    perm = _head_perm()
    mod = _ada_mod(jnp.concatenate([c_prompt, c_sample], axis=0), ada_w, ada_b)
    s5tabs = _s5_setup(s5_a_re, s5_a_im, s5_log_dt, s5_b_re, s5_b_im, s5_c_re, s5_c_im)

    tabs_p = _rope_tables(jnp.arange(tp))
    rs = bs * ts
    tabs_s = _rope_tables(jnp.tile(past + jnp.arange(ts), bs))
    nb_p = tp // BLOCK
    nbp_p = (nb_p + LANES - 1) // LANES * LANES
    ctabs_p = _rope_tables(jnp.arange(nbp_p) * BLOCK + (BLOCK - 1))
    nb_s = (past + ts + BLOCK - 1) // BLOCK
    nbp_s = ((nb_s + 7) // 8 * 8 + LANES - 1) // LANES * LANES
    ctabs_s = _rope_tables(jnp.arange(nbp_s) * BLOCK + (BLOCK - 1))

    cache5 = cache_kv.transpose(0, 1, 3, 4, 5, 2).reshape(n_pool, DEPTH, 4, KV_W, page)
    state_wint = state_win.transpose(0, 1, 3, 4, 5, 2).reshape(bs, DEPTH, 2, KV_W, wbuf)
    tabs_pt = tuple(tb.T for tb in tabs_p)
    tm_p = 512 if tp % 512 == 0 else 256
    tq_p = 128
    nj_p = tp // S5_CHUNK
    consts = _attn_constants(tp, tq_p)
    keep = min(WINDOW, tp)

    xp = x_prompt.reshape(bp * tp, D_MODEL)
    xs = x_sample.reshape(rs, D_MODEL)
    kv_p, kv_s, win_p, win_s, ssm_p, ssm_s = [], [], [], [], [], []

    def pad_rows(a3):
        return jnp.pad(a3, ((0, 0), (0, 8 - ts), (0, 0)))

    def to_chunks(a2):
        return a2.reshape(a2.shape[0], D_S5 // LANES, LANES).transpose(1, 0, 2)

    def new_tiles(a3):
        a4 = a3.reshape(bs, ts, 2, KV_W).transpose(0, 2, 3, 1)
        return jnp.pad(a4, ((0, 0), (0, 0), (0, 0), (0, LANES - ts)))

    lw_all = _all_layer_weights(w_in, cmp_pe, cmp_w1, cmp_w2, s5_glu_w, w_s5_out, w_nsa_out, w_o, perm)
    s5w_all = _s5_row_weights(s5tabs)

    for l in range(DEPTH):
        lw = {name: val[l] for name, val in lw_all.items()}
        g_row = norm_g[l][None, :]
        d_row = s5_d[l].reshape(D_S5 // LANES, 1, LANES)
        glub = s5_glu_b[l][None, :]
        mp, ms = mod[l, :bp], mod[l, bp:]
        post_w = (d_row, lw['gluw'], glub, lw['ws5'], lw['wnsa'], lw['wo'])

        shift, scale, gmod = [mp[:, k * D_MODEL:(k + 1) * D_MODEL][:, None, :] for k in range(3)]
        u, zs5, q, kvt, wint, cmp, znsa, merge, gate = _inproj_prompt(
            xp, shift, scale, g_row, lw['wp'], lw['wt'], tabs_p, tabs_pt, tm=tm_p, nb_batch=bp, t=tp)
        kcvc = _cmp_prompt(cmp, bp, tp, nbp_p, lw['pe2'], lw['w1bd'], lw['w2bd'], ctabs_p)
        o = _nsa_prompt(q, kvt, wint, kcvc, gate, consts, bp, tp, tq_p)
        yssm, hl = _s5_rows(u, *[a[l] for a in s5w_all], bp, tp)
        xp = _post(xp, gmod, yssm, u, zs5, o, znsa, merge, *post_w,
                   tm=tm_p, tiles_per_b=tp // tm_p, per_row_mod=False)
        kv_p.append(kvt)
        win_p.append(wint[:, :, tp - keep:])
        ssm_p.append(_state_from_groups(hl.reshape(S5_GROUPS, bp, 2 * S5_STATE)))

        shift, scale, gmod = [jnp.repeat(ms[:, k * D_MODEL:(k + 1) * D_MODEL], ts, axis=0) for k in range(3)]
        u, zs5, q, kv, win, znsa, merge, gate = _inproj(
            xs, shift, scale, g_row, lw['wp'], tabs_s, tm=rs, tiles_per_b=1, per_row_mod=True)
        kv3 = kv.reshape(bs, ts, 4 * KV_W)
        kcvc = _cmp_sample(page_table, cache5, pad_rows(kv3[:, :, :2 * KV_W]), l, nbp_s,
                           lw['pe2'], lw['w1bd'], lw['w2bd'], ctabs_s, ts)
        o8, wst = _nsa_sample(page_table, cache5, pad_rows(q.reshape(bs, ts, D_NSA)),
                              new_tiles(kv3[:, :, 2 * KV_W:]), new_tiles(win.reshape(bs, ts, 2 * KV_W)),
                              state_wint, kcvc, pad_rows(gate.reshape(bs, ts, LANES)), l, ts)
        o = o8[:, :ts].reshape(rs, D_NSA)
        mats = _s5_matrices(s5tabs, l, ts)
        u_pad = jnp.pad(u.reshape(bs, ts, D_S5), ((0, 0), (0, S5_CHUNK - ts), (0, 0)))
        y_g, hl = _s5_chunk(_to_groups(u_pad.reshape(bs * S5_CHUNK, D_S5), bs, 1).astype(BF16),
                            _state_to_groups(state_ssm[:, l]), *mats, nbt=bs)
        yssm = _from_groups(y_g, bs, 1).reshape(bs, S5_CHUNK, D_S5)[:, :ts].reshape(rs, D_S5)
        xs = _post(xs, gmod, to_chunks(yssm), to_chunks(u), zs5, o, znsa, merge, *post_w,
                   tm=rs, tiles_per_b=1, per_row_mod=True)
        kv_s.append(kv.reshape(bs, ts, 4, N_KV, HEAD_DIM))
        win_s.append(wst)
        ssm_s.append(_state_from_groups(hl))

    def from_t(parts, n_streams):
        a = jnp.stack(parts, axis=1)
        nbt, keys = a.shape[0], a.shape[-1]
        return a.reshape(nbt, DEPTH, n_streams, N_KV, HEAD_DIM, keys).transpose(0, 1, 5, 2, 3, 4)

    fg = final_g[None, :]
    y_prompt = _final_norm(xp, fg, tm_p).reshape(bp, tp, D_MODEL)
    y_sample = _final_norm(xs, fg, rs).reshape(bs, ts, D_MODEL)
    return (y_prompt, y_sample, from_t(kv_p, 4), jnp.stack(kv_s, axis=1),
            from_t(win_p, 2), from_t(win_s, 2),
            jnp.stack(ssm_p, axis=1), jnp.stack(ssm_s, axis=1))
```

```python
import functools
import math

import numpy as np
import jax
import jax.numpy as jnp
from jax import lax
from jax.experimental import pallas as pl
from jax.experimental.pallas import tpu as pltpu

F32 = jnp.float32
BF16 = jnp.bfloat16

D_MODEL = 1024
DEPTH = 4
D_S5 = 512
S5_GROUP = 16
S5_GROUPS = 32
S5_STATE = 64
D_NSA = 512
HEAD_DIM = 64
N_HEADS = 8
N_KV = 2
GROUP_SIZE = 4
BLOCK = 64
N_SELECT = 16
WINDOW = 512
ROT_HALF = 8
ROPE_THETA = 500000.0
RMS_EPS = 1e-6
NEG_INF = -1e30
FORCED_SCORE = 1e4

LANES = 128
S5_CHUNK = 16
KV_W = N_KV * HEAD_DIM
N_PROJ = 4992
CMP_PITCH = 72
VMEM_LIMIT = 56 * 1024 * 1024


def _dot(a, b):
    return jnp.dot(a, b, preferred_element_type=F32)


def _dot_nt(a, b):
    return lax.dot_general(a, b, (((1,), (1,)), ((), ())), preferred_element_type=F32)


def _silu(x):
    return x * jax.nn.sigmoid(x)


def _rope128(x, c, sa, sb):
    return x * c + pltpu.roll(x, LANES - ROT_HALF, 1) * sa + pltpu.roll(x, ROT_HALF, 1) * sb


def _masked_exp(s, mask):
    s = jnp.where(mask, s, NEG_INF)
    m = jnp.max(s, axis=-1, keepdims=True)
    p = jnp.where(mask, jnp.exp(s - m), 0.0)
    return p, jnp.sum(p, axis=-1, keepdims=True)


def _safe_inv(l):
    return jnp.where(l > 0.0, 1.0 / l, 0.0)


def _cparams(sem):
    return pltpu.CompilerParams(dimension_semantics=sem, vmem_limit_bytes=VMEM_LIMIT)


def _ada_kernel(c_ref, w_ref, b_ref, o_ref):
    c = c_ref[...]
    o_ref[...] = _dot(_silu(c).astype(BF16), w_ref[...].astype(BF16)) + b_ref[...]


def _ada_mod(c_all, ada_w, ada_b):
    nc = c_all.shape[0]
    tn = 1024
    return pl.pallas_call(
        _ada_kernel,
        grid=(DEPTH, 3 * D_MODEL // tn),
        in_specs=[pl.BlockSpec((nc, D_MODEL), lambda l, n: (0, 0)),
                  pl.BlockSpec((None, D_MODEL, tn), lambda l, n: (l, 0, n)),
                  pl.BlockSpec((None, 1, tn), lambda l, n: (l, 0, n))],
        out_specs=pl.BlockSpec((None, nc, tn), lambda l, n: (l, 0, n)),
        out_shape=jax.ShapeDtypeStruct((DEPTH, nc, 3 * D_MODEL), F32),
        compiler_params=_cparams(("arbitrary", "arbitrary")),
        name="ada_mod",
    )(c_all, ada_w, ada_b.reshape(DEPTH, 1, 3 * D_MODEL))


def _inproj_kernel(x_ref, shift_ref, scale_ref, g_ref, w_ref, cos_ref, sa_ref, sb_ref,
                   u_ref, zs5_ref, q_ref, kv_ref, win_ref, znsa_ref, merge_ref, gate_ref):
    x = x_ref[...]
    h = x * lax.rsqrt(jnp.mean(x * x, axis=-1, keepdims=True) + RMS_EPS) * g_ref[...]
    h = h * (1.0 + scale_ref[...]) + shift_ref[...]
    hb = h.astype(BF16)

    def mm(lo, hi):
        return _dot(hb, w_ref[:, lo:hi])

    c, sa, sb = cos_ref[...], sa_ref[...], sb_ref[...]
    u_ref[...] = mm(0, 512)
    zs5_ref[...] = mm(512, 1024).astype(BF16)
    for r in range(GROUP_SIZE):
        lo = 1024 + r * LANES
        q = _rope128(mm(lo, lo + LANES), c, sa, sb) * (HEAD_DIM ** -0.5)
        q_ref[:, r * LANES:(r + 1) * LANES] = q.astype(BF16)
    kv_ref[:, 0:256] = mm(1536, 1792)
    kv_ref[:, 256:384] = _rope128(mm(1792, 1920), c, sa, sb)
    kv_ref[:, 384:512] = mm(1920, 2048)
    win_ref[:, 0:128] = _rope128(mm(2048, 2176), c, sa, sb)
    win_ref[:, 128:256] = mm(2176, 2304)
    znsa_ref[...] = mm(2304, 2816).astype(BF16)
    merge_ref[...] = mm(2816, 4864).astype(BF16)
    gate_ref[...] = mm(4864, 4992)


def _inproj(x2d, shift, scale, g, wp, tabs, *, tm, tiles_per_b, per_row_mod):
    rows = x2d.shape[0]
    nt = rows // tm
    n_tab = tabs[0].shape[0] // tm
    if per_row_mod:
        mod_spec = pl.BlockSpec((tm, D_MODEL), lambda i: (i, 0))
    else:
        mod_spec = pl.BlockSpec((None, 1, D_MODEL), lambda i: (i // tiles_per_b, 0, 0))
    tab_spec = pl.BlockSpec((tm, LANES), lambda i: (i % n_tab, 0))

    def row_spec(w):
        return pl.BlockSpec((tm, w), lambda i: (i, 0))

    widths = (512, 512, 512, 512, 256, 512, 2048, 128)
    dtypes = (F32, BF16, BF16, F32, F32, BF16, BF16, F32)
    return pl.pallas_call(
        _inproj_kernel,
        grid=(nt,),
        in_specs=[row_spec(D_MODEL), mod_spec, mod_spec,
                  pl.BlockSpec((1, D_MODEL), lambda i: (0, 0)),
                  pl.BlockSpec((D_MODEL, N_PROJ), lambda i: (0, 0)),
                  tab_spec, tab_spec, tab_spec],
        out_specs=[row_spec(w) for w in widths],
        out_shape=[jax.ShapeDtypeStruct((rows, w), d) for w, d in zip(widths, dtypes)],
        compiler_params=_cparams(("arbitrary",)),
        name="inproj",
    )(x2d, shift, scale, g, wp, *tabs)


def _rope128_t(x, c, sa, sb):
    return x * c + pltpu.roll(x, KV_W - ROT_HALF, 0) * sa + pltpu.roll(x, ROT_HALF, 0) * sb


def _inproj_prompt_kernel(x_ref, shift_ref, scale_ref, g_ref, w_ref, wt_ref, cos_ref, sa_ref, sb_ref,
                          cost_ref, sat_ref, sbt_ref,
                          u_ref, zs5_ref, q_ref, kvt_ref, wint_ref, cmp_ref, znsa_ref, merge_ref, gate_ref):
    x = x_ref[...]
    h = x * lax.rsqrt(jnp.mean(x * x, axis=-1, keepdims=True) + RMS_EPS) * g_ref[...]
    h = h * (1.0 + scale_ref[...]) + shift_ref[...]
    hb = h.astype(BF16)

    def mm(lo, hi):
        return _dot(hb, w_ref[:, lo:hi])

    c, sa, sb = cos_ref[...], sa_ref[...], sb_ref[...]
    for pair in range(D_S5 // (2 * LANES)):
        up = mm(pair * 2 * LANES, (pair + 1) * 2 * LANES)
        u_ref[2 * pair] = up[:, :LANES]
        u_ref[2 * pair + 1] = up[:, LANES:]
    zs5_ref[...] = mm(512, 1024).astype(BF16)
    for pair in range(GROUP_SIZE // 2):
        lo = 1024 + pair * 2 * LANES
        qp = mm(lo, lo + 2 * LANES)
        for k in range(2):
            q = _rope128(qp[:, k * LANES:(k + 1) * LANES], c, sa, sb) * (HEAD_DIM ** -0.5)
            q_ref[:, (2 * pair + k) * LANES:(2 * pair + k + 1) * LANES] = q.astype(BF16)
    cmp_ref[...] = mm(1536, 1792)
    znsa_ref[...] = mm(2304, 2816).astype(BF16)
    merge_ref[...] = mm(2816, 4864).astype(BF16)
    gate_ref[...] = mm(4864, 4992)

    ct, sat, sbt = cost_ref[...], sat_ref[...], sbt_ref[...]
    kvxt = _dot_nt(wt_ref[...], hb)

    def mmt(s):
        return kvxt[s * KV_W:(s + 1) * KV_W, :]

    kvt_ref[0:KV_W, :] = mmt(0)
    kvt_ref[KV_W:2 * KV_W, :] = mmt(1)
    kvt_ref[2 * KV_W:3 * KV_W, :] = _rope128_t(mmt(2), ct, sat, sbt)
    kvt_ref[3 * KV_W:4 * KV_W, :] = mmt(3)
    wint_ref[0:KV_W, :] = _rope128_t(mmt(4), ct, sat, sbt)
    wint_ref[KV_W:2 * KV_W, :] = mmt(5)


def _inproj_prompt(x2d, shift, scale, g, wp, wt, tabs, tabs_t, *, tm, nb_batch, t):
    rows = x2d.shape[0]
    tpb = t // tm
    mod_spec = pl.BlockSpec((None, 1, D_MODEL), lambda i: (i // tpb, 0, 0))
    tab_spec = pl.BlockSpec((tm, LANES), lambda i: (i % tpb, 0))
    tabt_spec = pl.BlockSpec((KV_W, tm), lambda i: (0, i % tpb))
    row = lambda w: pl.BlockSpec((tm, w), lambda i: (i, 0))
    tr = lambda h: pl.BlockSpec((None, h, tm), lambda i: (i // tpb, 0, i % tpb))
    u_spec = pl.BlockSpec((D_S5 // LANES, tm, LANES), lambda i: (0, i, 0))
    outs = [((D_S5 // LANES, rows, LANES), F32, u_spec), ((rows, 512), BF16, row(512)),
            ((rows, 512), BF16, row(512)),
            ((nb_batch, 4 * KV_W, t), F32, tr(4 * KV_W)), ((nb_batch, 2 * KV_W, t), F32, tr(2 * KV_W)),
            ((rows, 256), F32, row(256)), ((rows, 512), BF16, row(512)), ((rows, 2048), BF16, row(2048)),
            ((rows, 128), F32, row(128))]
    return pl.pallas_call(
        _inproj_prompt_kernel,
        grid=(rows // tm,),
        in_specs=[row(D_MODEL), mod_spec, mod_spec,
                  pl.BlockSpec((1, D_MODEL), lambda i: (0, 0)),
                  pl.BlockSpec((D_MODEL, N_PROJ), lambda i: (0, 0)),
                  pl.BlockSpec((6 * KV_W, D_MODEL), lambda i: (0, 0)),
                  tab_spec, tab_spec, tab_spec, tabt_spec, tabt_spec, tabt_spec],
        out_specs=[o[2] for o in outs],
        out_shape=[jax.ShapeDtypeStruct(o[0], o[1]) for o in outs],
        compiler_params=_cparams(("arbitrary",)),
        name="inproj_prompt",
    )(x2d, shift, scale, g, wp, wt, *tabs, *tabs_t)


def _compress(load_p, nbk, pe_ref, w1_ref, w2_ref, c, sa, sb, xcat_sc):
    rows = xcat_sc.shape[0]
    outs = []
    for s in range(2):
        for p in range(BLOCK):
            xp = load_p(s, p) + pe_ref[s, p]
            if rows > nbk:
                xp = jnp.concatenate([xp, jnp.zeros((rows - nbk, KV_W), F32)], axis=0)
            xcat_sc[:, p * KV_W:(p + 1) * KV_W] = xp.astype(BF16)
        acc = _dot(xcat_sc[...], w1_ref[s])[0:nbk]
        outs.append(_dot(_silu(acc).astype(BF16), w2_ref[s]))
    return jnp.concatenate([_rope128(outs[0], c, sa, sb), outs[1]], axis=1)


def _cmp_prompt_kernel(k_ref, v_ref, pe_ref, w1_ref, w2_ref, cos_ref, sa_ref, sb_ref, o_ref, xcat_sc,
                       *, nbk):
    def load_p(s, p):
        return (k_ref, v_ref)[s][pl.ds(p, nbk, stride=BLOCK), :]

    res = _compress(load_p, nbk, pe_ref, w1_ref, w2_ref,
                    cos_ref[0:nbk, :], sa_ref[0:nbk, :], sb_ref[0:nbk, :], xcat_sc)
    o_ref[...] = jnp.zeros(o_ref.shape, F32)
    o_ref[0:nbk, :] = res


def _cmp_prompt(kv2d, nb_batch, t, nbp, pe2, w1bd, w2bd, ctabs):
    nbk = t // BLOCK
    full = lambda shape: pl.BlockSpec(shape, lambda b: (0,) * len(shape))
    return pl.pallas_call(
        functools.partial(_cmp_prompt_kernel, nbk=nbk),
        grid=(nb_batch,),
        in_specs=[pl.BlockSpec((t, KV_W), lambda b: (b, 0)), pl.BlockSpec((t, KV_W), lambda b: (b, 1)),
                  full((2, BLOCK, 1, KV_W)), full((2, BLOCK * KV_W, KV_W)),
                  full((2, KV_W, KV_W)), full((nbp, LANES)), full((nbp, LANES)), full((nbp, LANES))],
        out_specs=pl.BlockSpec((None, nbp, 2 * KV_W), lambda b: (b, 0, 0)),
        out_shape=jax.ShapeDtypeStruct((nb_batch, nbp, 2 * KV_W), F32),
        scratch_shapes=[pltpu.VMEM(((nbk + 15) // 16 * 16, BLOCK * KV_W), BF16)],
        compiler_params=_cparams(("arbitrary",)),
        name="cmp_prompt",
    )(kv2d, kv2d, pe2, w1bd, w2bd, *ctabs)


def _stack_queries(qt, tq):
    lane = lax.broadcasted_iota(jnp.int32, (tq, LANES), 1)
    qt = qt.astype(F32)
    blocks = []
    for g in range(N_KV):
        keep = (lane < HEAD_DIM) if g == 0 else (lane >= HEAD_DIM)
        for r in range(GROUP_SIZE):
            blocks.append(jnp.where(keep, qt[:, r * LANES:(r + 1) * LANES], 0.0))
    return jnp.concatenate(blocks, axis=0).astype(BF16)


def _select_blocks(imp, qblk, nb, n_sel):
    nq, nbp = imp.shape
    n = lax.broadcasted_iota(jnp.int32, (nq, nbp), 1)
    forced = (n == 0) | (n == qblk) | (n == qblk - 1)
    imp = jnp.where(forced, FORCED_SCORE, imp)
    imp = jnp.where(n > qblk, -1.0, imp)
    imp = jnp.where(n >= nb, -2.0, imp)
    rank = jnp.zeros((nq, nbp), F32)
    for m in range(nb):
        col = imp[:, m:m + 1]
        beats = (col > imp) | ((col == imp) & (n > m))
        rank = rank + jnp.where(beats, 1.0, 0.0)
    return jnp.where((rank < n_sel) & (imp > -0.5), 1.0, 0.0)


def _compressed_branch(qs, kcvc, tq_col, tq, nb):
    nbp = kcvc.shape[0]
    kc = kcvc[:, :KV_W].astype(BF16)
    vc = kcvc[:, KV_W:].astype(BF16)
    s = _dot_nt(qs, kc)
    n = lax.broadcasted_iota(jnp.int32, (1, nbp), 1)
    mask = (n * BLOCK + (BLOCK - 1) <= tq_col) & (n < nb)
    p, l = _masked_exp(s, mask)
    p = p * _safe_inv(l)
    o_c = _dot(p.astype(BF16), vc)
    imps = []
    for g in range(N_KV):
        acc = p[(g * GROUP_SIZE) * tq:(g * GROUP_SIZE + 1) * tq]
        for r in range(1, GROUP_SIZE):
            acc = acc + p[(g * GROUP_SIZE + r) * tq:(g * GROUP_SIZE + r + 1) * tq]
        imps.append(acc)
    return o_c, imps


def _block_mask(selstack, k0, tk):
    nbp = selstack.shape[1]
    kblk = (k0 + lax.broadcasted_iota(jnp.int32, (nbp, tk), 1)) // BLOCK
    e = jnp.where(kblk == lax.broadcasted_iota(jnp.int32, (nbp, tk), 0), 1.0, 0.0).astype(BF16)
    return _dot(selstack, e) > 0.5


def _gated_heads(gt, branches, tq):
    lane = lax.broadcasted_iota(jnp.int32, (tq, LANES), 1)
    chunks = []
    for r in range(GROUP_SIZE):
        halves = []
        for g in range(N_KV):
            h = g * GROUP_SIZE + r
            acc = None
            for br, ob in enumerate(branches):
                term = gt[:, br * N_HEADS + h:br * N_HEADS + h + 1] * ob[h * tq:(h + 1) * tq]
                acc = term if acc is None else acc + term
            halves.append(acc)
        chunks.append(jnp.where(lane < HEAD_DIM, halves[0], halves[1]))
    return chunks


def _select_blocks_t(imp, qblk_row, nb, n_sel):
    nq, nbp = imp.shape
    nbr = (nb + 7) // 8 * 8
    x = imp.T[0:nbr, :]
    n = lax.broadcasted_iota(jnp.int32, (nbr, nq), 0)
    forced = (n == 0) | (n == qblk_row) | (n == qblk_row - 1)
    x = jnp.where(forced, FORCED_SCORE, x)
    x = jnp.where(n > qblk_row, -1.0, x)
    x = jnp.where(n >= nb, -2.0, x)
    rank = jnp.zeros((nbr, nq), F32)
    for m in range(nb):
        row = x[m:m + 1, :]
        beats = (row > x) | ((row == x) & (n > m))
        rank = rank + jnp.where(beats, 1.0, 0.0)
    sel = jnp.where((rank < n_sel) & (x > -0.5), 1.0, 0.0)
    if nbr < nbp:
        sel = jnp.concatenate([sel, jnp.zeros((nbp - nbr, nq), F32)], axis=0)
    return sel.T


def _nsa_prompt_kernel(q_ref, kvt_ref, wint_ref, kcvc_ref, gate_ref, e_ref, cbias_ref, wbias_ref,
                       o_ref, kaug_sc, vsel_sc, waug_sc, vwin_sc, s_sc, m_sc, l_sc, acc_sc, sel_sc,
                       *, tq, t, nb, n_sel):
    i = pl.program_id(1)
    q0 = pl.multiple_of(i * tq, tq)
    rows = N_HEADS * tq
    wk = WINDOW + tq
    tk = 2 * LANES
    neg = jnp.asarray(NEG_INF, F32)

    @pl.when(i == 0)
    def _():
        kaug_sc[0:KV_W, :] = kvt_ref[0:KV_W, :].astype(BF16)
        kaug_sc[KV_W:2 * KV_W, :] = e_ref[...]
        vsel_sc[...] = kvt_ref[KV_W:2 * KV_W, :].astype(BF16)
        waug_sc[...] = jnp.zeros(waug_sc.shape, BF16)
        waug_sc[0:KV_W, WINDOW:] = wint_ref[0:KV_W, :].astype(BF16)
        waug_sc[KV_W:KV_W + 16, 0:WINDOW] = jnp.ones((16, WINDOW), BF16)
        vwin_sc[:, 0:WINDOW] = jnp.zeros((KV_W, WINDOW), BF16)
        vwin_sc[:, WINDOW:] = wint_ref[KV_W:2 * KV_W, :].astype(BF16)

    qs = _stack_queries(q_ref[...], tq)
    t_loc = lax.broadcasted_iota(jnp.int32, (rows, 1), 0) % tq
    tq_col = q0 + t_loc

    o_c, imps = _compressed_branch(qs, kcvc_ref[...], tq_col, tq, nb)

    sel_sc[...] = jnp.ones(sel_sc.shape, F32)

    @pl.when(q0 + tq > n_sel * BLOCK)
    def _():
        qblk_row = (q0 + lax.broadcasted_iota(jnp.int32, (1, tq), 1)) // BLOCK
        for g in range(N_KV):
            sel_sc[g] = _select_blocks_t(imps[g], qblk_row, nb, n_sel)

    selb = [((sel_sc[g] - 1.0) * (-NEG_INF)).astype(BF16) for g in range(N_KV)]
    selstack = jnp.concatenate([selb[g] for g in range(N_KV) for _ in range(GROUP_SIZE)], axis=0)
    qaug = jnp.concatenate([qs, selstack], axis=1)

    lane = lax.broadcasted_iota(jnp.int32, (rows, LANES), 1)
    padcol = jnp.where(lane == 0, neg, 0.0).astype(BF16)
    qaug_w = jnp.concatenate([qs, padcol], axis=1)
    n_full = q0 // tk
    par = (q0 % tk) // tq
    m_sc[...] = jnp.full(m_sc.shape, neg, F32)

    def score_tile(k0, w, bias):
        s = _dot(qaug, kaug_sc[:, pl.ds(k0, w)])
        if bias is not None:
            s = s + bias
        s_sc[:, pl.ds(k0, w)] = s
        mx = s[:, :LANES]
        for c in range(1, w // LANES):
            mx = jnp.maximum(mx, s[:, c * LANES:(c + 1) * LANES])
        m_sc[...] = jnp.maximum(m_sc[...], mx)

    def pass1(j, carry):
        score_tile(pl.multiple_of(j * 2 * tk, 2 * tk), 2 * tk, None)
        return carry

    lax.fori_loop(0, n_full // 2, pass1, 0)

    @pl.when(n_full % 2 == 1)
    def _():
        score_tile(pl.multiple_of((n_full - 1) * tk, tk), tk, None)

    score_tile(pl.multiple_of(n_full * tk, tk), tk, cbias_ref[par])
    m_b = jnp.broadcast_to(jnp.max(m_sc[...], axis=-1, keepdims=True), (rows, LANES))

    l_sc[...] = jnp.zeros(l_sc.shape, F32)
    acc_sc[...] = jnp.zeros(acc_sc.shape, F32)

    def pv_tile(k0, w, m_b):
        s = s_sc[:, pl.ds(k0, w)]
        ps = [jnp.exp(s[:, c * LANES:(c + 1) * LANES] - m_b) for c in range(w // LANES)]
        tot = ps[0]
        for pc in ps[1:]:
            tot = tot + pc
        l_sc[...] = l_sc[...] + tot
        p = jnp.concatenate(ps, axis=1).astype(BF16)
        acc_sc[...] = acc_sc[...] + _dot_nt(p, vsel_sc[:, pl.ds(k0, w)])

    def pass2(j, carry):
        pv_tile(pl.multiple_of(j * 2 * tk, 2 * tk), 2 * tk, m_b)
        return carry

    lax.fori_loop(0, (n_full + 1) // 2, pass2, 0)

    @pl.when(n_full % 2 == 0)
    def _():
        pv_tile(pl.multiple_of(n_full * tk, tk), tk, m_b)

    o_s = acc_sc[...] / jnp.sum(l_sc[...], axis=-1, keepdims=True)

    s = _dot(qaug_w, waug_sc[:, pl.ds(q0, wk)]) + wbias_ref[...]
    p = jnp.exp(s - jnp.max(s, axis=-1, keepdims=True))
    o_w = _dot_nt(p.astype(BF16), vwin_sc[:, pl.ds(q0, wk)]) / jnp.sum(p, axis=-1, keepdims=True)

    for r, ch in enumerate(_gated_heads(jax.nn.sigmoid(gate_ref[...]), (o_c, o_s, o_w), tq)):
        o_ref[:, r * LANES:(r + 1) * LANES] = ch


def _attn_constants(t, tq):
    rows = N_HEADS * tq
    t_loc = np.arange(rows)[:, None] % tq
    k = np.arange(2 * LANES)[None, :]
    cb = np.stack([np.where(k - par * tq <= t_loc, 0.0, NEG_INF) for par in range(2 * LANES // tq)])
    kr = np.arange(WINDOW + tq)[None, :]
    wb = np.where((kr - WINDOW <= t_loc) & (kr > t_loc), 0.0, NEG_INF)
    e = (np.arange(t)[None, :] // BLOCK == np.arange(LANES)[:, None]).astype(np.float32)
    return jnp.asarray(e, BF16), jnp.asarray(cb, F32), jnp.asarray(wb, F32)


def _nsa_prompt(q2d, kvt, wint, kcvc, gate2d, consts, nb_batch, t, tq):
    nb = t // BLOCK
    nq = t // tq
    nbp = kcvc.shape[1]
    rows = N_HEADS * tq
    wk = WINDOW + tq
    assert nbp == LANES and t % (2 * LANES) == 0
    e, cb, wb = consts
    kern = functools.partial(_nsa_prompt_kernel, tq=tq, t=t, nb=nb, n_sel=min(N_SELECT, nb))
    return pl.pallas_call(
        kern,
        grid=(nb_batch, nq),
        in_specs=[pl.BlockSpec((tq, D_NSA), lambda b, i: (b * nq + i, 0)),
                  pl.BlockSpec((None, 2 * KV_W, t), lambda b, i: (b, 1, 0)),
                  pl.BlockSpec((None, 2 * KV_W, t), lambda b, i: (b, 0, 0)),
                  pl.BlockSpec((None, nbp, 2 * KV_W), lambda b, i: (b, 0, 0)),
                  pl.BlockSpec((tq, LANES), lambda b, i: (b * nq + i, 0)),
                  pl.BlockSpec((LANES, t), lambda b, i: (0, 0)),
                  pl.BlockSpec((2 * LANES // tq, rows, 2 * LANES), lambda b, i: (0, 0, 0)),
                  pl.BlockSpec((rows, wk), lambda b, i: (0, 0))],
        out_specs=pl.BlockSpec((tq, D_NSA), lambda b, i: (b * nq + i, 0)),
        out_shape=jax.ShapeDtypeStruct((nb_batch * t, D_NSA), F32),
        scratch_shapes=[pltpu.VMEM((2 * KV_W, t), BF16), pltpu.VMEM((KV_W, t), BF16),
                        pltpu.VMEM((2 * KV_W, WINDOW + t), BF16), pltpu.VMEM((KV_W, WINDOW + t), BF16),
                        pltpu.VMEM((rows, t), F32), pltpu.VMEM((rows, LANES), F32),
                        pltpu.VMEM((rows, LANES), F32), pltpu.VMEM((rows, KV_W), F32),
                        pltpu.VMEM((N_KV, tq, LANES), F32)],
        compiler_params=_cparams(("arbitrary", "arbitrary")),
        name="nsa_prompt",
    )(q2d, kvt, wint, kcvc, gate2d, e, cb, wb)


def _page_copy(cache_ref, slabs, sem_ref, pt_ref, layer, b, p, slot, s0, page, s):
    return pltpu.make_async_copy(
        cache_ref.at[pt_ref[b, p], layer, s0 + s],
        slabs[s].at[slot, :, pl.ds(p * page, page)],
        sem_ref.at[slot])


def _gather_pages(cache_ref, slabs, sem_ref, pt_ref, layer, s0, n_pages, page):
    b = pl.program_id(0)
    nb_batch = pl.num_programs(0)
    slot = b % 2

    def copies(bb, sl):
        return [_page_copy(cache_ref, slabs, sem_ref, pt_ref, layer, bb, p, sl, s0, page, s)
                for p in range(n_pages) for s in range(2)]

    @pl.when(b == 0)
    def _():
        for cp in copies(0, 0):
            cp.start()

    @pl.when(b + 1 < nb_batch)
    def _():
        for cp in copies(b + 1, 1 - slot):
            cp.start()

    for cp in copies(b, slot):
        cp.wait()
    return slot


def _cmp_sample_kernel(pt_ref, cache_ref, new_ref, pe_ref, w1_ref, w2_ref, cos_ref, sa_ref, sb_ref,
                       o_ref, stagek_ref, stagev_ref, rowk_ref, rowv_ref, xcat_sc, sem_ref,
                       *, layer, n_pages, page, nbk, nbs):
    stages = (stagek_ref, stagev_ref)
    rowm = (rowk_ref, rowv_ref)
    i = pl.program_id(0)
    n_steps = pl.num_programs(0)

    def copies(step):
        return [_page_copy(cache_ref, stages, sem_ref, pt_ref, layer, step * nbs + bl, p, bl, 0, page, s)
                for bl in range(nbs) for p in range(n_pages) for s in range(2)]

    @pl.when(i == 0)
    def _():
        for cp in copies(0):
            cp.start()

    for cp in copies(i):
        cp.wait()

    bpp = page // BLOCK
    nb_past = n_pages * bpp
    unroll = next(k for k in (8, 4, 2, 1) if n_pages % k == 0)
    rows_b = nbk * CMP_PITCH
    for s in range(2):
        for bl in range(nbs):
            def xpose(it, carry):
                for k in range(unroll):
                    p = it * unroll + k
                    c0 = pl.multiple_of(p * page, page)
                    tile = stages[s][bl, :, pl.ds(c0, page)].T
                    for h in range(bpp):
                        r0 = pl.multiple_of(bl * rows_b + (p * bpp + h) * CMP_PITCH, 8)
                        rowm[s][pl.ds(r0, BLOCK), :] = tile[h * BLOCK:(h + 1) * BLOCK, :]
                return carry

            lax.fori_loop(0, n_pages // unroll, xpose, 0)
            tail0 = bl * rows_b + nb_past * CMP_PITCH
            tail = (bl + 1) * rows_b - tail0
            rowm[s][pl.ds(tail0, tail), :] = jnp.zeros((tail, KV_W), F32)
            rowm[s][pl.ds(tail0, 8), :] = new_ref[bl, :, s * KV_W:(s + 1) * KV_W]

    @pl.when(i + 1 < n_steps)
    def _():
        for cp in copies(i + 1):
            cp.start()

    def load_p(s, p):
        return rowm[s][pl.ds(p, nbs * nbk, stride=CMP_PITCH), :]

    tabs = [jnp.concatenate([t_ref[0:nbk, :]] * nbs, axis=0) for t_ref in (cos_ref, sa_ref, sb_ref)]
    res = _compress(load_p, nbs * nbk, pe_ref, w1_ref, w2_ref, *tabs, xcat_sc)
    o_ref[...] = jnp.zeros(o_ref.shape, F32)
    for bl in range(nbs):
        o_ref[bl, 0:nbk, :] = res[bl * nbk:(bl + 1) * nbk]


def _cmp_sample(page_table, cache5, newcmp, layer, nbp, pe2, w1bd, w2bd, ctabs, t_new):
    nb_batch, n_pages = page_table.shape
    page = cache5.shape[4]
    past = n_pages * page
    nbk = ((past + t_new + BLOCK - 1) // BLOCK + 7) // 8 * 8
    nbs = 2
    slab_rows = nbs * nbk * CMP_PITCH
    assert page % BLOCK == 0 and t_new <= 8 and nb_batch % nbs == 0
    full = lambda shape: pl.BlockSpec(shape, lambda b, pt: (0,) * len(shape))
    kern = functools.partial(_cmp_sample_kernel, layer=layer, n_pages=n_pages, page=page, nbk=nbk,
                             nbs=nbs)
    return pl.pallas_call(
        kern,
        grid_spec=pltpu.PrefetchScalarGridSpec(
            num_scalar_prefetch=1,
            grid=(nb_batch // nbs,),
            in_specs=[pl.BlockSpec(memory_space=pl.ANY),
                      pl.BlockSpec((nbs, 8, 2 * KV_W), lambda b, pt: (b, 0, 0)),
                      full((2, BLOCK, 1, KV_W)), full((2, BLOCK * KV_W, KV_W)),
                      full((2, KV_W, KV_W)), full((nbp, LANES)), full((nbp, LANES)),
                      full((nbp, LANES))],
            out_specs=pl.BlockSpec((nbs, nbp, 2 * KV_W), lambda b, pt: (b, 0, 0)),
            scratch_shapes=[pltpu.VMEM((nbs, KV_W, past), F32), pltpu.VMEM((nbs, KV_W, past), F32),
                            pltpu.VMEM((slab_rows, KV_W), F32), pltpu.VMEM((slab_rows, KV_W), F32),
                            pltpu.VMEM(((nbs * nbk + 15) // 16 * 16, BLOCK * KV_W), BF16),
                            pltpu.SemaphoreType.DMA((nbs,))]),
        out_shape=jax.ShapeDtypeStruct((nb_batch, nbp, 2 * KV_W), F32),
        compiler_params=_cparams(("arbitrary",)),
        name="cmp_sample",
    )(page_table, cache5, newcmp, pe2, w1bd, w2bd, *ctabs)


def _nsa_sample_kernel(pt_ref, cache_ref, q_ref, newt_ref, wnewt_ref, swint_ref, kcvc_ref, gate_ref,
                       o_ref, wout_ref, slabk_ref, slabv_ref, wslab_ref, sem_ref,
                       *, layer, n_pages, page, t_new, nb, n_sel):
    tq = 8
    past = n_pages * page
    rows = N_HEADS * tq
    slabs = (slabk_ref, slabv_ref)
    slot = _gather_pages(cache_ref, slabs, sem_ref, pt_ref, layer, 2, n_pages, page)
    nk = slabk_ref.shape[2]
    for s in range(2):
        slabs[s][slot, :, pl.ds(past, LANES)] = newt_ref[s]

    qs = _stack_queries(q_ref[...], tq)
    t_loc = lax.broadcasted_iota(jnp.int32, (rows, 1), 0) % tq
    tq_col = past + t_loc

    o_c, imps = _compressed_branch(qs, kcvc_ref[...], tq_col, tq, nb)
    qblk = (past + lax.broadcasted_iota(jnp.int32, (tq, 1), 0)) // BLOCK
    sels = [_select_blocks(imps[g], qblk, nb, n_sel).astype(BF16) for g in range(N_KV)]
    selstack = jnp.concatenate([sels[g] for g in range(N_KV) for _ in range(GROUP_SIZE)], axis=0)

    s = _dot(qs, slabk_ref[slot].astype(BF16))
    kpos = lax.broadcasted_iota(jnp.int32, (1, nk), 1)
    p, l = _masked_exp(s, _block_mask(selstack, 0, nk) & (kpos <= tq_col))
    o_s = _dot_nt(p.astype(BF16), slabv_ref[slot].astype(BF16)) * _safe_inv(l)

    wbuf = swint_ref.shape[2]
    wk = wslab_ref.shape[2]
    wslab_ref[:, :, 0:wbuf] = swint_ref[...]
    wslab_ref[:, :, wbuf:wk] = wnewt_ref[...]
    s = _dot(qs, wslab_ref[0].astype(BF16))
    kpos = past - wbuf + lax.broadcasted_iota(jnp.int32, (1, wk), 1)
    p, l = _masked_exp(s, (kpos <= tq_col) & (kpos > tq_col - WINDOW) & (kpos >= 0))
    o_w = _dot_nt(p.astype(BF16), wslab_ref[1].astype(BF16)) * _safe_inv(l)
    wout_ref[...] = wslab_ref[:, :, t_new:t_new + wbuf]

    for r, ch in enumerate(_gated_heads(jax.nn.sigmoid(gate_ref[...]), (o_c, o_s, o_w), tq)):
        o_ref[:, r * LANES:(r + 1) * LANES] = ch


def _nsa_sample(page_table, cache5, q3, newselt, wnewt, state_wint, kcvc, gate3, layer, t_new):
    nb_batch, n_pages = page_table.shape
    page = cache5.shape[4]
    past = n_pages * page
    nb = (past + t_new + BLOCK - 1) // BLOCK
    nk = past + LANES
    wbuf = state_wint.shape[4]
    wk = wbuf + LANES
    nbp = kcvc.shape[1]
    kern = functools.partial(_nsa_sample_kernel, layer=layer, n_pages=n_pages, page=page,
                             t_new=t_new, nb=nb, n_sel=min(N_SELECT, nb))
    per_b = lambda d1, d2: pl.BlockSpec((None, d1, d2), lambda b, pt: (b, 0, 0))
    per_b4 = lambda d1, d2, d3: pl.BlockSpec((None, d1, d2, d3), lambda b, pt: (b, 0, 0, 0))
    return pl.pallas_call(
        kern,
        grid_spec=pltpu.PrefetchScalarGridSpec(
            num_scalar_prefetch=1,
            grid=(nb_batch,),
            in_specs=[pl.BlockSpec(memory_space=pl.ANY),
                      per_b(8, D_NSA), per_b4(2, KV_W, LANES), per_b4(2, KV_W, LANES),
                      pl.BlockSpec((None, None, 2, KV_W, wbuf), lambda b, pt: (b, layer, 0, 0, 0)),
                      per_b(nbp, 2 * KV_W), per_b(8, LANES)],
            out_specs=[per_b(8, D_NSA), per_b4(2, KV_W, wbuf)],
            scratch_shapes=[pltpu.VMEM((2, KV_W, nk), F32), pltpu.VMEM((2, KV_W, nk), F32),
                            pltpu.VMEM((2, KV_W, wk), F32), pltpu.SemaphoreType.DMA((2,))]),
        out_shape=[jax.ShapeDtypeStruct((nb_batch, 8, D_NSA), F32),
                   jax.ShapeDtypeStruct((nb_batch, 2, KV_W, wbuf), F32)],
        compiler_params=_cparams(("arbitrary",)),
        name="nsa_sample",
    )(page_table, cache5, q3, newselt, wnewt, state_wint, kcvc, gate3)


def _s5_setup_kernel(are_ref, aim_ref, ldt_ref, bre_ref, bim_ref, cre_ref, cim_ref,
                     ckr_ref, cki_ref, bkr_ref, bki_ref, m_ref, pwr_ref, pwi_ref):
    nk = pwr_ref.shape[1]

    def body(g, carry):
        ar = are_ref[g]
        ai = aim_ref[g]
        dt = jnp.exp(ldt_ref[g])
        mag = jnp.exp(ar * dt)
        abr = mag * jnp.cos(ai * dt)
        abi = mag * jnp.sin(ai * dt)
        den = ar * ar + ai * ai
        nr = abr - 1.0
        e_re = (nr * ar + abi * ai) / den
        e_im = (abi * ar - nr * ai) / den
        bre = bre_ref[g]
        bim = bim_ref[g]
        bbr = e_re * bre - e_im * bim
        bbi = e_re * bim + e_im * bre
        kk = lax.broadcasted_iota(jnp.int32, (nk, S5_STATE), 0).astype(F32)
        pmag = jnp.exp(kk * (ar * dt))
        pwr = pmag * jnp.cos(kk * (ai * dt))
        pwi = pmag * jnp.sin(kk * (ai * dt))
        pwr_ref[g] = pwr
        pwi_ref[g] = pwi
        cre = cre_ref[g]
        cim = cim_ref[g]
        for k in range(S5_CHUNK + 1):
            wr = pwr[k:k + 1, :]
            wi = pwi[k:k + 1, :]
            ckr_ref[g, k * S5_GROUP:(k + 1) * S5_GROUP, :] = cre * wr - cim * wi
            cki_ref[g, k * S5_GROUP:(k + 1) * S5_GROUP, :] = -(cre * wi + cim * wr)
            if k < S5_CHUNK:
                bkr_ref[g, k * S5_GROUP:(k + 1) * S5_GROUP, :] = bbr * wr - bbi * wi
                bki_ref[g, k * S5_GROUP:(k + 1) * S5_GROUP, :] = bbr * wi + bbi * wr
        nl = S5_CHUNK * S5_GROUP
        hp = lax.Precision.HIGHEST
        m_ref[g] = (lax.dot_general(ckr_ref[g, 0:nl, :], bbr, (((1,), (1,)), ((), ())),
                                    precision=hp, preferred_element_type=F32)
                    + lax.dot_general(cki_ref[g, 0:nl, :], bbi, (((1,), (1,)), ((), ())),
                                      precision=hp, preferred_element_type=F32))
        return carry

    lax.fori_loop(0, S5_GROUPS, body, 0)


def _s5_setup(a_re, a_im, log_dt, b_re, b_im, c_re, c_im):
    g, p, c = S5_GROUPS, S5_STATE, S5_GROUP
    nk = 24
    outs = [(DEPTH, g, (S5_CHUNK + 1) * c, p), (DEPTH, g, (S5_CHUNK + 1) * c, p),
            (DEPTH, g, S5_CHUNK * c, p), (DEPTH, g, S5_CHUNK * c, p),
            (DEPTH, g, S5_CHUNK * c, c), (DEPTH, g, nk, p), (DEPTH, g, nk, p)]
    lay = lambda s: pl.BlockSpec((None,) + s[1:], lambda l: (l,) + (0,) * (len(s) - 1))
    ins = [a_re.reshape(DEPTH, g, 1, p), a_im.reshape(DEPTH, g, 1, p),
           jnp.broadcast_to(log_dt[:, :, None, None], (DEPTH, g, 1, p)),
           b_re.transpose(0, 1, 3, 2), b_im.transpose(0, 1, 3, 2), c_re, c_im]
    return pl.pallas_call(
        _s5_setup_kernel,
        grid=(DEPTH,),
        in_specs=[lay(x.shape) for x in ins],
        out_specs=[lay(s) for s in outs],
        out_shape=[jax.ShapeDtypeStruct(s, F32) for s in outs],
        compiler_params=_cparams(("arbitrary",)),
        name="s5_setup",
    )(*ins)


def _s5_chunk_kernel(u_ref, h0_ref, kmat_ref, smat_ref, ymat_ref, a1_ref, a2_ref,
                     y_ref, hl_ref, s_sc, hp_sc, *, gb, nj, nbt):
    for gi in range(gb):
        s_sc[gi] = _dot(u_ref[gi], smat_ref[gi])

    def step(j, hs):
        r0 = pl.multiple_of(j * nbt, nbt)
        new = []
        for gi in range(gb):
            h = hs[gi]
            hp_sc[gi, pl.ds(r0, nbt), :] = h
            new.append(a1_ref[gi] * h + a2_ref[gi] * pltpu.roll(h, S5_STATE, 1)
                       + s_sc[gi, pl.ds(r0, nbt), :])
        return tuple(new)

    hs = lax.fori_loop(0, nj, step, tuple(h0_ref[gi] for gi in range(gb)))
    for gi in range(gb):
        hl_ref[gi] = hs[gi]
        y_ref[gi] = _dot(u_ref[gi], kmat_ref[gi]) + _dot(hp_sc[gi].astype(BF16), ymat_ref[gi])


def _s5_chunk(u_g, h0_g, kmat, smat, ymat, a1, a2, nbt):
    g, rows, w = u_g.shape
    nj = rows // nbt
    gb = 8
    st = 2 * S5_STATE
    blk = lambda d1, d2: pl.BlockSpec((gb, d1, d2), lambda i: (i, 0, 0))
    kern = functools.partial(_s5_chunk_kernel, gb=gb, nj=nj, nbt=nbt)
    return pl.pallas_call(
        kern,
        grid=(g // gb,),
        in_specs=[blk(rows, w), blk(nbt, st), blk(w, w), blk(w, st), blk(st, w), blk(1, st), blk(1, st)],
        out_specs=[blk(rows, w), blk(nbt, st)],
        out_shape=[jax.ShapeDtypeStruct((g, rows, w), F32), jax.ShapeDtypeStruct((g, nbt, st), F32)],
        scratch_shapes=[pltpu.VMEM((gb, rows, st), F32), pltpu.VMEM((gb, rows, st), F32)],
        compiler_params=_cparams(("arbitrary",)),
        name="s5_chunk",
    )(u_g, h0_g, kmat, smat, ymat, a1, a2)


def _s5_rows_kernel(u_ref, w_ref, sw_ref, yw_ref, a1_ref, a2_ref, y_ref, hl_ref,
                    xr_sc, sg_sc, sgs_sc, hp_sc, hcat_sc, *, nbl, nj):
    L = S5_CHUNK
    mr = nbl * nj
    ng = LANES // S5_GROUP
    st = 2 * S5_STATE
    for s in range(L):
        xr_sc[:, (L - 1 - s) * LANES:(L - s) * LANES] = u_ref[pl.ds(s, mr, stride=L), :].astype(BF16)
    sall = _dot(xr_sc[...], sw_ref[...])
    for g in range(ng):
        s_g = sall[:, g * st:(g + 1) * st]
        sg_sc[g] = s_g
        sgs_sc[g] = pltpu.roll(s_g, S5_STATE, 1)

    def step(j, carry):
        hs, hss = carry
        new, news = [], []
        for g in range(ng):
            h, hsw = hs[g], hss[g]
            a1, a2 = a1_ref[g], a2_ref[g]
            hp_sc.at[g][pl.ds(j, nbl, stride=nj), :] = h
            new.append(a1 * h + a2 * hsw + sg_sc.at[g][pl.ds(j, nbl, stride=nj), :])
            news.append(a1 * hsw - a2 * h + sgs_sc.at[g][pl.ds(j, nbl, stride=nj), :])
        return tuple(new), tuple(news)

    zero = tuple(jnp.zeros((nbl, st), F32) for _ in range(ng))
    hs, _ = lax.fori_loop(0, nj, step, (zero, zero))
    for g in range(ng):
        hl_ref[g] = hs[g]
        hcat_sc[:, g * st:(g + 1) * st] = hp_sc[g].astype(BF16)
    for tp in range(L // 2):
        t1 = 2 * tp + 1
        y = (_dot(xr_sc[:, (L - 1 - t1) * LANES:], w_ref[0:(t1 + 1) * LANES, :])
             + _dot(hcat_sc[...], yw_ref[:, (t1 - 1) * LANES:(t1 + 1) * LANES]))
        y_ref[pl.ds(t1 - 1, mr, stride=L), :] = y[:, :LANES]
        y_ref[pl.ds(t1, mr, stride=L), :] = y[:, LANES:]


def _s5_rows(u4, w, sw, yw, a1, a2, nb_batch, t):
    nch, rows, _ = u4.shape
    nsplit = 2
    nbl = nb_batch // nsplit
    nj = t // S5_CHUNK
    rb = nbl * t
    mr = nbl * nj
    ng = LANES // S5_GROUP
    st = 2 * S5_STATE
    kern = functools.partial(_s5_rows_kernel, nbl=nbl, nj=nj)
    per_c = lambda *s: pl.BlockSpec((None,) + s, lambda c, h: (c,) + (0,) * len(s))
    return pl.pallas_call(
        kern,
        grid=(nch, nsplit),
        in_specs=[pl.BlockSpec((None, rb, LANES), lambda c, h: (c, h, 0)),
                  per_c(S5_CHUNK * LANES, 2 * LANES), per_c(S5_CHUNK * LANES, ng * st),
                  per_c(ng * st, S5_CHUNK * LANES), per_c(ng, 1, st), per_c(ng, 1, st)],
        out_specs=[pl.BlockSpec((None, rb, LANES), lambda c, h: (c, h, 0)),
                   pl.BlockSpec((None, ng, None, nbl, st), lambda c, h: (c, 0, h, 0, 0))],
        out_shape=[jax.ShapeDtypeStruct((nch, rows, LANES), F32),
                   jax.ShapeDtypeStruct((nch, ng, nsplit, nbl, st), F32)],
        scratch_shapes=[pltpu.VMEM((mr, S5_CHUNK * LANES), BF16), pltpu.VMEM((ng, mr, st), F32),
                        pltpu.VMEM((ng, mr, st), F32), pltpu.VMEM((ng, mr, st), F32),
                        pltpu.VMEM((mr, ng * st), BF16)],
        compiler_params=_cparams(("arbitrary", "arbitrary")),
        name="s5_rows",
    )(u4, w, sw, yw, a1, a2)


def _post_kernel(x_ref, gmod_ref, yssm_ref, u_ref, zs5_ref, o_ref, znsa_ref, merge_ref,
                 d_ref, gluw_ref, glub_ref, ws5_ref, wnsa_ref, wo_ref, xo_ref):
    y = jnp.concatenate([yssm_ref[ch] + d_ref[ch] * u_ref[ch] for ch in range(D_S5 // LANES)], axis=1)
    y = 0.5 * y * (1.0 + jnp.tanh(math.sqrt(2.0 / math.pi) * (y + 0.044715 * (y * y * y))))
    y = y * jax.nn.sigmoid(_dot(y.astype(BF16), gluw_ref[...]) + glub_ref[...])
    y = y * _silu(zs5_ref[...].astype(F32))
    b_s5 = _dot(y.astype(BF16), ws5_ref[...])
    b_nsa = _dot((o_ref[...] * _silu(znsa_ref[...].astype(F32))).astype(BF16), wnsa_ref[...])
    m = jax.nn.sigmoid(merge_ref[...].astype(F32))
    mix = m[:, :D_MODEL] * b_s5 + m[:, D_MODEL:] * b_nsa
    xo_ref[...] = x_ref[...] + gmod_ref[...] * _dot(mix.astype(BF16), wo_ref[...])


def _post(x2d, gmod, yssm, u, zs5, o, znsa, merge, d, gluw, glub, ws5, wnsa, wo,
          *, tm, tiles_per_b, per_row_mod):
    rows = x2d.shape[0]
    if per_row_mod:
        mod_spec = pl.BlockSpec((tm, D_MODEL), lambda i: (i, 0))
    else:
        mod_spec = pl.BlockSpec((None, 1, D_MODEL), lambda i: (i // tiles_per_b, 0, 0))
    row = lambda w: pl.BlockSpec((tm, w), lambda i: (i, 0))
    full = lambda a, b: pl.BlockSpec((a, b), lambda i: (0, 0))
    nch = D_S5 // LANES
    chunked = pl.BlockSpec((nch, tm, LANES), lambda i: (0, i, 0))
    return pl.pallas_call(
        _post_kernel,
        grid=(rows // tm,),
        in_specs=[row(D_MODEL), mod_spec, chunked, chunked, row(512), row(512), row(512), row(2048),
                  pl.BlockSpec((nch, 1, LANES), lambda i: (0, 0, 0)),
                  full(512, 512), full(1, 512), full(512, D_MODEL), full(512, D_MODEL),
                  full(D_MODEL, D_MODEL)],
        out_specs=row(D_MODEL),
        out_shape=jax.ShapeDtypeStruct((rows, D_MODEL), F32),
        compiler_params=_cparams(("arbitrary",)),
        name="post",
    )(x2d, gmod, yssm, u, zs5, o, znsa, merge, d, gluw, glub, ws5, wnsa, wo)


def _final_norm_kernel(x_ref, g_ref, o_ref):
    x = x_ref[...]
    o_ref[...] = x * lax.rsqrt(jnp.mean(x * x, axis=-1, keepdims=True) + RMS_EPS) * g_ref[...]


def _final_norm(x2d, g, tm):
    rows = x2d.shape[0]
    return pl.pallas_call(
        _final_norm_kernel,
        grid=(rows // tm,),
        in_specs=[pl.BlockSpec((tm, D_MODEL), lambda i: (i, 0)), pl.BlockSpec((1, D_MODEL), lambda i: (0, 0))],
        out_specs=pl.BlockSpec((tm, D_MODEL), lambda i: (i, 0)),
        out_shape=jax.ShapeDtypeStruct((rows, D_MODEL), F32),
        compiler_params=_cparams(("arbitrary",)),
        name="final_norm",
    )(x2d, g)


def _head_perm():
    idx = [HEAD_DIM * (GROUP_SIZE * g + r) + d
           for r in range(GROUP_SIZE) for g in range(N_KV) for d in range(HEAD_DIM)]
    return np.asarray(idx, np.int32)


def _rope_tables(pos, width=LANES):
    inv = ROPE_THETA ** (-jnp.arange(ROT_HALF, dtype=F32) / ROT_HALF)
    ang = pos.astype(F32)[:, None] * inv[None, :]
    cos, sin = jnp.cos(ang), jnp.sin(ang)
    n = pos.shape[0]
    one = jnp.ones((n, HEAD_DIM - 2 * ROT_HALF), F32)
    zero8 = jnp.zeros((n, ROT_HALF), F32)
    zero = jnp.zeros((n, HEAD_DIM - 2 * ROT_HALF), F32)
    c = jnp.concatenate([cos, cos, one], axis=1)
    sa = jnp.concatenate([-sin, zero8, zero], axis=1)
    sb = jnp.concatenate([zero8, sin, zero], axis=1)
    rep = width // HEAD_DIM
    return tuple(jnp.tile(t, (1, rep)) for t in (c, sa, sb))


def _blockdiag2(w):
    z = jnp.zeros_like(w)
    return jnp.concatenate([jnp.concatenate([w, z], -1), jnp.concatenate([z, w], -1)], -2)


def _s5_matrices(tabs, l, t_eff):
    ckr, cki, bkr, bki, m, pwr, pwi = [t[l] for t in tabs]
    g, c, L = S5_GROUPS, S5_GROUP, S5_CHUNK
    mk = m.reshape(g, L, c, c)
    s_idx = np.arange(L)[:, None]
    t_idx = np.arange(L)[None, :]
    lag = np.clip(t_idx - s_idx, 0, L - 1)
    kfull = mk[:, lag]
    kfull = jnp.where((t_idx >= s_idx)[None, :, :, None, None], kfull, 0.0)
    kmat = kfull.transpose(0, 1, 4, 2, 3).reshape(g, L * c, L * c).astype(BF16)
    ck = jnp.concatenate([ckr, cki], axis=-1)
    ymat = ck[:, c:, :].transpose(0, 2, 1).astype(BF16)
    bk = jnp.concatenate([bkr, bki], axis=-1).reshape(g, L, c, 2 * S5_STATE)
    sm = bk[:, t_eff - 1::-1] if t_eff == L else bk[:, np.arange(t_eff - 1, -1, -1)]
    if t_eff < L:
        sm = jnp.concatenate([sm, jnp.zeros((g, L - t_eff, c, 2 * S5_STATE), F32)], axis=1)
    smat = sm.reshape(g, L * c, 2 * S5_STATE).astype(BF16)
    ar = pwr[:, t_eff][:, None, :]
    ai = pwi[:, t_eff][:, None, :]
    a1 = jnp.concatenate([ar, ar], axis=-1)
    a2 = jnp.concatenate([-ai, ai], axis=-1)
    return kmat, smat, ymat, a1, a2


def _s5_row_weights(tabs):
    ckr, cki, bkr, bki, m, pwr, pwi = tabs
    c, L, st = S5_GROUP, S5_CHUNK, 2 * S5_STATE
    ng = LANES // c
    nch = S5_GROUPS // ng
    eye = jnp.eye(ng, dtype=BF16)
    mk = m.astype(BF16).reshape(DEPTH, nch, ng, L, c, c)
    w = jnp.einsum('dhgloi,gq->dhlgiqo', mk, eye).reshape(DEPTH, nch, L * LANES, LANES)
    w_shift = jnp.concatenate([jnp.zeros_like(w[:, :, :LANES]), w[:, :, :-LANES]], axis=2)
    w2 = jnp.concatenate([w_shift, w], axis=-1)
    bk = jnp.concatenate([bkr, bki], axis=-1).astype(BF16).reshape(DEPTH, nch, ng, L, c, st)
    sw = jnp.einsum('dhgkcp,gq->dhkgcqp', bk, eye).reshape(DEPTH, nch, L * LANES, ng * st)
    ck = jnp.concatenate([ckr, cki], axis=-1).astype(BF16).reshape(DEPTH, nch, ng, L + 1, c, st)[:, :, :, 1:]
    yw = jnp.einsum('dhgtcp,gq->dhgptqc', ck, eye).reshape(DEPTH, nch, ng * st, L * LANES)
    ar = pwr[:, :, L].reshape(DEPTH, nch, ng, 1, S5_STATE)
    ai = pwi[:, :, L].reshape(DEPTH, nch, ng, 1, S5_STATE)
    a1 = jnp.concatenate([ar, ar], axis=-1)
    a2 = jnp.concatenate([-ai, ai], axis=-1)
    return w2, sw, yw, a1, a2


def _all_layer_weights(w_in, cmp_pe, cmp_w1, cmp_w2, s5_glu_w, w_s5_out, w_nsa_out, w_o, perm):
    gate_w = jnp.pad(w_in[:, :, 2304:2328], ((0, 0), (0, 0), (0, LANES - 3 * N_HEADS)))
    wp = jnp.concatenate([w_in[:, :, :1024], w_in[:, :, 1024:1536][:, :, perm], w_in[:, :, 1536:2304],
                          w_in[:, :, 2328:2840][:, :, perm], w_in[:, :, 2840:], gate_w], axis=2).astype(BF16)
    pe2 = jnp.concatenate([cmp_pe, cmp_pe], axis=-1)[:, :, :, None, :]
    w1bd = _blockdiag2(cmp_w1.astype(BF16).reshape(DEPTH, 2, BLOCK, HEAD_DIM, HEAD_DIM))
    w1bd = w1bd.reshape(DEPTH, 2, BLOCK * KV_W, KV_W)
    w2bd = _blockdiag2(cmp_w2.astype(BF16))
    wt = w_in[:, :, 1536:2304].transpose(0, 2, 1).astype(BF16)
    return dict(wp=wp, wt=wt, pe2=pe2, w1bd=w1bd, w2bd=w2bd, gluw=s5_glu_w.astype(BF16),
                ws5=w_s5_out.astype(BF16), wnsa=w_nsa_out[:, perm, :].astype(BF16),
                wo=w_o.astype(BF16))


def _to_groups(u2d, nbt, nj):
    u5 = u2d.reshape(nbt, nj, S5_CHUNK, S5_GROUPS, S5_GROUP)
    return u5.transpose(3, 1, 0, 2, 4).reshape(S5_GROUPS, nj * nbt, S5_CHUNK * S5_GROUP)


def _from_groups(y_g, nbt, nj):
    y5 = y_g.reshape(S5_GROUPS, nj, nbt, S5_CHUNK, S5_GROUP)
    return y5.transpose(2, 1, 3, 0, 4).reshape(nbt * nj * S5_CHUNK, D_S5)


def _state_to_groups(h):
    return h.transpose(2, 0, 1, 3).reshape(S5_GROUPS, h.shape[0], 2 * S5_STATE)


def _state_from_groups(hg):
    g, b, _ = hg.shape
    return hg.reshape(g, b, 2, S5_STATE).transpose(1, 2, 0, 3)


def kernel(x_prompt, x_sample, c_prompt, c_sample, cache_kv, page_table, state_win, state_ssm, ada_w, ada_b, norm_g, w_in, s5_a_re, s5_a_im, s5_log_dt, s5_b_re, s5_b_im, s5_c_re, s5_c_im, s5_d, s5_glu_w, s5_glu_b, cmp_pe, cmp_w1, cmp_w2, w_s5_out, w_nsa_out, w_o, final_g):
    bp, tp, _ = x_prompt.shape
    bs, ts, _ = x_sample.shape
    n_pool, _, page = cache_kv.shape[:3]
    n_pages = page_table.shape[1]
    past = n_pages * page
    wbuf = state_win.shape[2]
    assert tp % S5_CHUNK == 0 and tp % 128 == 0 and ts <= 8 and bs % 8 == 0 and bp % 8 == 0

    perm = _head_perm()
    mod = _ada_mod(jnp.concatenate([c_prompt, c_sample], axis=0), ada_w, ada_b)
    s5tabs = _s5_setup(s5_a_re, s5_a_im, s5_log_dt, s5_b_re, s5_b_im, s5_c_re, s5_c_im)

    tabs_p = _rope_tables(jnp.arange(tp))
    rs = bs * ts
    tabs_s = _rope_tables(jnp.tile(past + jnp.arange(ts), bs))
    nb_p = tp // BLOCK
    nbp_p = (nb_p + LANES - 1) // LANES * LANES
    ctabs_p = _rope_tables(jnp.arange(nbp_p) * BLOCK + (BLOCK - 1))
    nb_s = (past + ts + BLOCK - 1) // BLOCK
    nbp_s = ((nb_s + 7) // 8 * 8 + LANES - 1) // LANES * LANES
    ctabs_s = _rope_tables(jnp.arange(nbp_s) * BLOCK + (BLOCK - 1))

    cache5 = cache_kv.transpose(0, 1, 3, 4, 5, 2).reshape(n_pool, DEPTH, 4, KV_W, page)
    state_wint = state_win.transpose(0, 1, 3, 4, 5, 2).reshape(bs, DEPTH, 2, KV_W, wbuf)
    tabs_pt = tuple(tb.T for tb in tabs_p)
    tm_p = 512 if tp % 512 == 0 else 256
    tq_p = 256
    nj_p = tp // S5_CHUNK
    consts = _attn_constants(tp, tq_p)
    keep = min(WINDOW, tp)

    xp = x_prompt.reshape(bp * tp, D_MODEL)
    xs = x_sample.reshape(rs, D_MODEL)
    kv_p, kv_s, win_p, win_s, ssm_p, ssm_s = [], [], [], [], [], []

    def pad_rows(a3):
        return jnp.pad(a3, ((0, 0), (0, 8 - ts), (0, 0)))

    def to_chunks(a2):
        return a2.reshape(a2.shape[0], D_S5 // LANES, LANES).transpose(1, 0, 2)

    def new_tiles(a3):
        a4 = a3.reshape(bs, ts, 2, KV_W).transpose(0, 2, 3, 1)
        return jnp.pad(a4, ((0, 0), (0, 0), (0, 0), (0, LANES - ts)))

    lw_all = _all_layer_weights(w_in, cmp_pe, cmp_w1, cmp_w2, s5_glu_w, w_s5_out, w_nsa_out, w_o, perm)
    s5w_all = _s5_row_weights(s5tabs)

    for l in range(DEPTH):
        lw = {name: val[l] for name, val in lw_all.items()}
        g_row = norm_g[l][None, :]
        d_row = s5_d[l].reshape(D_S5 // LANES, 1, LANES)
        glub = s5_glu_b[l][None, :]
        mp, ms = mod[l, :bp], mod[l, bp:]
        post_w = (d_row, lw['gluw'], glub, lw['ws5'], lw['wnsa'], lw['wo'])

        shift, scale, gmod = [mp[:, k * D_MODEL:(k + 1) * D_MODEL][:, None, :] for k in range(3)]
        u, zs5, q, kvt, wint, cmp, znsa, merge, gate = _inproj_prompt(
            xp, shift, scale, g_row, lw['wp'], lw['wt'], tabs_p, tabs_pt, tm=tm_p, nb_batch=bp, t=tp)
        kcvc = _cmp_prompt(cmp, bp, tp, nbp_p, lw['pe2'], lw['w1bd'], lw['w2bd'], ctabs_p)
        o = _nsa_prompt(q, kvt, wint, kcvc, gate, consts, bp, tp, tq_p)
        yssm, hl = _s5_rows(u, *[a[l] for a in s5w_all], bp, tp)
        xp = _post(xp, gmod, yssm, u, zs5, o, znsa, merge, *post_w,
                   tm=tm_p, tiles_per_b=tp // tm_p, per_row_mod=False)
        kv_p.append(kvt)
        win_p.append(wint[:, :, tp - keep:])
        ssm_p.append(_state_from_groups(hl.reshape(S5_GROUPS, bp, 2 * S5_STATE)))

        shift, scale, gmod = [jnp.repeat(ms[:, k * D_MODEL:(k + 1) * D_MODEL], ts, axis=0) for k in range(3)]
        u, zs5, q, kv, win, znsa, merge, gate = _inproj(
            xs, shift, scale, g_row, lw['wp'], tabs_s, tm=rs, tiles_per_b=1, per_row_mod=True)
        kv3 = kv.reshape(bs, ts, 4 * KV_W)
        kcvc = _cmp_sample(page_table, cache5, pad_rows(kv3[:, :, :2 * KV_W]), l, nbp_s,
                           lw['pe2'], lw['w1bd'], lw['w2bd'], ctabs_s, ts)
        o8, wst = _nsa_sample(page_table, cache5, pad_rows(q.reshape(bs, ts, D_NSA)),
                              new_tiles(kv3[:, :, 2 * KV_W:]), new_tiles(win.reshape(bs, ts, 2 * KV_W)),
                              state_wint, kcvc, pad_rows(gate.reshape(bs, ts, LANES)), l, ts)
        o = o8[:, :ts].reshape(rs, D_NSA)
        mats = _s5_matrices(s5tabs, l, ts)
        u_pad = jnp.pad(u.reshape(bs, ts, D_S5), ((0, 0), (0, S5_CHUNK - ts), (0, 0)))
        y_g, hl = _s5_chunk(_to_groups(u_pad.reshape(bs * S5_CHUNK, D_S5), bs, 1).astype(BF16),
                            _state_to_groups(state_ssm[:, l]), *mats, nbt=bs)
        yssm = _from_groups(y_g, bs, 1).reshape(bs, S5_CHUNK, D_S5)[:, :ts].reshape(rs, D_S5)
        xs = _post(xs, gmod, to_chunks(yssm), to_chunks(u), zs5, o, znsa, merge, *post_w,
                   tm=rs, tiles_per_b=1, per_row_mod=True)
        kv_s.append(kv.reshape(bs, ts, 4, N_KV, HEAD_DIM))
        win_s.append(wst)
        ssm_s.append(_state_from_groups(hl))

    def from_t(parts, n_streams):
        a = jnp.stack(parts, axis=1)
        nbt, keys = a.shape[0], a.shape[-1]
        return a.reshape(nbt, DEPTH, n_streams, N_KV, HEAD_DIM, keys).transpose(0, 1, 5, 2, 3, 4)

    fg = final_g[None, :]
    y_prompt = _final_norm(xp, fg, tm_p).reshape(bp, tp, D_MODEL)
    y_sample = _final_norm(xs, fg, rs).reshape(bs, ts, D_MODEL)
    return (y_prompt, y_sample, from_t(kv_p, 4), jnp.stack(kv_s, axis=1),
            from_t(win_p, 2), from_t(win_s, 2),
            jnp.stack(ssm_p, axis=1), jnp.stack(ssm_s, axis=1))
```

```python
import functools
import math

import numpy as np
import jax
import jax.numpy as jnp
from jax import lax
from jax.experimental import pallas as pl
from jax.experimental.pallas import tpu as pltpu

F32 = jnp.float32
BF16 = jnp.bfloat16

D_MODEL = 1024
DEPTH = 4
D_S5 = 512
S5_GROUP = 16
S5_GROUPS = 32
S5_STATE = 64
D_NSA = 512
HEAD_DIM = 64
N_HEADS = 8
N_KV = 2
GROUP_SIZE = 4
BLOCK = 64
N_SELECT = 16
WINDOW = 512
ROT_HALF = 8
ROPE_THETA = 500000.0
RMS_EPS = 1e-6
NEG_INF = -1e30
FORCED_SCORE = 1e4

LANES = 128
S5_CHUNK = 16
KV_W = N_KV * HEAD_DIM
N_PROJ = 4992
CMP_PITCH = 72
VMEM_LIMIT = 56 * 1024 * 1024


def _dot(a, b):
    return jnp.dot(a, b, preferred_element_type=F32)


def _dot_nt(a, b):
    return lax.dot_general(a, b, (((1,), (1,)), ((), ())), preferred_element_type=F32)


def _silu(x):
    return x * jax.nn.sigmoid(x)


def _rope128(x, c, sa, sb):
    return x * c + pltpu.roll(x, LANES - ROT_HALF, 1) * sa + pltpu.roll(x, ROT_HALF, 1) * sb


def _masked_exp(s, mask):
    s = jnp.where(mask, s, NEG_INF)
    m = jnp.max(s, axis=-1, keepdims=True)
    p = jnp.where(mask, jnp.exp(s - m), 0.0)
    return p, jnp.sum(p, axis=-1, keepdims=True)


def _safe_inv(l):
    return jnp.where(l > 0.0, 1.0 / l, 0.0)


def _cparams(sem):
    return pltpu.CompilerParams(dimension_semantics=sem, vmem_limit_bytes=VMEM_LIMIT)


def _ada_kernel(c_ref, w_ref, b_ref, o_ref):
    c = c_ref[...]
    o_ref[...] = _dot(_silu(c).astype(BF16), w_ref[...].astype(BF16)) + b_ref[...]


def _ada_mod(c_all, ada_w, ada_b):
    nc = c_all.shape[0]
    tn = 1024
    return pl.pallas_call(
        _ada_kernel,
        grid=(DEPTH, 3 * D_MODEL // tn),
        in_specs=[pl.BlockSpec((nc, D_MODEL), lambda l, n: (0, 0)),
                  pl.BlockSpec((None, D_MODEL, tn), lambda l, n: (l, 0, n)),
                  pl.BlockSpec((None, 1, tn), lambda l, n: (l, 0, n))],
        out_specs=pl.BlockSpec((None, nc, tn), lambda l, n: (l, 0, n)),
        out_shape=jax.ShapeDtypeStruct((DEPTH, nc, 3 * D_MODEL), F32),
        compiler_params=_cparams(("arbitrary", "arbitrary")),
        name="ada_mod",
    )(c_all, ada_w, ada_b.reshape(DEPTH, 1, 3 * D_MODEL))


def _inproj_kernel(x_ref, shift_ref, scale_ref, g_ref, w_ref, cos_ref, sa_ref, sb_ref,
                   u_ref, zs5_ref, q_ref, kv_ref, win_ref, znsa_ref, merge_ref, gate_ref):
    x = x_ref[...]
    h = x * lax.rsqrt(jnp.mean(x * x, axis=-1, keepdims=True) + RMS_EPS) * g_ref[...]
    h = h * (1.0 + scale_ref[...]) + shift_ref[...]
    hb = h.astype(BF16)

    def mm(lo, hi):
        return _dot(hb, w_ref[:, lo:hi])

    c, sa, sb = cos_ref[...], sa_ref[...], sb_ref[...]
    u_ref[...] = mm(0, 512)
    zs5_ref[...] = mm(512, 1024).astype(BF16)
    for r in range(GROUP_SIZE):
        lo = 1024 + r * LANES
        q = _rope128(mm(lo, lo + LANES), c, sa, sb) * (HEAD_DIM ** -0.5)
        q_ref[:, r * LANES:(r + 1) * LANES] = q.astype(BF16)
    kv_ref[:, 0:256] = mm(1536, 1792)
    kv_ref[:, 256:384] = _rope128(mm(1792, 1920), c, sa, sb)
    kv_ref[:, 384:512] = mm(1920, 2048)
    win_ref[:, 0:128] = _rope128(mm(2048, 2176), c, sa, sb)
    win_ref[:, 128:256] = mm(2176, 2304)
    znsa_ref[...] = mm(2304, 2816).astype(BF16)
    merge_ref[...] = mm(2816, 4864).astype(BF16)
    gate_ref[...] = mm(4864, 4992)


def _inproj(x2d, shift, scale, g, wp, tabs, *, tm, tiles_per_b, per_row_mod):
    rows = x2d.shape[0]
    nt = rows // tm
    n_tab = tabs[0].shape[0] // tm
    if per_row_mod:
        mod_spec = pl.BlockSpec((tm, D_MODEL), lambda i: (i, 0))
    else:
        mod_spec = pl.BlockSpec((None, 1, D_MODEL), lambda i: (i // tiles_per_b, 0, 0))
    tab_spec = pl.BlockSpec((tm, LANES), lambda i: (i % n_tab, 0))

    def row_spec(w):
        return pl.BlockSpec((tm, w), lambda i: (i, 0))

    widths = (512, 512, 512, 512, 256, 512, 2048, 128)
    dtypes = (F32, BF16, BF16, F32, F32, BF16, BF16, F32)
    return pl.pallas_call(
        _inproj_kernel,
        grid=(nt,),
        in_specs=[row_spec(D_MODEL), mod_spec, mod_spec,
                  pl.BlockSpec((1, D_MODEL), lambda i: (0, 0)),
                  pl.BlockSpec((D_MODEL, N_PROJ), lambda i: (0, 0)),
                  tab_spec, tab_spec, tab_spec],
        out_specs=[row_spec(w) for w in widths],
        out_shape=[jax.ShapeDtypeStruct((rows, w), d) for w, d in zip(widths, dtypes)],
        compiler_params=_cparams(("arbitrary",)),
        name="inproj",
    )(x2d, shift, scale, g, wp, *tabs)


def _rope128_t(x, c, sa, sb):
    return x * c + pltpu.roll(x, KV_W - ROT_HALF, 0) * sa + pltpu.roll(x, ROT_HALF, 0) * sb


def _inproj_prompt_kernel(x_ref, shift_ref, scale_ref, g_ref, w_ref, wt_ref, cos_ref, sa_ref, sb_ref,
                          cost_ref, sat_ref, sbt_ref,
                          u_ref, zs5_ref, q_ref, kvt_ref, wint_ref, cmp_ref, znsa_ref, merge_ref, gate_ref):
    x = x_ref[...]
    h = x * lax.rsqrt(jnp.mean(x * x, axis=-1, keepdims=True) + RMS_EPS) * g_ref[...]
    h = h * (1.0 + scale_ref[...]) + shift_ref[...]
    hb = h.astype(BF16)

    def mm(lo, hi):
        return _dot(hb, w_ref[:, lo:hi])

    c, sa, sb = cos_ref[...], sa_ref[...], sb_ref[...]
    for pair in range(D_S5 // (2 * LANES)):
        up = mm(pair * 2 * LANES, (pair + 1) * 2 * LANES)
        u_ref[2 * pair] = up[:, :LANES]
        u_ref[2 * pair + 1] = up[:, LANES:]
    zs5_ref[...] = mm(512, 1024).astype(BF16)
    for pair in range(GROUP_SIZE // 2):
        lo = 1024 + pair * 2 * LANES
        qp = mm(lo, lo + 2 * LANES)
        for k in range(2):
            q = _rope128(qp[:, k * LANES:(k + 1) * LANES], c, sa, sb) * (HEAD_DIM ** -0.5)
            q_ref[:, (2 * pair + k) * LANES:(2 * pair + k + 1) * LANES] = q.astype(BF16)
    cmp_ref[...] = mm(1536, 1792)
    znsa_ref[...] = mm(2304, 2816).astype(BF16)
    merge_ref[...] = mm(2816, 4864).astype(BF16)
    gate_ref[...] = mm(4864, 4992)

    ct, sat, sbt = cost_ref[...], sat_ref[...], sbt_ref[...]
    kvxt = _dot_nt(wt_ref[...], hb)

    def mmt(s):
        return kvxt[s * KV_W:(s + 1) * KV_W, :]

    kvt_ref[0:KV_W, :] = mmt(0)
    kvt_ref[KV_W:2 * KV_W, :] = mmt(1)
    kvt_ref[2 * KV_W:3 * KV_W, :] = _rope128_t(mmt(2), ct, sat, sbt)
    kvt_ref[3 * KV_W:4 * KV_W, :] = mmt(3)
    wint_ref[0:KV_W, :] = _rope128_t(mmt(4), ct, sat, sbt)
    wint_ref[KV_W:2 * KV_W, :] = mmt(5)


def _inproj_prompt(x2d, shift, scale, g, wp, wt, tabs, tabs_t, *, tm, nb_batch, t):
    rows = x2d.shape[0]
    tpb = t // tm
    mod_spec = pl.BlockSpec((None, 1, D_MODEL), lambda i: (i // tpb, 0, 0))
    tab_spec = pl.BlockSpec((tm, LANES), lambda i: (i % tpb, 0))
    tabt_spec = pl.BlockSpec((KV_W, tm), lambda i: (0, i % tpb))
    row = lambda w: pl.BlockSpec((tm, w), lambda i: (i, 0))
    tr = lambda h: pl.BlockSpec((None, h, tm), lambda i: (i // tpb, 0, i % tpb))
    u_spec = pl.BlockSpec((D_S5 // LANES, tm, LANES), lambda i: (0, i, 0))
    outs = [((D_S5 // LANES, rows, LANES), F32, u_spec), ((rows, 512), BF16, row(512)),
            ((rows, 512), BF16, row(512)),
            ((nb_batch, 4 * KV_W, t), F32, tr(4 * KV_W)), ((nb_batch, 2 * KV_W, t), F32, tr(2 * KV_W)),
            ((rows, 256), F32, row(256)), ((rows, 512), BF16, row(512)), ((rows, 2048), BF16, row(2048)),
            ((rows, 128), F32, row(128))]
    return pl.pallas_call(
        _inproj_prompt_kernel,
        grid=(rows // tm,),
        in_specs=[row(D_MODEL), mod_spec, mod_spec,
                  pl.BlockSpec((1, D_MODEL), lambda i: (0, 0)),
                  pl.BlockSpec((D_MODEL, N_PROJ), lambda i: (0, 0)),
                  pl.BlockSpec((6 * KV_W, D_MODEL), lambda i: (0, 0)),
                  tab_spec, tab_spec, tab_spec, tabt_spec, tabt_spec, tabt_spec],
        out_specs=[o[2] for o in outs],
        out_shape=[jax.ShapeDtypeStruct(o[0], o[1]) for o in outs],
        compiler_params=_cparams(("arbitrary",)),
        name="inproj_prompt",
    )(x2d, shift, scale, g, wp, wt, *tabs, *tabs_t)


def _compress(load_p, nbk, pe_ref, w1_ref, w2_ref, c, sa, sb, xcat_sc):
    rows = xcat_sc.shape[0]
    outs = []
    for s in range(2):
        for p in range(BLOCK):
            xp = load_p(s, p) + pe_ref[s, p]
            if rows > nbk:
                xp = jnp.concatenate([xp, jnp.zeros((rows - nbk, KV_W), F32)], axis=0)
            xcat_sc[:, p * KV_W:(p + 1) * KV_W] = xp.astype(BF16)
        acc = _dot(xcat_sc[...], w1_ref[s])[0:nbk]
        outs.append(_dot(_silu(acc).astype(BF16), w2_ref[s]))
    return jnp.concatenate([_rope128(outs[0], c, sa, sb), outs[1]], axis=1)


def _cmp_prompt_kernel(k_ref, v_ref, pe_ref, w1_ref, w2_ref, cos_ref, sa_ref, sb_ref, o_ref, xcat_sc,
                       *, nbk):
    def load_p(s, p):
        return (k_ref, v_ref)[s][pl.ds(p, nbk, stride=BLOCK), :]

    res = _compress(load_p, nbk, pe_ref, w1_ref, w2_ref,
                    cos_ref[0:nbk, :], sa_ref[0:nbk, :], sb_ref[0:nbk, :], xcat_sc)
    o_ref[...] = jnp.zeros(o_ref.shape, F32)
    o_ref[0:nbk, :] = res


def _cmp_prompt(kv2d, nb_batch, t, nbp, pe2, w1bd, w2bd, ctabs):
    nbk = t // BLOCK
    full = lambda shape: pl.BlockSpec(shape, lambda b: (0,) * len(shape))
    return pl.pallas_call(
        functools.partial(_cmp_prompt_kernel, nbk=nbk),
        grid=(nb_batch,),
        in_specs=[pl.BlockSpec((t, KV_W), lambda b: (b, 0)), pl.BlockSpec((t, KV_W), lambda b: (b, 1)),
                  full((2, BLOCK, 1, KV_W)), full((2, BLOCK * KV_W, KV_W)),
                  full((2, KV_W, KV_W)), full((nbp, LANES)), full((nbp, LANES)), full((nbp, LANES))],
        out_specs=pl.BlockSpec((None, nbp, 2 * KV_W), lambda b: (b, 0, 0)),
        out_shape=jax.ShapeDtypeStruct((nb_batch, nbp, 2 * KV_W), F32),
        scratch_shapes=[pltpu.VMEM(((nbk + 15) // 16 * 16, BLOCK * KV_W), BF16)],
        compiler_params=_cparams(("arbitrary",)),
        name="cmp_prompt",
    )(kv2d, kv2d, pe2, w1bd, w2bd, *ctabs)


def _stack_queries(qt, tq):
    lane = lax.broadcasted_iota(jnp.int32, (tq, LANES), 1)
    qt = qt.astype(F32)
    blocks = []
    for g in range(N_KV):
        keep = (lane < HEAD_DIM) if g == 0 else (lane >= HEAD_DIM)
        for r in range(GROUP_SIZE):
            blocks.append(jnp.where(keep, qt[:, r * LANES:(r + 1) * LANES], 0.0))
    return jnp.concatenate(blocks, axis=0).astype(BF16)


def _select_blocks(imp, qblk, nb, n_sel):
    nq, nbp = imp.shape
    n = lax.broadcasted_iota(jnp.int32, (nq, nbp), 1)
    forced = (n == 0) | (n == qblk) | (n == qblk - 1)
    imp = jnp.where(forced, FORCED_SCORE, imp)
    imp = jnp.where(n > qblk, -1.0, imp)
    imp = jnp.where(n >= nb, -2.0, imp)
    rank = jnp.zeros((nq, nbp), F32)
    for m in range(nb):
        col = imp[:, m:m + 1]
        beats = (col > imp) | ((col == imp) & (n > m))
        rank = rank + jnp.where(beats, 1.0, 0.0)
    return jnp.where((rank < n_sel) & (imp > -0.5), 1.0, 0.0)


def _compressed_branch(qs, kcvc, tq_col, tq, nb):
    nbp = kcvc.shape[0]
    kc = kcvc[:, :KV_W].astype(BF16)
    vc = kcvc[:, KV_W:].astype(BF16)
    s = _dot_nt(qs, kc)
    n = lax.broadcasted_iota(jnp.int32, (1, nbp), 1)
    mask = (n * BLOCK + (BLOCK - 1) <= tq_col) & (n < nb)
    p, l = _masked_exp(s, mask)
    p = p * _safe_inv(l)
    o_c = _dot(p.astype(BF16), vc)
    imps = []
    for g in range(N_KV):
        acc = p[(g * GROUP_SIZE) * tq:(g * GROUP_SIZE + 1) * tq]
        for r in range(1, GROUP_SIZE):
            acc = acc + p[(g * GROUP_SIZE + r) * tq:(g * GROUP_SIZE + r + 1) * tq]
        imps.append(acc)
    return o_c, imps


def _block_mask(selstack, k0, tk):
    nbp = selstack.shape[1]
    kblk = (k0 + lax.broadcasted_iota(jnp.int32, (nbp, tk), 1)) // BLOCK
    e = jnp.where(kblk == lax.broadcasted_iota(jnp.int32, (nbp, tk), 0), 1.0, 0.0).astype(BF16)
    return _dot(selstack, e) > 0.5


def _gated_heads(gt, branches, tq):
    lane = lax.broadcasted_iota(jnp.int32, (tq, LANES), 1)
    chunks = []
    for r in range(GROUP_SIZE):
        halves = []
        for g in range(N_KV):
            h = g * GROUP_SIZE + r
            acc = None
            for br, ob in enumerate(branches):
                term = gt[:, br * N_HEADS + h:br * N_HEADS + h + 1] * ob[h * tq:(h + 1) * tq]
                acc = term if acc is None else acc + term
            halves.append(acc)
        chunks.append(jnp.where(lane < HEAD_DIM, halves[0], halves[1]))
    return chunks


def _select_blocks_t(imp, qblk_row, nb, n_sel):
    nq, nbp = imp.shape
    nbr = (nb + 7) // 8 * 8
    x = imp.T[0:nbr, :]
    n = lax.broadcasted_iota(jnp.int32, (nbr, nq), 0)
    forced = (n == 0) | (n == qblk_row) | (n == qblk_row - 1)
    x = jnp.where(forced, FORCED_SCORE, x)
    x = jnp.where(n > qblk_row, -1.0, x)
    x = jnp.where(n >= nb, -2.0, x)
    rank = jnp.zeros((nbr, nq), F32)
    for m in range(nb):
        row = x[m:m + 1, :]
        beats = (row > x) | ((row == x) & (n > m))
        rank = rank + jnp.where(beats, 1.0, 0.0)
    sel = jnp.where((rank < n_sel) & (x > -0.5), 1.0, 0.0)
    if nbr < nbp:
        sel = jnp.concatenate([sel, jnp.zeros((nbp - nbr, nq), F32)], axis=0)
    return sel.T


def _nsa_prompt_kernel(q_ref, kvt_ref, wint_ref, kcvc_ref, gate_ref, e_ref, cbias_ref, wbias_ref,
                       o_ref, kaug_sc, vsel_sc, waug_sc, vwin_sc, s_sc, m_sc, l_sc, acc_sc, sel_sc,
                       *, tq, nb, n_sel):
    i = pl.program_id(1)
    q0 = pl.multiple_of(i * tq, tq)
    rows = N_HEADS * tq
    wk = WINDOW + tq
    tk = 2 * LANES
    neg = jnp.asarray(NEG_INF, F32)

    @pl.when(i == 0)
    def _():
        kaug_sc[0:KV_W, :] = kvt_ref[0:KV_W, :].astype(BF16)
        kaug_sc[KV_W:2 * KV_W, :] = e_ref[...]
        vsel_sc[...] = kvt_ref[KV_W:2 * KV_W, :].astype(BF16)
        waug_sc[...] = jnp.zeros(waug_sc.shape, BF16)
        waug_sc[0:KV_W, WINDOW:] = wint_ref[0:KV_W, :].astype(BF16)
        waug_sc[KV_W:KV_W + 16, 0:WINDOW] = jnp.ones((16, WINDOW), BF16)
        vwin_sc[:, 0:WINDOW] = jnp.zeros((KV_W, WINDOW), BF16)
        vwin_sc[:, WINDOW:] = wint_ref[KV_W:2 * KV_W, :].astype(BF16)

    qs = _stack_queries(q_ref[...], tq)
    t_loc = lax.broadcasted_iota(jnp.int32, (rows, 1), 0) % tq
    tq_col = q0 + t_loc

    o_c, imps = _compressed_branch(qs, kcvc_ref[...], tq_col, tq, nb)

    sel_sc[...] = jnp.ones(sel_sc.shape, F32)

    @pl.when(q0 + tq > n_sel * BLOCK)
    def _():
        qblk_row = (q0 + lax.broadcasted_iota(jnp.int32, (1, tq), 1)) // BLOCK
        for g in range(N_KV):
            sel_sc[g] = _select_blocks_t(imps[g], qblk_row, nb, n_sel)

    selb = [((sel_sc[g] - 1.0) * (-NEG_INF)).astype(BF16) for g in range(N_KV)]
    selstack = jnp.concatenate([selb[g] for g in range(N_KV) for _ in range(GROUP_SIZE)], axis=0)
    qaug = jnp.concatenate([qs, selstack], axis=1)

    lane = lax.broadcasted_iota(jnp.int32, (rows, LANES), 1)
    padcol = jnp.where(lane == 0, neg, 0.0).astype(BF16)
    qaug_w = jnp.concatenate([qs, padcol], axis=1)
    n_full = q0 // tk
    par = (q0 % tk) // tq
    m_sc[...] = jnp.full(m_sc.shape, neg, F32)

    def score_tile(k0, w, bias):
        s = _dot(qaug, kaug_sc[:, pl.ds(k0, w)])
        if bias is not None:
            s = s + bias
        s_sc[:, pl.ds(k0, w)] = s
        mx = s[:, :LANES]
        for c in range(1, w // LANES):
            mx = jnp.maximum(mx, s[:, c * LANES:(c + 1) * LANES])
        m_sc[...] = jnp.maximum(m_sc[...], mx)

    def pass1(j, carry):
        score_tile(pl.multiple_of(j * 2 * tk, 2 * tk), 2 * tk, None)
        return carry

    lax.fori_loop(0, n_full // 2, pass1, 0)

    @pl.when(n_full % 2 == 1)
    def _():
        score_tile(pl.multiple_of((n_full - 1) * tk, tk), tk, None)

    score_tile(pl.multiple_of(n_full * tk, tk), tk, cbias_ref[par])
    m_b = jnp.broadcast_to(jnp.max(m_sc[...], axis=-1, keepdims=True), (rows, LANES))

    l_sc[...] = jnp.zeros(l_sc.shape, F32)
    acc_sc[...] = jnp.zeros(acc_sc.shape, F32)

    def pv_tile(k0, w, m_b):
        s = s_sc[:, pl.ds(k0, w)]
        ps = [jnp.exp(s[:, c * LANES:(c + 1) * LANES] - m_b) for c in range(w // LANES)]
        tot = ps[0]
        for pc in ps[1:]:
            tot = tot + pc
        l_sc[...] = l_sc[...] + tot
        p = jnp.concatenate(ps, axis=1).astype(BF16)
        acc_sc[...] = acc_sc[...] + _dot_nt(p, vsel_sc[:, pl.ds(k0, w)])

    def pass2(j, carry):
        pv_tile(pl.multiple_of(j * 2 * tk, 2 * tk), 2 * tk, m_b)
        return carry

    lax.fori_loop(0, (n_full + 1) // 2, pass2, 0)

    @pl.when(n_full % 2 == 0)
    def _():
        pv_tile(pl.multiple_of(n_full * tk, tk), tk, m_b)

    o_s = acc_sc[...] / jnp.sum(l_sc[...], axis=-1, keepdims=True)

    s = _dot(qaug_w, waug_sc[:, pl.ds(q0, wk)]) + wbias_ref[...]
    p = jnp.exp(s - jnp.max(s, axis=-1, keepdims=True))
    o_w = _dot_nt(p.astype(BF16), vwin_sc[:, pl.ds(q0, wk)]) / jnp.sum(p, axis=-1, keepdims=True)

    for r, ch in enumerate(_gated_heads(jax.nn.sigmoid(gate_ref[...]), (o_c, o_s, o_w), tq)):
        o_ref[:, r * LANES:(r + 1) * LANES] = ch


def _attn_constants(t, tq):
    rows = N_HEADS * tq
    t_loc = np.arange(rows)[:, None] % tq
    k = np.arange(2 * LANES)[None, :]
    cb = np.stack([np.where(k - par * tq <= t_loc, 0.0, NEG_INF) for par in range(2 * LANES // tq)])
    kr = np.arange(WINDOW + tq)[None, :]
    wb = np.where((kr - WINDOW <= t_loc) & (kr > t_loc), 0.0, NEG_INF)
    e = (np.arange(t)[None, :] // BLOCK == np.arange(LANES)[:, None]).astype(np.float32)
    return jnp.asarray(e, BF16), jnp.asarray(cb, F32), jnp.asarray(wb, F32)


def _nsa_prompt(q2d, kvt, wint, kcvc, gate2d, consts, nb_batch, t, tq):
    nb = t // BLOCK
    nq = t // tq
    nbp = kcvc.shape[1]
    rows = N_HEADS * tq
    wk = WINDOW + tq
    assert nbp == LANES and t % (2 * LANES) == 0
    e, cb, wb = consts
    kern = functools.partial(_nsa_prompt_kernel, tq=tq, nb=nb, n_sel=min(N_SELECT, nb))
    return pl.pallas_call(
        kern,
        grid=(nb_batch, nq),
        in_specs=[pl.BlockSpec((tq, D_NSA), lambda b, i: (b * nq + i, 0)),
                  pl.BlockSpec((None, 2 * KV_W, t), lambda b, i: (b, 1, 0)),
                  pl.BlockSpec((None, 2 * KV_W, t), lambda b, i: (b, 0, 0)),
                  pl.BlockSpec((None, nbp, 2 * KV_W), lambda b, i: (b, 0, 0)),
                  pl.BlockSpec((tq, LANES), lambda b, i: (b * nq + i, 0)),
                  pl.BlockSpec((LANES, t), lambda b, i: (0, 0)),
                  pl.BlockSpec((2 * LANES // tq, rows, 2 * LANES), lambda b, i: (0, 0, 0)),
                  pl.BlockSpec((rows, wk), lambda b, i: (0, 0))],
        out_specs=pl.BlockSpec((tq, D_NSA), lambda b, i: (b * nq + i, 0)),
        out_shape=jax.ShapeDtypeStruct((nb_batch * t, D_NSA), F32),
        scratch_shapes=[pltpu.VMEM((2 * KV_W, t), BF16), pltpu.VMEM((KV_W, t), BF16),
                        pltpu.VMEM((2 * KV_W, WINDOW + t), BF16), pltpu.VMEM((KV_W, WINDOW + t), BF16),
                        pltpu.VMEM((rows, t), F32), pltpu.VMEM((rows, LANES), F32),
                        pltpu.VMEM((rows, LANES), F32), pltpu.VMEM((rows, KV_W), F32),
                        pltpu.VMEM((N_KV, tq, LANES), F32)],
        compiler_params=_cparams(("arbitrary", "arbitrary")),
        name="nsa_prompt",
    )(q2d, kvt, wint, kcvc, gate2d, e, cb, wb)


def _page_copy(cache_ref, slabs, sem_ref, pt_ref, layer, b, p, slot, s0, page, s):
    return pltpu.make_async_copy(
        cache_ref.at[pt_ref[b, p], layer, s0 + s],
        slabs[s].at[slot, :, pl.ds(p * page, page)],
        sem_ref.at[slot])


def _gather_pages(cache_ref, slabs, sem_ref, pt_ref, layer, s0, n_pages, page):
    b = pl.program_id(0)
    nb_batch = pl.num_programs(0)
    slot = b % 2

    def copies(bb, sl):
        return [_page_copy(cache_ref, slabs, sem_ref, pt_ref, layer, bb, p, sl, s0, page, s)
                for p in range(n_pages) for s in range(2)]

    @pl.when(b == 0)
    def _():
        for cp in copies(0, 0):
            cp.start()

    @pl.when(b + 1 < nb_batch)
    def _():
        for cp in copies(b + 1, 1 - slot):
            cp.start()

    for cp in copies(b, slot):
        cp.wait()
    return slot


def _cmp_sample_kernel(pt_ref, cache_ref, new_ref, pe_ref, w1_ref, w2_ref, cos_ref, sa_ref, sb_ref,
                       o_ref, stagek_ref, stagev_ref, rowk_ref, rowv_ref, xcat_sc, sem_ref,
                       *, layer, n_pages, page, nbk, nbs):
    stages = (stagek_ref, stagev_ref)
    rowm = (rowk_ref, rowv_ref)
    i = pl.program_id(0)
    n_steps = pl.num_programs(0)

    def copies(step):
        return [_page_copy(cache_ref, stages, sem_ref, pt_ref, layer, step * nbs + bl, p, bl, 0, page, s)
                for bl in range(nbs) for p in range(n_pages) for s in range(2)]

    @pl.when(i == 0)
    def _():
        for cp in copies(0):
            cp.start()

    for cp in copies(i):
        cp.wait()

    bpp = page // BLOCK
    nb_past = n_pages * bpp
    unroll = next(k for k in (8, 4, 2, 1) if n_pages % k == 0)
    rows_b = nbk * CMP_PITCH
    for s in range(2):
        for bl in range(nbs):
            def xpose(it, carry):
                for k in range(unroll):
                    p = it * unroll + k
                    c0 = pl.multiple_of(p * page, page)
                    tile = stages[s][bl, :, pl.ds(c0, page)].T
                    for h in range(bpp):
                        r0 = pl.multiple_of(bl * rows_b + (p * bpp + h) * CMP_PITCH, 8)
                        rowm[s][pl.ds(r0, BLOCK), :] = tile[h * BLOCK:(h + 1) * BLOCK, :]
                return carry

            lax.fori_loop(0, n_pages // unroll, xpose, 0)
            tail0 = bl * rows_b + nb_past * CMP_PITCH
            tail = (bl + 1) * rows_b - tail0
            rowm[s][pl.ds(tail0, tail), :] = jnp.zeros((tail, KV_W), F32)
            rowm[s][pl.ds(tail0, 8), :] = new_ref[bl, :, s * KV_W:(s + 1) * KV_W]

    @pl.when(i + 1 < n_steps)
    def _():
        for cp in copies(i + 1):
            cp.start()

    def load_p(s, p):
        return rowm[s][pl.ds(p, nbs * nbk, stride=CMP_PITCH), :]

    tabs = [jnp.concatenate([t_ref[0:nbk, :]] * nbs, axis=0) for t_ref in (cos_ref, sa_ref, sb_ref)]
    res = _compress(load_p, nbs * nbk, pe_ref, w1_ref, w2_ref, *tabs, xcat_sc)
    o_ref[...] = jnp.zeros(o_ref.shape, F32)
    for bl in range(nbs):
        o_ref[bl, 0:nbk, :] = res[bl * nbk:(bl + 1) * nbk]


def _cmp_sample(page_table, cache5, newcmp, layer, nbp, pe2, w1bd, w2bd, ctabs, t_new):
    nb_batch, n_pages = page_table.shape
    page = cache5.shape[4]
    past = n_pages * page
    nbk = ((past + t_new + BLOCK - 1) // BLOCK + 7) // 8 * 8
    nbs = 2
    slab_rows = nbs * nbk * CMP_PITCH
    assert page % BLOCK == 0 and t_new <= 8 and nb_batch % nbs == 0
    full = lambda shape: pl.BlockSpec(shape, lambda b, pt: (0,) * len(shape))
    kern = functools.partial(_cmp_sample_kernel, layer=layer, n_pages=n_pages, page=page, nbk=nbk,
                             nbs=nbs)
    return pl.pallas_call(
        kern,
        grid_spec=pltpu.PrefetchScalarGridSpec(
            num_scalar_prefetch=1,
            grid=(nb_batch // nbs,),
            in_specs=[pl.BlockSpec(memory_space=pl.ANY),
                      pl.BlockSpec((nbs, 8, 2 * KV_W), lambda b, pt: (b, 0, 0)),
                      full((2, BLOCK, 1, KV_W)), full((2, BLOCK * KV_W, KV_W)),
                      full((2, KV_W, KV_W)), full((nbp, LANES)), full((nbp, LANES)),
                      full((nbp, LANES))],
            out_specs=pl.BlockSpec((nbs, nbp, 2 * KV_W), lambda b, pt: (b, 0, 0)),
            scratch_shapes=[pltpu.VMEM((nbs, KV_W, past), F32), pltpu.VMEM((nbs, KV_W, past), F32),
                            pltpu.VMEM((slab_rows, KV_W), F32), pltpu.VMEM((slab_rows, KV_W), F32),
                            pltpu.VMEM(((nbs * nbk + 15) // 16 * 16, BLOCK * KV_W), BF16),
                            pltpu.SemaphoreType.DMA((nbs,))]),
        out_shape=jax.ShapeDtypeStruct((nb_batch, nbp, 2 * KV_W), F32),
        compiler_params=_cparams(("arbitrary",)),
        name="cmp_sample",
    )(page_table, cache5, newcmp, pe2, w1bd, w2bd, *ctabs)


def _nsa_sample_kernel(pt_ref, cache_ref, q_ref, newt_ref, wnewt_ref, swint_ref, kcvc_ref, gate_ref,
                       o_ref, wout_ref, slabk_ref, slabv_ref, wslab_ref, sem_ref,
                       *, layer, n_pages, page, t_new, nb, n_sel):
    tq = 8
    past = n_pages * page
    rows = N_HEADS * tq
    slabs = (slabk_ref, slabv_ref)
    slot = _gather_pages(cache_ref, slabs, sem_ref, pt_ref, layer, 2, n_pages, page)
    nk = slabk_ref.shape[2]
    for s in range(2):
        slabs[s][slot, :, pl.ds(past, LANES)] = newt_ref[s]

    qs = _stack_queries(q_ref[...], tq)
    t_loc = lax.broadcasted_iota(jnp.int32, (rows, 1), 0) % tq
    tq_col = past + t_loc

    o_c, imps = _compressed_branch(qs, kcvc_ref[...], tq_col, tq, nb)
    qblk = (past + lax.broadcasted_iota(jnp.int32, (tq, 1), 0)) // BLOCK
    sels = [_select_blocks(imps[g], qblk, nb, n_sel).astype(BF16) for g in range(N_KV)]
    selstack = jnp.concatenate([sels[g] for g in range(N_KV) for _ in range(GROUP_SIZE)], axis=0)

    s = _dot(qs, slabk_ref[slot].astype(BF16))
    kpos = lax.broadcasted_iota(jnp.int32, (1, nk), 1)
    p, l = _masked_exp(s, _block_mask(selstack, 0, nk) & (kpos <= tq_col))
    o_s = _dot_nt(p.astype(BF16), slabv_ref[slot].astype(BF16)) * _safe_inv(l)

    wbuf = swint_ref.shape[2]
    wk = wslab_ref.shape[2]
    wslab_ref[:, :, 0:wbuf] = swint_ref[...]
    wslab_ref[:, :, wbuf:wk] = wnewt_ref[...]
    s = _dot(qs, wslab_ref[0].astype(BF16))
    kpos = past - wbuf + lax.broadcasted_iota(jnp.int32, (1, wk), 1)
    p, l = _masked_exp(s, (kpos <= tq_col) & (kpos > tq_col - WINDOW) & (kpos >= 0))
    o_w = _dot_nt(p.astype(BF16), wslab_ref[1].astype(BF16)) * _safe_inv(l)
    wout_ref[...] = wslab_ref[:, :, t_new:t_new + wbuf]

    for r, ch in enumerate(_gated_heads(jax.nn.sigmoid(gate_ref[...]), (o_c, o_s, o_w), tq)):
        o_ref[:, r * LANES:(r + 1) * LANES] = ch


def _nsa_sample(page_table, cache5, q3, newselt, wnewt, state_wint, kcvc, gate3, layer, t_new):
    nb_batch, n_pages = page_table.shape
    page = cache5.shape[4]
    past = n_pages * page
    nb = (past + t_new + BLOCK - 1) // BLOCK
    nk = past + LANES
    wbuf = state_wint.shape[4]
    wk = wbuf + LANES
    nbp = kcvc.shape[1]
    kern = functools.partial(_nsa_sample_kernel, layer=layer, n_pages=n_pages, page=page,
                             t_new=t_new, nb=nb, n_sel=min(N_SELECT, nb))
    per_b = lambda d1, d2: pl.BlockSpec((None, d1, d2), lambda b, pt: (b, 0, 0))
    per_b4 = lambda d1, d2, d3: pl.BlockSpec((None, d1, d2, d3), lambda b, pt: (b, 0, 0, 0))
    return pl.pallas_call(
        kern,
        grid_spec=pltpu.PrefetchScalarGridSpec(
            num_scalar_prefetch=1,
            grid=(nb_batch,),
            in_specs=[pl.BlockSpec(memory_space=pl.ANY),
                      per_b(8, D_NSA), per_b4(2, KV_W, LANES), per_b4(2, KV_W, LANES),
                      pl.BlockSpec((None, None, 2, KV_W, wbuf), lambda b, pt: (b, layer, 0, 0, 0)),
                      per_b(nbp, 2 * KV_W), per_b(8, LANES)],
            out_specs=[per_b(8, D_NSA), per_b4(2, KV_W, wbuf)],
            scratch_shapes=[pltpu.VMEM((2, KV_W, nk), F32), pltpu.VMEM((2, KV_W, nk), F32),
                            pltpu.VMEM((2, KV_W, wk), F32), pltpu.SemaphoreType.DMA((2,))]),
        out_shape=[jax.ShapeDtypeStruct((nb_batch, 8, D_NSA), F32),
                   jax.ShapeDtypeStruct((nb_batch, 2, KV_W, wbuf), F32)],
        compiler_params=_cparams(("arbitrary",)),
        name="nsa_sample",
    )(page_table, cache5, q3, newselt, wnewt, state_wint, kcvc, gate3)


def _s5_setup_kernel(are_ref, aim_ref, ldt_ref, bre_ref, bim_ref, cre_ref, cim_ref,
                     ckr_ref, cki_ref, bkr_ref, bki_ref, m_ref, pwr_ref, pwi_ref):
    nk = pwr_ref.shape[1]

    def body(g, carry):
        ar = are_ref[g]
        ai = aim_ref[g]
        dt = jnp.exp(ldt_ref[g])
        mag = jnp.exp(ar * dt)
        abr = mag * jnp.cos(ai * dt)
        abi = mag * jnp.sin(ai * dt)
        den = ar * ar + ai * ai
        nr = abr - 1.0
        e_re = (nr * ar + abi * ai) / den
        e_im = (abi * ar - nr * ai) / den
        bre = bre_ref[g]
        bim = bim_ref[g]
        bbr = e_re * bre - e_im * bim
        bbi = e_re * bim + e_im * bre
        kk = lax.broadcasted_iota(jnp.int32, (nk, S5_STATE), 0).astype(F32)
        pmag = jnp.exp(kk * (ar * dt))
        pwr = pmag * jnp.cos(kk * (ai * dt))
        pwi = pmag * jnp.sin(kk * (ai * dt))
        pwr_ref[g] = pwr
        pwi_ref[g] = pwi
        cre = cre_ref[g]
        cim = cim_ref[g]
        for k in range(S5_CHUNK + 1):
            wr = pwr[k:k + 1, :]
            wi = pwi[k:k + 1, :]
            ckr_ref[g, k * S5_GROUP:(k + 1) * S5_GROUP, :] = cre * wr - cim * wi
            cki_ref[g, k * S5_GROUP:(k + 1) * S5_GROUP, :] = -(cre * wi + cim * wr)
            if k < S5_CHUNK:
                bkr_ref[g, k * S5_GROUP:(k + 1) * S5_GROUP, :] = bbr * wr - bbi * wi
                bki_ref[g, k * S5_GROUP:(k + 1) * S5_GROUP, :] = bbr * wi + bbi * wr
        nl = S5_CHUNK * S5_GROUP
        hp = lax.Precision.HIGHEST
        m_ref[g] = (lax.dot_general(ckr_ref[g, 0:nl, :], bbr, (((1,), (1,)), ((), ())),
                                    precision=hp, preferred_element_type=F32)
                    + lax.dot_general(cki_ref[g, 0:nl, :], bbi, (((1,), (1,)), ((), ())),
                                      precision=hp, preferred_element_type=F32))
        return carry

    lax.fori_loop(0, S5_GROUPS, body, 0)


def _s5_setup(a_re, a_im, log_dt, b_re, b_im, c_re, c_im):
    g, p, c = S5_GROUPS, S5_STATE, S5_GROUP
    nk = 24
    outs = [(DEPTH, g, (S5_CHUNK + 1) * c, p), (DEPTH, g, (S5_CHUNK + 1) * c, p),
            (DEPTH, g, S5_CHUNK * c, p), (DEPTH, g, S5_CHUNK * c, p),
            (DEPTH, g, S5_CHUNK * c, c), (DEPTH, g, nk, p), (DEPTH, g, nk, p)]
    lay = lambda s: pl.BlockSpec((None,) + s[1:], lambda l: (l,) + (0,) * (len(s) - 1))
    ins = [a_re.reshape(DEPTH, g, 1, p), a_im.reshape(DEPTH, g, 1, p),
           jnp.broadcast_to(log_dt[:, :, None, None], (DEPTH, g, 1, p)),
           b_re.transpose(0, 1, 3, 2), b_im.transpose(0, 1, 3, 2), c_re, c_im]
    return pl.pallas_call(
        _s5_setup_kernel,
        grid=(DEPTH,),
        in_specs=[lay(x.shape) for x in ins],
        out_specs=[lay(s) for s in outs],
        out_shape=[jax.ShapeDtypeStruct(s, F32) for s in outs],
        compiler_params=_cparams(("arbitrary",)),
        name="s5_setup",
    )(*ins)


def _s5_chunk_kernel(u_ref, h0_ref, kmat_ref, smat_ref, ymat_ref, a1_ref, a2_ref,
                     y_ref, hl_ref, s_sc, hp_sc, *, gb, nj, nbt):
    for gi in range(gb):
        s_sc[gi] = _dot(u_ref[gi], smat_ref[gi])

    def step(j, hs):
        r0 = pl.multiple_of(j * nbt, nbt)
        new = []
        for gi in range(gb):
            h = hs[gi]
            hp_sc[gi, pl.ds(r0, nbt), :] = h
            new.append(a1_ref[gi] * h + a2_ref[gi] * pltpu.roll(h, S5_STATE, 1)
                       + s_sc[gi, pl.ds(r0, nbt), :])
        return tuple(new)

    hs = lax.fori_loop(0, nj, step, tuple(h0_ref[gi] for gi in range(gb)))
    for gi in range(gb):
        hl_ref[gi] = hs[gi]
        y_ref[gi] = _dot(u_ref[gi], kmat_ref[gi]) + _dot(hp_sc[gi].astype(BF16), ymat_ref[gi])


def _s5_chunk(u_g, h0_g, kmat, smat, ymat, a1, a2, nbt):
    g, rows, w = u_g.shape
    nj = rows // nbt
    gb = 8
    st = 2 * S5_STATE
    blk = lambda d1, d2: pl.BlockSpec((gb, d1, d2), lambda i: (i, 0, 0))
    kern = functools.partial(_s5_chunk_kernel, gb=gb, nj=nj, nbt=nbt)
    return pl.pallas_call(
        kern,
        grid=(g // gb,),
        in_specs=[blk(rows, w), blk(nbt, st), blk(w, w), blk(w, st), blk(st, w), blk(1, st), blk(1, st)],
        out_specs=[blk(rows, w), blk(nbt, st)],
        out_shape=[jax.ShapeDtypeStruct((g, rows, w), F32), jax.ShapeDtypeStruct((g, nbt, st), F32)],
        scratch_shapes=[pltpu.VMEM((gb, rows, st), F32), pltpu.VMEM((gb, rows, st), F32)],
        compiler_params=_cparams(("arbitrary",)),
        name="s5_chunk",
    )(u_g, h0_g, kmat, smat, ymat, a1, a2)


def _s5_rows_kernel(u_ref, w_ref, sw_ref, yw_ref, a1_ref, a2_ref, y_ref, hl_ref,
                    xr_sc, sg_sc, sgs_sc, hp_sc, hcat_sc, *, nbl, nj):
    L = S5_CHUNK
    mr = nbl * nj
    ng = LANES // S5_GROUP
    st = 2 * S5_STATE
    for s in range(L):
        xr_sc[:, (L - 1 - s) * LANES:(L - s) * LANES] = u_ref[pl.ds(s, mr, stride=L), :].astype(BF16)
    sall = _dot(xr_sc[...], sw_ref[...])
    for g in range(ng):
        s_g = sall[:, g * st:(g + 1) * st]
        sg_sc[g] = s_g
        sgs_sc[g] = pltpu.roll(s_g, S5_STATE, 1)

    def step(j, carry):
        hs, hss = carry
        new, news = [], []
        for g in range(ng):
            h, hsw = hs[g], hss[g]
            a1, a2 = a1_ref[g], a2_ref[g]
            hp_sc.at[g][pl.ds(j, nbl, stride=nj), :] = h
            new.append(a1 * h + a2 * hsw + sg_sc.at[g][pl.ds(j, nbl, stride=nj), :])
            news.append(a1 * hsw - a2 * h + sgs_sc.at[g][pl.ds(j, nbl, stride=nj), :])
        return tuple(new), tuple(news)

    zero = tuple(jnp.zeros((nbl, st), F32) for _ in range(ng))
    hs, _ = lax.fori_loop(0, nj, step, (zero, zero))
    for g in range(ng):
        hl_ref[g] = hs[g]
        hcat_sc[:, g * st:(g + 1) * st] = hp_sc[g].astype(BF16)
    for tp in range(L // 2):
        t1 = 2 * tp + 1
        y = (_dot(xr_sc[:, (L - 1 - t1) * LANES:], w_ref[0:(t1 + 1) * LANES, :])
             + _dot(hcat_sc[...], yw_ref[:, (t1 - 1) * LANES:(t1 + 1) * LANES]))
        y_ref[pl.ds(t1 - 1, mr, stride=L), :] = y[:, :LANES]
        y_ref[pl.ds(t1, mr, stride=L), :] = y[:, LANES:]


def _s5_rows(u4, w, sw, yw, a1, a2, nb_batch, t):
    nch, rows, _ = u4.shape
    nsplit = 2
    nbl = nb_batch // nsplit
    nj = t // S5_CHUNK
    rb = nbl * t
    mr = nbl * nj
    ng = LANES // S5_GROUP
    st = 2 * S5_STATE
    kern = functools.partial(_s5_rows_kernel, nbl=nbl, nj=nj)
    per_c = lambda *s: pl.BlockSpec((None,) + s, lambda c, h: (c,) + (0,) * len(s))
    return pl.pallas_call(
        kern,
        grid=(nch, nsplit),
        in_specs=[pl.BlockSpec((None, rb, LANES), lambda c, h: (c, h, 0)),
                  per_c(S5_CHUNK * LANES, 2 * LANES), per_c(S5_CHUNK * LANES, ng * st),
                  per_c(ng * st, S5_CHUNK * LANES), per_c(ng, 1, st), per_c(ng, 1, st)],
        out_specs=[pl.BlockSpec((None, rb, LANES), lambda c, h: (c, h, 0)),
                   pl.BlockSpec((None, ng, None, nbl, st), lambda c, h: (c, 0, h, 0, 0))],
        out_shape=[jax.ShapeDtypeStruct((nch, rows, LANES), F32),
                   jax.ShapeDtypeStruct((nch, ng, nsplit, nbl, st), F32)],
        scratch_shapes=[pltpu.VMEM((mr, S5_CHUNK * LANES), BF16), pltpu.VMEM((ng, mr, st), F32),
                        pltpu.VMEM((ng, mr, st), F32), pltpu.VMEM((ng, mr, st), F32),
                        pltpu.VMEM((mr, ng * st), BF16)],
        compiler_params=_cparams(("arbitrary", "arbitrary")),
        name="s5_rows",
    )(u4, w, sw, yw, a1, a2)


def _post_kernel(x_ref, gmod_ref, yssm_ref, u_ref, zs5_ref, o_ref, znsa_ref, merge_ref,
                 d_ref, gluw_ref, glub_ref, ws5_ref, wnsa_ref, wo_ref, fg_ref, xo_ref, *, final):
    y = jnp.concatenate([yssm_ref[ch] + d_ref[ch] * u_ref[ch] for ch in range(D_S5 // LANES)], axis=1)
    y = 0.5 * y * (1.0 + jnp.tanh(math.sqrt(2.0 / math.pi) * (y + 0.044715 * (y * y * y))))
    y = y * jax.nn.sigmoid(_dot(y.astype(BF16), gluw_ref[...]) + glub_ref[...])
    y = y * _silu(zs5_ref[...].astype(F32))
    b_s5 = _dot(y.astype(BF16), ws5_ref[...])
    b_nsa = _dot((o_ref[...] * _silu(znsa_ref[...].astype(F32))).astype(BF16), wnsa_ref[...])
    m = jax.nn.sigmoid(merge_ref[...].astype(F32))
    mix = m[:, :D_MODEL] * b_s5 + m[:, D_MODEL:] * b_nsa
    x = x_ref[...] + gmod_ref[...] * _dot(mix.astype(BF16), wo_ref[...])
    if final:
        x = x * lax.rsqrt(jnp.mean(x * x, axis=-1, keepdims=True) + RMS_EPS) * fg_ref[...]
    xo_ref[...] = x


def _post(x2d, gmod, yssm, u, zs5, o, znsa, merge, d, gluw, glub, ws5, wnsa, wo, fg,
          *, tm, tiles_per_b, per_row_mod, final):
    rows = x2d.shape[0]
    if per_row_mod:
        mod_spec = pl.BlockSpec((tm, D_MODEL), lambda i: (i, 0))
    else:
        mod_spec = pl.BlockSpec((None, 1, D_MODEL), lambda i: (i // tiles_per_b, 0, 0))
    row = lambda w: pl.BlockSpec((tm, w), lambda i: (i, 0))
    full = lambda a, b: pl.BlockSpec((a, b), lambda i: (0, 0))
    nch = D_S5 // LANES
    chunked = pl.BlockSpec((nch, tm, LANES), lambda i: (0, i, 0))
    return pl.pallas_call(
        functools.partial(_post_kernel, final=final),
        grid=(rows // tm,),
        in_specs=[row(D_MODEL), mod_spec, chunked, chunked, row(512), row(512), row(512), row(2048),
                  pl.BlockSpec((nch, 1, LANES), lambda i: (0, 0, 0)),
                  full(512, 512), full(1, 512), full(512, D_MODEL), full(512, D_MODEL),
                  full(D_MODEL, D_MODEL), full(1, D_MODEL)],
        out_specs=row(D_MODEL),
        out_shape=jax.ShapeDtypeStruct((rows, D_MODEL), F32),
        compiler_params=_cparams(("arbitrary",)),
        name="post",
    )(x2d, gmod, yssm, u, zs5, o, znsa, merge, d, gluw, glub, ws5, wnsa, wo, fg)


def _head_perm():
    idx = [HEAD_DIM * (GROUP_SIZE * g + r) + d
           for r in range(GROUP_SIZE) for g in range(N_KV) for d in range(HEAD_DIM)]
    return np.asarray(idx, np.int32)


def _rope_tables(pos, width=LANES):
    inv = ROPE_THETA ** (-jnp.arange(ROT_HALF, dtype=F32) / ROT_HALF)
    ang = pos.astype(F32)[:, None] * inv[None, :]
    cos, sin = jnp.cos(ang), jnp.sin(ang)
    n = pos.shape[0]
    one = jnp.ones((n, HEAD_DIM - 2 * ROT_HALF), F32)
    zero8 = jnp.zeros((n, ROT_HALF), F32)
    zero = jnp.zeros((n, HEAD_DIM - 2 * ROT_HALF), F32)
    c = jnp.concatenate([cos, cos, one], axis=1)
    sa = jnp.concatenate([-sin, zero8, zero], axis=1)
    sb = jnp.concatenate([zero8, sin, zero], axis=1)
    rep = width // HEAD_DIM
    return tuple(jnp.tile(t, (1, rep)) for t in (c, sa, sb))


def _blockdiag2(w):
    z = jnp.zeros_like(w)
    return jnp.concatenate([jnp.concatenate([w, z], -1), jnp.concatenate([z, w], -1)], -2)


def _s5_matrices(tabs, l, t_eff):
    ckr, cki, bkr, bki, m, pwr, pwi = [t[l] for t in tabs]
    g, c, L = S5_GROUPS, S5_GROUP, S5_CHUNK
    mk = m.reshape(g, L, c, c)
    s_idx = np.arange(L)[:, None]
    t_idx = np.arange(L)[None, :]
    lag = np.clip(t_idx - s_idx, 0, L - 1)
    kfull = mk[:, lag]
    kfull = jnp.where((t_idx >= s_idx)[None, :, :, None, None], kfull, 0.0)
    kmat = kfull.transpose(0, 1, 4, 2, 3).reshape(g, L * c, L * c).astype(BF16)
    ck = jnp.concatenate([ckr, cki], axis=-1)
    ymat = ck[:, c:, :].transpose(0, 2, 1).astype(BF16)
    bk = jnp.concatenate([bkr, bki], axis=-1).reshape(g, L, c, 2 * S5_STATE)
    sm = bk[:, t_eff - 1::-1] if t_eff == L else bk[:, np.arange(t_eff - 1, -1, -1)]
    if t_eff < L:
        sm = jnp.concatenate([sm, jnp.zeros((g, L - t_eff, c, 2 * S5_STATE), F32)], axis=1)
    smat = sm.reshape(g, L * c, 2 * S5_STATE).astype(BF16)
    ar = pwr[:, t_eff][:, None, :]
    ai = pwi[:, t_eff][:, None, :]
    a1 = jnp.concatenate([ar, ar], axis=-1)
    a2 = jnp.concatenate([-ai, ai], axis=-1)
    return kmat, smat, ymat, a1, a2


def _s5_row_weights(tabs):
    ckr, cki, bkr, bki, m, pwr, pwi = tabs
    c, L, st = S5_GROUP, S5_CHUNK, 2 * S5_STATE
    ng = LANES // c
    nch = S5_GROUPS // ng
    eye = jnp.eye(ng, dtype=BF16)
    mk = m.astype(BF16).reshape(DEPTH, nch, ng, L, c, c)
    w = jnp.einsum('dhgloi,gq->dhlgiqo', mk, eye).reshape(DEPTH, nch, L * LANES, LANES)
    w_shift = jnp.concatenate([jnp.zeros_like(w[:, :, :LANES]), w[:, :, :-LANES]], axis=2)
    w2 = jnp.concatenate([w_shift, w], axis=-1)
    bk = jnp.concatenate([bkr, bki], axis=-1).astype(BF16).reshape(DEPTH, nch, ng, L, c, st)
    sw = jnp.einsum('dhgkcp,gq->dhkgcqp', bk, eye).reshape(DEPTH, nch, L * LANES, ng * st)
    ck = jnp.concatenate([ckr, cki], axis=-1).astype(BF16).reshape(DEPTH, nch, ng, L + 1, c, st)[:, :, :, 1:]
    yw = jnp.einsum('dhgtcp,gq->dhgptqc', ck, eye).reshape(DEPTH, nch, ng * st, L * LANES)
    ar = pwr[:, :, L].reshape(DEPTH, nch, ng, 1, S5_STATE)
    ai = pwi[:, :, L].reshape(DEPTH, nch, ng, 1, S5_STATE)
    a1 = jnp.concatenate([ar, ar], axis=-1)
    a2 = jnp.concatenate([-ai, ai], axis=-1)
    return w2, sw, yw, a1, a2


def _all_layer_weights(w_in, cmp_pe, cmp_w1, cmp_w2, s5_glu_w, w_s5_out, w_nsa_out, w_o, perm):
    gate_w = jnp.pad(w_in[:, :, 2304:2328], ((0, 0), (0, 0), (0, LANES - 3 * N_HEADS)))
    wp = jnp.concatenate([w_in[:, :, :1024], w_in[:, :, 1024:1536][:, :, perm], w_in[:, :, 1536:2304],
                          w_in[:, :, 2328:2840][:, :, perm], w_in[:, :, 2840:], gate_w], axis=2).astype(BF16)
    pe2 = jnp.concatenate([cmp_pe, cmp_pe], axis=-1)[:, :, :, None, :]
    w1bd = _blockdiag2(cmp_w1.astype(BF16).reshape(DEPTH, 2, BLOCK, HEAD_DIM, HEAD_DIM))
    w1bd = w1bd.reshape(DEPTH, 2, BLOCK * KV_W, KV_W)
    w2bd = _blockdiag2(cmp_w2.astype(BF16))
    wt = w_in[:, :, 1536:2304].transpose(0, 2, 1).astype(BF16)
    return dict(wp=wp, wt=wt, pe2=pe2, w1bd=w1bd, w2bd=w2bd, gluw=s5_glu_w.astype(BF16),
                ws5=w_s5_out.astype(BF16), wnsa=w_nsa_out[:, perm, :].astype(BF16),
                wo=w_o.astype(BF16))


def _to_groups(u2d, nbt, nj):
    u5 = u2d.reshape(nbt, nj, S5_CHUNK, S5_GROUPS, S5_GROUP)
    return u5.transpose(3, 1, 0, 2, 4).reshape(S5_GROUPS, nj * nbt, S5_CHUNK * S5_GROUP)


def _from_groups(y_g, nbt, nj):
    y5 = y_g.reshape(S5_GROUPS, nj, nbt, S5_CHUNK, S5_GROUP)
    return y5.transpose(2, 1, 3, 0, 4).reshape(nbt * nj * S5_CHUNK, D_S5)


def _state_to_groups(h):
    return h.transpose(2, 0, 1, 3).reshape(S5_GROUPS, h.shape[0], 2 * S5_STATE)


def _state_from_groups(hg):
    g, b, _ = hg.shape
    return hg.reshape(g, b, 2, S5_STATE).transpose(1, 2, 0, 3)


def kernel(x_prompt, x_sample, c_prompt, c_sample, cache_kv, page_table, state_win, state_ssm, ada_w, ada_b, norm_g, w_in, s5_a_re, s5_a_im, s5_log_dt, s5_b_re, s5_b_im, s5_c_re, s5_c_im, s5_d, s5_glu_w, s5_glu_b, cmp_pe, cmp_w1, cmp_w2, w_s5_out, w_nsa_out, w_o, final_g):
    bp, tp, _ = x_prompt.shape
    bs, ts, _ = x_sample.shape
    n_pool, _, page = cache_kv.shape[:3]
    n_pages = page_table.shape[1]
    past = n_pages * page
    wbuf = state_win.shape[2]
    assert tp % S5_CHUNK == 0 and tp % 128 == 0 and ts <= 8 and bs % 8 == 0 and bp % 8 == 0

    perm = _head_perm()
    mod = _ada_mod(jnp.concatenate([c_prompt, c_sample], axis=0), ada_w, ada_b)
    s5tabs = _s5_setup(s5_a_re, s5_a_im, s5_log_dt, s5_b_re, s5_b_im, s5_c_re, s5_c_im)

    tabs_p = _rope_tables(jnp.arange(tp))
    rs = bs * ts
    tabs_s = _rope_tables(jnp.tile(past + jnp.arange(ts), bs))
    nb_p = tp // BLOCK
    nbp_p = (nb_p + LANES - 1) // LANES * LANES
    ctabs_p = _rope_tables(jnp.arange(nbp_p) * BLOCK + (BLOCK - 1))
    nb_s = (past + ts + BLOCK - 1) // BLOCK
    nbp_s = ((nb_s + 7) // 8 * 8 + LANES - 1) // LANES * LANES
    ctabs_s = _rope_tables(jnp.arange(nbp_s) * BLOCK + (BLOCK - 1))

    cache5 = cache_kv.transpose(0, 1, 3, 4, 5, 2).reshape(n_pool, DEPTH, 4, KV_W, page)
    state_wint = state_win.transpose(0, 1, 3, 4, 5, 2).reshape(bs, DEPTH, 2, KV_W, wbuf)
    tabs_pt = tuple(tb.T for tb in tabs_p)
    tm_p = 512 if tp % 512 == 0 else 256
    tq_p = 256
    nj_p = tp // S5_CHUNK
    consts = _attn_constants(tp, tq_p)
    keep = min(WINDOW, tp)

    xp = x_prompt.reshape(bp * tp, D_MODEL)
    xs = x_sample.reshape(rs, D_MODEL)
    kv_p, kv_s, win_p, win_s, ssm_p, ssm_s = [], [], [], [], [], []

    def pad_rows(a3):
        return jnp.pad(a3, ((0, 0), (0, 8 - ts), (0, 0)))

    def to_chunks(a2):
        return a2.reshape(a2.shape[0], D_S5 // LANES, LANES).transpose(1, 0, 2)

    def new_tiles(a3):
        a4 = a3.reshape(bs, ts, 2, KV_W).transpose(0, 2, 3, 1)
        return jnp.pad(a4, ((0, 0), (0, 0), (0, 0), (0, LANES - ts)))

    lw_all = _all_layer_weights(w_in, cmp_pe, cmp_w1, cmp_w2, s5_glu_w, w_s5_out, w_nsa_out, w_o, perm)
    s5w_all = _s5_row_weights(s5tabs)

    for l in range(DEPTH):
        lw = {name: val[l] for name, val in lw_all.items()}
        g_row = norm_g[l][None, :]
        d_row = s5_d[l].reshape(D_S5 // LANES, 1, LANES)
        glub = s5_glu_b[l][None, :]
        mp, ms = mod[l, :bp], mod[l, bp:]
        post_w = (d_row, lw['gluw'], glub, lw['ws5'], lw['wnsa'], lw['wo'], final_g[None, :])
        last = l == DEPTH - 1

        shift, scale, gmod = [mp[:, k * D_MODEL:(k + 1) * D_MODEL][:, None, :] for k in range(3)]
        u, zs5, q, kvt, wint, cmp, znsa, merge, gate = _inproj_prompt(
            xp, shift, scale, g_row, lw['wp'], lw['wt'], tabs_p, tabs_pt, tm=tm_p, nb_batch=bp, t=tp)
        kcvc = _cmp_prompt(cmp, bp, tp, nbp_p, lw['pe2'], lw['w1bd'], lw['w2bd'], ctabs_p)
        o = _nsa_prompt(q, kvt, wint, kcvc, gate, consts, bp, tp, tq_p)
        yssm, hl = _s5_rows(u, *[a[l] for a in s5w_all], bp, tp)
        xp = _post(xp, gmod, yssm, u, zs5, o, znsa, merge, *post_w,
                   tm=tm_p, tiles_per_b=tp // tm_p, per_row_mod=False, final=last)
        kv_p.append(kvt)
        win_p.append(wint[:, :, tp - keep:])
        ssm_p.append(_state_from_groups(hl.reshape(S5_GROUPS, bp, 2 * S5_STATE)))

        shift, scale, gmod = [jnp.repeat(ms[:, k * D_MODEL:(k + 1) * D_MODEL], ts, axis=0) for k in range(3)]
        u, zs5, q, kv, win, znsa, merge, gate = _inproj(
            xs, shift, scale, g_row, lw['wp'], tabs_s, tm=rs, tiles_per_b=1, per_row_mod=True)
        kv3 = kv.reshape(bs, ts, 4 * KV_W)
        kcvc = _cmp_sample(page_table, cache5, pad_rows(kv3[:, :, :2 * KV_W]), l, nbp_s,
                           lw['pe2'], lw['w1bd'], lw['w2bd'], ctabs_s, ts)
        o8, wst = _nsa_sample(page_table, cache5, pad_rows(q.reshape(bs, ts, D_NSA)),
                              new_tiles(kv3[:, :, 2 * KV_W:]), new_tiles(win.reshape(bs, ts, 2 * KV_W)),
                              state_wint, kcvc, pad_rows(gate.reshape(bs, ts, LANES)), l, ts)
        o = o8[:, :ts].reshape(rs, D_NSA)
        mats = _s5_matrices(s5tabs, l, ts)
        u_pad = jnp.pad(u.reshape(bs, ts, D_S5), ((0, 0), (0, S5_CHUNK - ts), (0, 0)))
        y_g, hl = _s5_chunk(_to_groups(u_pad.reshape(bs * S5_CHUNK, D_S5), bs, 1).astype(BF16),
                            _state_to_groups(state_ssm[:, l]), *mats, nbt=bs)
        yssm = _from_groups(y_g, bs, 1).reshape(bs, S5_CHUNK, D_S5)[:, :ts].reshape(rs, D_S5)
        xs = _post(xs, gmod, to_chunks(yssm), to_chunks(u), zs5, o, znsa, merge, *post_w,
                   tm=rs, tiles_per_b=1, per_row_mod=True, final=last)
        kv_s.append(kv.reshape(bs, ts, 4, N_KV, HEAD_DIM))
        win_s.append(wst)
        ssm_s.append(_state_from_groups(hl))

    def from_t(parts, n_streams):
        a = jnp.stack(parts, axis=1)
        nbt, keys = a.shape[0], a.shape[-1]
        return a.reshape(nbt, DEPTH, n_streams, N_KV, HEAD_DIM, keys).transpose(0, 1, 5, 2, 3, 4)

    y_prompt = xp.reshape(bp, tp, D_MODEL)
    y_sample = xs.reshape(bs, ts, D_MODEL)
    return (y_prompt, y_sample, from_t(kv_p, 4), jnp.stack(kv_s, axis=1),
            from_t(win_p, 2), from_t(win_s, 2),
            jnp.stack(ssm_p, axis=1), jnp.stack(ssm_s, axis=1))
```

```python
import functools
import math

import numpy as np
import jax
import jax.numpy as jnp
from jax import lax
from jax.experimental import pallas as pl
from jax.experimental.pallas import tpu as pltpu

F32 = jnp.float32
BF16 = jnp.bfloat16

D_MODEL = 1024
DEPTH = 4
D_S5 = 512
S5_GROUP = 16
S5_GROUPS = 32
S5_STATE = 64
D_NSA = 512
HEAD_DIM = 64
N_HEADS = 8
N_KV = 2
GROUP_SIZE = 4
BLOCK = 64
N_SELECT = 16
WINDOW = 512
ROT_HALF = 8
ROPE_THETA = 500000.0
RMS_EPS = 1e-6
NEG_INF = -1e30
FORCED_SCORE = 1e4

LANES = 128
S5_CHUNK = 16
KV_W = N_KV * HEAD_DIM
N_PROJ = 4992
CMP_PITCH = 72
VMEM_LIMIT = 56 * 1024 * 1024


def _dot(a, b):
    return jnp.dot(a, b, preferred_element_type=F32)


def _dot_nt(a, b):
    return lax.dot_general(a, b, (((1,), (1,)), ((), ())), preferred_element_type=F32)


def _silu(x):
    return x * jax.nn.sigmoid(x)


def _rope128(x, c, sa, sb):
    return x * c + pltpu.roll(x, LANES - ROT_HALF, 1) * sa + pltpu.roll(x, ROT_HALF, 1) * sb


def _masked_exp(s, mask):
    s = jnp.where(mask, s, NEG_INF)
    m = jnp.max(s, axis=-1, keepdims=True)
    p = jnp.where(mask, jnp.exp(s - m), 0.0)
    return p, jnp.sum(p, axis=-1, keepdims=True)


def _safe_inv(l):
    return jnp.where(l > 0.0, 1.0 / l, 0.0)


def _cparams(sem):
    return pltpu.CompilerParams(dimension_semantics=sem, vmem_limit_bytes=VMEM_LIMIT)


def _ada_kernel(c_ref, w_ref, b_ref, o_ref):
    c = c_ref[...]
    o_ref[...] = _dot(_silu(c).astype(BF16), w_ref[...].astype(BF16)) + b_ref[...]


def _ada_mod(c_all, ada_w, ada_b):
    nc = c_all.shape[0]
    tn = 1024
    return pl.pallas_call(
        _ada_kernel,
        grid=(DEPTH, 3 * D_MODEL // tn),
        in_specs=[pl.BlockSpec((nc, D_MODEL), lambda l, n: (0, 0)),
                  pl.BlockSpec((None, D_MODEL, tn), lambda l, n: (l, 0, n)),
                  pl.BlockSpec((None, 1, tn), lambda l, n: (l, 0, n))],
        out_specs=pl.BlockSpec((None, nc, tn), lambda l, n: (l, 0, n)),
        out_shape=jax.ShapeDtypeStruct((DEPTH, nc, 3 * D_MODEL), F32),
        compiler_params=_cparams(("arbitrary", "arbitrary")),
        name="ada_mod",
    )(c_all, ada_w, ada_b.reshape(DEPTH, 1, 3 * D_MODEL))


def _inproj_kernel(x_ref, shift_ref, scale_ref, g_ref, w_ref, cos_ref, sa_ref, sb_ref,
                   u_ref, zs5_ref, q_ref, kv_ref, win_ref, znsa_ref, merge_ref, gate_ref):
    x = x_ref[...]
    h = x * lax.rsqrt(jnp.mean(x * x, axis=-1, keepdims=True) + RMS_EPS) * g_ref[...]
    h = h * (1.0 + scale_ref[...]) + shift_ref[...]
    hb = h.astype(BF16)

    def mm(lo, hi):
        return _dot(hb, w_ref[:, lo:hi])

    c, sa, sb = cos_ref[...], sa_ref[...], sb_ref[...]
    u_ref[...] = mm(0, 512)
    zs5_ref[...] = mm(512, 1024).astype(BF16)
    for r in range(GROUP_SIZE):
        lo = 1024 + r * LANES
        q = _rope128(mm(lo, lo + LANES), c, sa, sb) * (HEAD_DIM ** -0.5)
        q_ref[:, r * LANES:(r + 1) * LANES] = q.astype(BF16)
    kv_ref[:, 0:256] = mm(1536, 1792)
    kv_ref[:, 256:384] = _rope128(mm(1792, 1920), c, sa, sb)
    kv_ref[:, 384:512] = mm(1920, 2048)
    win_ref[:, 0:128] = _rope128(mm(2048, 2176), c, sa, sb)
    win_ref[:, 128:256] = mm(2176, 2304)
    znsa_ref[...] = mm(2304, 2816).astype(BF16)
    merge_ref[...] = mm(2816, 4864).astype(BF16)
    gate_ref[...] = mm(4864, 4992)


def _inproj(x2d, shift, scale, g, wp, tabs, *, tm, tiles_per_b, per_row_mod):
    rows = x2d.shape[0]
    nt = rows // tm
    n_tab = tabs[0].shape[0] // tm
    if per_row_mod:
        mod_spec = pl.BlockSpec((tm, D_MODEL), lambda i: (i, 0))
    else:
        mod_spec = pl.BlockSpec((None, 1, D_MODEL), lambda i: (i // tiles_per_b, 0, 0))
    tab_spec = pl.BlockSpec((tm, LANES), lambda i: (i % n_tab, 0))

    def row_spec(w):
        return pl.BlockSpec((tm, w), lambda i: (i, 0))

    widths = (512, 512, 512, 512, 256, 512, 2048, 128)
    dtypes = (F32, BF16, BF16, F32, F32, BF16, BF16, F32)
    return pl.pallas_call(
        _inproj_kernel,
        grid=(nt,),
        in_specs=[row_spec(D_MODEL), mod_spec, mod_spec,
                  pl.BlockSpec((1, D_MODEL), lambda i: (0, 0)),
                  pl.BlockSpec((D_MODEL, N_PROJ), lambda i: (0, 0)),
                  tab_spec, tab_spec, tab_spec],
        out_specs=[row_spec(w) for w in widths],
        out_shape=[jax.ShapeDtypeStruct((rows, w), d) for w, d in zip(widths, dtypes)],
        compiler_params=_cparams(("arbitrary",)),
        name="inproj",
    )(x2d, shift, scale, g, wp, *tabs)


def _rope128_t(x, c, sa, sb):
    return x * c + pltpu.roll(x, KV_W - ROT_HALF, 0) * sa + pltpu.roll(x, ROT_HALF, 0) * sb


def _inproj_prompt_kernel(x_ref, shift_ref, scale_ref, g_ref, w_ref, wt_ref, cos_ref, sa_ref, sb_ref,
                          cost_ref, sat_ref, sbt_ref,
                          u_ref, zs5_ref, q_ref, kvt_ref, wint_ref, cmp_ref, znsa_ref, merge_ref, gate_ref):
    x = x_ref[...]
    h = x * lax.rsqrt(jnp.mean(x * x, axis=-1, keepdims=True) + RMS_EPS) * g_ref[...]
    h = h * (1.0 + scale_ref[...]) + shift_ref[...]
    hb = h.astype(BF16)

    def mm(lo, hi):
        return _dot(hb, w_ref[:, lo:hi])

    c, sa, sb = cos_ref[...], sa_ref[...], sb_ref[...]
    for pair in range(D_S5 // (2 * LANES)):
        up = mm(pair * 2 * LANES, (pair + 1) * 2 * LANES)
        u_ref[2 * pair] = up[:, :LANES]
        u_ref[2 * pair + 1] = up[:, LANES:]
    zs5_ref[...] = mm(512, 1024).astype(BF16)
    for pair in range(GROUP_SIZE // 2):
        lo = 1024 + pair * 2 * LANES
        qp = mm(lo, lo + 2 * LANES)
        for k in range(2):
            q = _rope128(qp[:, k * LANES:(k + 1) * LANES], c, sa, sb) * (HEAD_DIM ** -0.5)
            q_ref[:, (2 * pair + k) * LANES:(2 * pair + k + 1) * LANES] = q.astype(BF16)
    cmp_ref[...] = mm(1536, 1792)
    znsa_ref[...] = mm(2304, 2816).astype(BF16)
    merge_ref[...] = mm(2816, 4864).astype(BF16)
    gate_ref[...] = mm(4864, 4992)

    ct, sat, sbt = cost_ref[...], sat_ref[...], sbt_ref[...]
    kvxt = _dot_nt(wt_ref[...], hb)

    def mmt(s):
        return kvxt[s * KV_W:(s + 1) * KV_W, :]

    kvt_ref[0:KV_W, :] = mmt(0)
    kvt_ref[KV_W:2 * KV_W, :] = mmt(1)
    kvt_ref[2 * KV_W:3 * KV_W, :] = _rope128_t(mmt(2), ct, sat, sbt)
    kvt_ref[3 * KV_W:4 * KV_W, :] = mmt(3)
    wint_ref[0:KV_W, :] = _rope128_t(mmt(4), ct, sat, sbt)
    wint_ref[KV_W:2 * KV_W, :] = mmt(5)


def _inproj_prompt(x2d, shift, scale, g, wp, wt, tabs, tabs_t, *, tm, nb_batch, t):
    rows = x2d.shape[0]
    tpb = t // tm
    mod_spec = pl.BlockSpec((None, 1, D_MODEL), lambda i: (i // tpb, 0, 0))
    tab_spec = pl.BlockSpec((tm, LANES), lambda i: (i % tpb, 0))
    tabt_spec = pl.BlockSpec((KV_W, tm), lambda i: (0, i % tpb))
    row = lambda w: pl.BlockSpec((tm, w), lambda i: (i, 0))
    tr = lambda h: pl.BlockSpec((None, h, tm), lambda i: (i // tpb, 0, i % tpb))
    u_spec = pl.BlockSpec((D_S5 // LANES, tm, LANES), lambda i: (0, i, 0))
    outs = [((D_S5 // LANES, rows, LANES), F32, u_spec), ((rows, 512), BF16, row(512)),
            ((rows, 512), BF16, row(512)),
            ((nb_batch, 4 * KV_W, t), F32, tr(4 * KV_W)), ((nb_batch, 2 * KV_W, t), F32, tr(2 * KV_W)),
            ((rows, 256), F32, row(256)), ((rows, 512), BF16, row(512)), ((rows, 2048), BF16, row(2048)),
            ((rows, 128), F32, row(128))]
    return pl.pallas_call(
        _inproj_prompt_kernel,
        grid=(rows // tm,),
        in_specs=[row(D_MODEL), mod_spec, mod_spec,
                  pl.BlockSpec((1, D_MODEL), lambda i: (0, 0)),
                  pl.BlockSpec((D_MODEL, N_PROJ), lambda i: (0, 0)),
                  pl.BlockSpec((6 * KV_W, D_MODEL), lambda i: (0, 0)),
                  tab_spec, tab_spec, tab_spec, tabt_spec, tabt_spec, tabt_spec],
        out_specs=[o[2] for o in outs],
        out_shape=[jax.ShapeDtypeStruct(o[0], o[1]) for o in outs],
        compiler_params=_cparams(("arbitrary",)),
        name="inproj_prompt",
    )(x2d, shift, scale, g, wp, wt, *tabs, *tabs_t)


def _compress(load_p, nbk, pe_ref, w1_ref, w2_ref, c, sa, sb, xcat_sc):
    rows = xcat_sc.shape[0]
    outs = []
    for s in range(2):
        for p in range(BLOCK):
            xp = load_p(s, p) + pe_ref[s, p]
            if rows > nbk:
                xp = jnp.concatenate([xp, jnp.zeros((rows - nbk, KV_W), F32)], axis=0)
            xcat_sc[:, p * KV_W:(p + 1) * KV_W] = xp.astype(BF16)
        acc = _dot(xcat_sc[...], w1_ref[s])[0:nbk]
        outs.append(_dot(_silu(acc).astype(BF16), w2_ref[s]))
    return jnp.concatenate([_rope128(outs[0], c, sa, sb), outs[1]], axis=1)


def _cmp_prompt_kernel(k_ref, v_ref, pe_ref, w1_ref, w2_ref, cos_ref, sa_ref, sb_ref, o_ref, xcat_sc,
                       *, nbk):
    def load_p(s, p):
        return (k_ref, v_ref)[s][pl.ds(p, nbk, stride=BLOCK), :]

    res = _compress(load_p, nbk, pe_ref, w1_ref, w2_ref,
                    cos_ref[0:nbk, :], sa_ref[0:nbk, :], sb_ref[0:nbk, :], xcat_sc)
    o_ref[...] = jnp.zeros(o_ref.shape, F32)
    o_ref[0:nbk, :] = res


def _cmp_prompt(kv2d, nb_batch, t, nbp, pe2, w1bd, w2bd, ctabs):
    nbk = t // BLOCK
    full = lambda shape: pl.BlockSpec(shape, lambda b: (0,) * len(shape))
    return pl.pallas_call(
        functools.partial(_cmp_prompt_kernel, nbk=nbk),
        grid=(nb_batch,),
        in_specs=[pl.BlockSpec((t, KV_W), lambda b: (b, 0)), pl.BlockSpec((t, KV_W), lambda b: (b, 1)),
                  full((2, BLOCK, 1, KV_W)), full((2, BLOCK * KV_W, KV_W)),
                  full((2, KV_W, KV_W)), full((nbp, LANES)), full((nbp, LANES)), full((nbp, LANES))],
        out_specs=pl.BlockSpec((None, nbp, 2 * KV_W), lambda b: (b, 0, 0)),
        out_shape=jax.ShapeDtypeStruct((nb_batch, nbp, 2 * KV_W), F32),
        scratch_shapes=[pltpu.VMEM(((nbk + 15) // 16 * 16, BLOCK * KV_W), BF16)],
        compiler_params=_cparams(("arbitrary",)),
        name="cmp_prompt",
    )(kv2d, kv2d, pe2, w1bd, w2bd, *ctabs)


def _stack_queries(qt, tq):
    lane = lax.broadcasted_iota(jnp.int32, (tq, LANES), 1)
    qt = qt.astype(F32)
    blocks = []
    for g in range(N_KV):
        keep = (lane < HEAD_DIM) if g == 0 else (lane >= HEAD_DIM)
        for r in range(GROUP_SIZE):
            blocks.append(jnp.where(keep, qt[:, r * LANES:(r + 1) * LANES], 0.0))
    return jnp.concatenate(blocks, axis=0).astype(BF16)


def _select_blocks(imp, qblk, nb, n_sel):
    nq, nbp = imp.shape
    n = lax.broadcasted_iota(jnp.int32, (nq, nbp), 1)
    forced = (n == 0) | (n == qblk) | (n == qblk - 1)
    imp = jnp.where(forced, FORCED_SCORE, imp)
    imp = jnp.where(n > qblk, -1.0, imp)
    imp = jnp.where(n >= nb, -2.0, imp)
    rank = jnp.zeros((nq, nbp), F32)
    for m in range(nb):
        col = imp[:, m:m + 1]
        beats = (col > imp) | ((col == imp) & (n > m))
        rank = rank + jnp.where(beats, 1.0, 0.0)
    return jnp.where((rank < n_sel) & (imp > -0.5), 1.0, 0.0)


def _compressed_branch(qs, kcvc, tq_col, tq, nb):
    nbp = kcvc.shape[0]
    kc = kcvc[:, :KV_W].astype(BF16)
    vc = kcvc[:, KV_W:].astype(BF16)
    s = _dot_nt(qs, kc)
    n = lax.broadcasted_iota(jnp.int32, (1, nbp), 1)
    mask = (n * BLOCK + (BLOCK - 1) <= tq_col) & (n < nb)
    p, l = _masked_exp(s, mask)
    p = p * _safe_inv(l)
    o_c = _dot(p.astype(BF16), vc)
    imps = []
    for g in range(N_KV):
        acc = p[(g * GROUP_SIZE) * tq:(g * GROUP_SIZE + 1) * tq]
        for r in range(1, GROUP_SIZE):
            acc = acc + p[(g * GROUP_SIZE + r) * tq:(g * GROUP_SIZE + r + 1) * tq]
        imps.append(acc)
    return o_c, imps


def _block_mask(selstack, k0, tk):
    nbp = selstack.shape[1]
    kblk = (k0 + lax.broadcasted_iota(jnp.int32, (nbp, tk), 1)) // BLOCK
    e = jnp.where(kblk == lax.broadcasted_iota(jnp.int32, (nbp, tk), 0), 1.0, 0.0).astype(BF16)
    return _dot(selstack, e) > 0.5


def _gated_heads(gt, branches, tq):
    lane = lax.broadcasted_iota(jnp.int32, (tq, LANES), 1)
    chunks = []
    for r in range(GROUP_SIZE):
        halves = []
        for g in range(N_KV):
            h = g * GROUP_SIZE + r
            acc = None
            for br, ob in enumerate(branches):
                term = gt[:, br * N_HEADS + h:br * N_HEADS + h + 1] * ob[h * tq:(h + 1) * tq]
                acc = term if acc is None else acc + term
            halves.append(acc)
        chunks.append(jnp.where(lane < HEAD_DIM, halves[0], halves[1]))
    return chunks


def _select_blocks_t(imp, qblk_row, nb, n_sel):
    nq, nbp = imp.shape
    nbr = (nb + 7) // 8 * 8
    x = imp.T[0:nbr, :]
    n = lax.broadcasted_iota(jnp.int32, (nbr, nq), 0)
    forced = (n == 0) | (n == qblk_row) | (n == qblk_row - 1)
    x = jnp.where(forced, FORCED_SCORE, x)
    x = jnp.where(n > qblk_row, -1.0, x)
    x = jnp.where(n >= nb, -2.0, x)
    rank = jnp.zeros((nbr, nq), F32)
    for m in range(nb):
        row = x[m:m + 1, :]
        beats = (row > x) | ((row == x) & (n > m))
        rank = rank + jnp.where(beats, 1.0, 0.0)
    sel = jnp.where((rank < n_sel) & (x > -0.5), 1.0, 0.0)
    if nbr < nbp:
        sel = jnp.concatenate([sel, jnp.zeros((nbp - nbr, nq), F32)], axis=0)
    return sel.T


def _nsa_prompt_kernel(q_ref, kvt_ref, wint_ref, kcvc_ref, gate_ref, e_ref, cbias_ref, wbias_ref,
                       o_ref, kaug_sc, vsel_sc, waug_sc, vwin_sc, s_sc, m_sc, l_sc, acc_sc, sel_sc,
                       *, tq, nb, n_sel):
    i = pl.program_id(1)
    q0 = pl.multiple_of(i * tq, tq)
    rows = N_HEADS * tq
    wk = WINDOW + tq
    tk = 2 * LANES
    neg = jnp.asarray(NEG_INF, F32)

    @pl.when(i == 0)
    def _():
        kaug_sc[0:KV_W, :] = kvt_ref[0:KV_W, :].astype(BF16)
        kaug_sc[KV_W:2 * KV_W, :] = e_ref[...]
        vsel_sc[...] = kvt_ref[KV_W:2 * KV_W, :].astype(BF16)
        waug_sc[...] = jnp.zeros(waug_sc.shape, BF16)
        waug_sc[0:KV_W, WINDOW:] = wint_ref[0:KV_W, :].astype(BF16)
        waug_sc[KV_W:KV_W + 16, 0:WINDOW] = jnp.ones((16, WINDOW), BF16)
        vwin_sc[:, 0:WINDOW] = jnp.zeros((KV_W, WINDOW), BF16)
        vwin_sc[:, WINDOW:] = wint_ref[KV_W:2 * KV_W, :].astype(BF16)

    qs = _stack_queries(q_ref[...], tq)
    t_loc = lax.broadcasted_iota(jnp.int32, (rows, 1), 0) % tq
    tq_col = q0 + t_loc

    o_c, imps = _compressed_branch(qs, kcvc_ref[...], tq_col, tq, nb)

    sel_sc[...] = jnp.ones(sel_sc.shape, F32)

    @pl.when(q0 + tq > n_sel * BLOCK)
    def _():
        qblk_row = (q0 + lax.broadcasted_iota(jnp.int32, (1, tq), 1)) // BLOCK
        for g in range(N_KV):
            sel_sc[g] = _select_blocks_t(imps[g], qblk_row, nb, n_sel)

    selb = [((sel_sc[g] - 1.0) * (-NEG_INF)).astype(BF16) for g in range(N_KV)]
    selstack = jnp.concatenate([selb[g] for g in range(N_KV) for _ in range(GROUP_SIZE)], axis=0)
    qaug = jnp.concatenate([qs, selstack], axis=1)

    lane = lax.broadcasted_iota(jnp.int32, (rows, LANES), 1)
    padcol = jnp.where(lane == 0, neg, 0.0).astype(BF16)
    qaug_w = jnp.concatenate([qs, padcol], axis=1)
    n_full = q0 // tk
    par = (q0 % tk) // tq
    m_sc[...] = jnp.full(m_sc.shape, neg, F32)

    def score_tile(k0, w, bias):
        s = _dot(qaug, kaug_sc[:, pl.ds(k0, w)])
        if bias is not None:
            s = s + bias
        s_sc[:, pl.ds(k0, w)] = s
        mx = s[:, :LANES]
        for c in range(1, w // LANES):
            mx = jnp.maximum(mx, s[:, c * LANES:(c + 1) * LANES])
        m_sc[...] = jnp.maximum(m_sc[...], mx)

    def pass1(j, carry):
        score_tile(pl.multiple_of(j * 2 * tk, 2 * tk), 2 * tk, None)
        return carry

    lax.fori_loop(0, n_full // 2, pass1, 0)

    @pl.when(n_full % 2 == 1)
    def _():
        score_tile(pl.multiple_of((n_full - 1) * tk, tk), tk, None)

    score_tile(pl.multiple_of(n_full * tk, tk), tk, cbias_ref[par])
    m_b = jnp.broadcast_to(jnp.max(m_sc[...], axis=-1, keepdims=True), (rows, LANES))

    l_sc[...] = jnp.zeros(l_sc.shape, F32)
    acc_sc[...] = jnp.zeros(acc_sc.shape, F32)

    def pv_tile(k0, w, m_b):
        s = s_sc[:, pl.ds(k0, w)]
        ps = [jnp.exp(s[:, c * LANES:(c + 1) * LANES] - m_b) for c in range(w // LANES)]
        tot = ps[0]
        for pc in ps[1:]:
            tot = tot + pc
        l_sc[...] = l_sc[...] + tot
        p = jnp.concatenate(ps, axis=1).astype(BF16)
        acc_sc[...] = acc_sc[...] + _dot_nt(p, vsel_sc[:, pl.ds(k0, w)])

    def pass2(j, carry):
        pv_tile(pl.multiple_of(j * 2 * tk, 2 * tk), 2 * tk, m_b)
        return carry

    lax.fori_loop(0, (n_full + 1) // 2, pass2, 0)

    @pl.when(n_full % 2 == 0)
    def _():
        pv_tile(pl.multiple_of(n_full * tk, tk), tk, m_b)

    o_s = acc_sc[...] / jnp.sum(l_sc[...], axis=-1, keepdims=True)

    s = _dot(qaug_w, waug_sc[:, pl.ds(q0, wk)]) + wbias_ref[...]
    p = jnp.exp(s - jnp.max(s, axis=-1, keepdims=True))
    o_w = _dot_nt(p.astype(BF16), vwin_sc[:, pl.ds(q0, wk)]) / jnp.sum(p, axis=-1, keepdims=True)

    for r, ch in enumerate(_gated_heads(jax.nn.sigmoid(gate_ref[...]), (o_c, o_s, o_w), tq)):
        o_ref[:, r * LANES:(r + 1) * LANES] = ch


def _attn_constants(t, tq):
    rows = N_HEADS * tq
    t_loc = np.arange(rows)[:, None] % tq
    k = np.arange(2 * LANES)[None, :]
    cb = np.stack([np.where(k - par * tq <= t_loc, 0.0, NEG_INF) for par in range(2 * LANES // tq)])
    kr = np.arange(WINDOW + tq)[None, :]
    wb = np.where((kr - WINDOW <= t_loc) & (kr > t_loc), 0.0, NEG_INF)
    e = (np.arange(t)[None, :] // BLOCK == np.arange(LANES)[:, None]).astype(np.float32)
    return jnp.asarray(e, BF16), jnp.asarray(cb, F32), jnp.asarray(wb, F32)


def _nsa_prompt(q2d, kvt, wint, kcvc, gate2d, consts, nb_batch, t, tq):
    nb = t // BLOCK
    nq = t // tq
    nbp = kcvc.shape[1]
    rows = N_HEADS * tq
    wk = WINDOW + tq
    assert nbp == LANES and t % (2 * LANES) == 0
    e, cb, wb = consts
    kern = functools.partial(_nsa_prompt_kernel, tq=tq, nb=nb, n_sel=min(N_SELECT, nb))
    return pl.pallas_call(
        kern,
        grid=(nb_batch, nq),
        in_specs=[pl.BlockSpec((tq, D_NSA), lambda b, i: (b * nq + i, 0)),
                  pl.BlockSpec((None, 2 * KV_W, t), lambda b, i: (b, 1, 0)),
                  pl.BlockSpec((None, 2 * KV_W, t), lambda b, i: (b, 0, 0)),
                  pl.BlockSpec((None, nbp, 2 * KV_W), lambda b, i: (b, 0, 0)),
                  pl.BlockSpec((tq, LANES), lambda b, i: (b * nq + i, 0)),
                  pl.BlockSpec((LANES, t), lambda b, i: (0, 0)),
                  pl.BlockSpec((2 * LANES // tq, rows, 2 * LANES), lambda b, i: (0, 0, 0)),
                  pl.BlockSpec((rows, wk), lambda b, i: (0, 0))],
        out_specs=pl.BlockSpec((tq, D_NSA), lambda b, i: (b * nq + i, 0)),
        out_shape=jax.ShapeDtypeStruct((nb_batch * t, D_NSA), F32),
        scratch_shapes=[pltpu.VMEM((2 * KV_W, t), BF16), pltpu.VMEM((KV_W, t), BF16),
                        pltpu.VMEM((2 * KV_W, WINDOW + t), BF16), pltpu.VMEM((KV_W, WINDOW + t), BF16),
                        pltpu.VMEM((rows, t), F32), pltpu.VMEM((rows, LANES), F32),
                        pltpu.VMEM((rows, LANES), F32), pltpu.VMEM((rows, KV_W), F32),
                        pltpu.VMEM((N_KV, tq, LANES), F32)],
        compiler_params=_cparams(("arbitrary", "arbitrary")),
        name="nsa_prompt",
    )(q2d, kvt, wint, kcvc, gate2d, e, cb, wb)


def _page_copy(cache_ref, slabs, sem_ref, pt_ref, layer, b, p, slot, s0, page, s):
    return pltpu.make_async_copy(
        cache_ref.at[pt_ref[b, p], layer, s0 + s],
        slabs[s].at[slot, :, pl.ds(p * page, page)],
        sem_ref.at[slot])


def _gather_pages(cache_ref, slabs, sem_ref, pt_ref, layer, s0, n_pages, page):
    b = pl.program_id(0)
    nb_batch = pl.num_programs(0)
    slot = b % 2

    def copies(bb, sl):
        return [_page_copy(cache_ref, slabs, sem_ref, pt_ref, layer, bb, p, sl, s0, page, s)
                for p in range(n_pages) for s in range(2)]

    @pl.when(b == 0)
    def _():
        for cp in copies(0, 0):
            cp.start()

    @pl.when(b + 1 < nb_batch)
    def _():
        for cp in copies(b + 1, 1 - slot):
            cp.start()

    for cp in copies(b, slot):
        cp.wait()
    return slot


def _cmp_sample_kernel(pt_ref, cache_ref, new_ref, pe_ref, w1_ref, w2_ref, cos_ref, sa_ref, sb_ref,
                       o_ref, stagek_ref, stagev_ref, rowk_ref, rowv_ref, xcat_sc, sem_ref,
                       *, layer, n_pages, page, nbk, nbs):
    stages = (stagek_ref, stagev_ref)
    rowm = (rowk_ref, rowv_ref)
    i = pl.program_id(0)
    n_steps = pl.num_programs(0)

    def copies(step):
        return [_page_copy(cache_ref, stages, sem_ref, pt_ref, layer, step * nbs + bl, p, bl, 0, page, s)
                for bl in range(nbs) for p in range(n_pages) for s in range(2)]

    @pl.when(i == 0)
    def _():
        for cp in copies(0):
            cp.start()

    for cp in copies(i):
        cp.wait()

    bpp = page // BLOCK
    nb_past = n_pages * bpp
    unroll = next(k for k in (8, 4, 2, 1) if n_pages % k == 0)
    rows_b = nbk * CMP_PITCH
    for s in range(2):
        for bl in range(nbs):
            def xpose(it, carry):
                for k in range(unroll):
                    p = it * unroll + k
                    c0 = pl.multiple_of(p * page, page)
                    tile = stages[s][bl, :, pl.ds(c0, page)].T
                    for h in range(bpp):
                        r0 = pl.multiple_of(bl * rows_b + (p * bpp + h) * CMP_PITCH, 8)
                        rowm[s][pl.ds(r0, BLOCK), :] = tile[h * BLOCK:(h + 1) * BLOCK, :]
                return carry

            lax.fori_loop(0, n_pages // unroll, xpose, 0)
            tail0 = bl * rows_b + nb_past * CMP_PITCH
            tail = (bl + 1) * rows_b - tail0
            rowm[s][pl.ds(tail0, tail), :] = jnp.zeros((tail, KV_W), F32)
            rowm[s][pl.ds(tail0, 8), :] = new_ref[bl, :, s * KV_W:(s + 1) * KV_W]

    @pl.when(i + 1 < n_steps)
    def _():
        for cp in copies(i + 1):
            cp.start()

    def load_p(s, p):
        return rowm[s][pl.ds(p, nbs * nbk, stride=CMP_PITCH), :]

    tabs = [jnp.concatenate([t_ref[0:nbk, :]] * nbs, axis=0) for t_ref in (cos_ref, sa_ref, sb_ref)]
    res = _compress(load_p, nbs * nbk, pe_ref, w1_ref, w2_ref, *tabs, xcat_sc)
    o_ref[...] = jnp.zeros(o_ref.shape, F32)
    for bl in range(nbs):
        o_ref[bl, 0:nbk, :] = res[bl * nbk:(bl + 1) * nbk]


def _cmp_sample(page_table, cache5, newcmp, layer, nbp, pe2, w1bd, w2bd, ctabs, t_new):
    nb_batch, n_pages = page_table.shape
    page = cache5.shape[4]
    past = n_pages * page
    nbk = ((past + t_new + BLOCK - 1) // BLOCK + 7) // 8 * 8
    nbs = 2
    slab_rows = nbs * nbk * CMP_PITCH
    assert page % BLOCK == 0 and t_new <= 8 and nb_batch % nbs == 0
    full = lambda shape: pl.BlockSpec(shape, lambda b, pt: (0,) * len(shape))
    kern = functools.partial(_cmp_sample_kernel, layer=layer, n_pages=n_pages, page=page, nbk=nbk,
                             nbs=nbs)
    return pl.pallas_call(
        kern,
        grid_spec=pltpu.PrefetchScalarGridSpec(
            num_scalar_prefetch=1,
            grid=(nb_batch // nbs,),
            in_specs=[pl.BlockSpec(memory_space=pl.ANY),
                      pl.BlockSpec((nbs, 8, 2 * KV_W), lambda b, pt: (b, 0, 0)),
                      full((2, BLOCK, 1, KV_W)), full((2, BLOCK * KV_W, KV_W)),
                      full((2, KV_W, KV_W)), full((nbp, LANES)), full((nbp, LANES)),
                      full((nbp, LANES))],
            out_specs=pl.BlockSpec((nbs, nbp, 2 * KV_W), lambda b, pt: (b, 0, 0)),
            scratch_shapes=[pltpu.VMEM((nbs, KV_W, past), F32), pltpu.VMEM((nbs, KV_W, past), F32),
                            pltpu.VMEM((slab_rows, KV_W), F32), pltpu.VMEM((slab_rows, KV_W), F32),
                            pltpu.VMEM(((nbs * nbk + 15) // 16 * 16, BLOCK * KV_W), BF16),
                            pltpu.SemaphoreType.DMA((nbs,))]),
        out_shape=jax.ShapeDtypeStruct((nb_batch, nbp, 2 * KV_W), F32),
        compiler_params=_cparams(("arbitrary",)),
        name="cmp_sample",
    )(page_table, cache5, newcmp, pe2, w1bd, w2bd, *ctabs)


def _nsa_sample_kernel(pt_ref, cache_ref, q_ref, newt_ref, wnewt_ref, swint_ref, kcvc_ref, gate_ref,
                       o_ref, wout_ref, slabk_ref, slabv_ref, wslab_ref, sem_ref,
                       *, layer, n_pages, page, t_new, nb, n_sel):
    tq = 8
    past = n_pages * page
    rows = N_HEADS * tq
    slabs = (slabk_ref, slabv_ref)
    slot = _gather_pages(cache_ref, slabs, sem_ref, pt_ref, layer, 2, n_pages, page)
    nk = slabk_ref.shape[2]
    for s in range(2):
        slabs[s][slot, :, pl.ds(past, LANES)] = newt_ref[s]

    qs = _stack_queries(q_ref[...], tq)
    t_loc = lax.broadcasted_iota(jnp.int32, (rows, 1), 0) % tq
    tq_col = past + t_loc

    o_c, imps = _compressed_branch(qs, kcvc_ref[...], tq_col, tq, nb)
    qblk = (past + lax.broadcasted_iota(jnp.int32, (tq, 1), 0)) // BLOCK
    sels = [_select_blocks(imps[g], qblk, nb, n_sel).astype(BF16) for g in range(N_KV)]
    selstack = jnp.concatenate([sels[g] for g in range(N_KV) for _ in range(GROUP_SIZE)], axis=0)

    s = _dot(qs, slabk_ref[slot].astype(BF16))
    kpos = lax.broadcasted_iota(jnp.int32, (1, nk), 1)
    p, l = _masked_exp(s, _block_mask(selstack, 0, nk) & (kpos <= tq_col))
    o_s = _dot_nt(p.astype(BF16), slabv_ref[slot].astype(BF16)) * _safe_inv(l)

    wbuf = swint_ref.shape[2]
    wk = wslab_ref.shape[2]
    wslab_ref[:, :, 0:wbuf] = swint_ref[...]
    wslab_ref[:, :, wbuf:wk] = wnewt_ref[...]
    s = _dot(qs, wslab_ref[0].astype(BF16))
    kpos = past - wbuf + lax.broadcasted_iota(jnp.int32, (1, wk), 1)
    p, l = _masked_exp(s, (kpos <= tq_col) & (kpos > tq_col - WINDOW) & (kpos >= 0))
    o_w = _dot_nt(p.astype(BF16), wslab_ref[1].astype(BF16)) * _safe_inv(l)
    wout_ref[...] = wslab_ref[:, :, t_new:t_new + wbuf]

    for r, ch in enumerate(_gated_heads(jax.nn.sigmoid(gate_ref[...]), (o_c, o_s, o_w), tq)):
        o_ref[:, r * LANES:(r + 1) * LANES] = ch


def _nsa_sample(page_table, cache5, q3, newselt, wnewt, state_wint, kcvc, gate3, layer, t_new):
    nb_batch, n_pages = page_table.shape
    page = cache5.shape[4]
    past = n_pages * page
    nb = (past + t_new + BLOCK - 1) // BLOCK
    nk = past + LANES
    wbuf = state_wint.shape[4]
    wk = wbuf + LANES
    nbp = kcvc.shape[1]
    kern = functools.partial(_nsa_sample_kernel, layer=layer, n_pages=n_pages, page=page,
                             t_new=t_new, nb=nb, n_sel=min(N_SELECT, nb))
    per_b = lambda d1, d2: pl.BlockSpec((None, d1, d2), lambda b, pt: (b, 0, 0))
    per_b4 = lambda d1, d2, d3: pl.BlockSpec((None, d1, d2, d3), lambda b, pt: (b, 0, 0, 0))
    return pl.pallas_call(
        kern,
        grid_spec=pltpu.PrefetchScalarGridSpec(
            num_scalar_prefetch=1,
            grid=(nb_batch,),
            in_specs=[pl.BlockSpec(memory_space=pl.ANY),
                      per_b(8, D_NSA), per_b4(2, KV_W, LANES), per_b4(2, KV_W, LANES),
                      pl.BlockSpec((None, None, 2, KV_W, wbuf), lambda b, pt: (b, layer, 0, 0, 0)),
                      per_b(nbp, 2 * KV_W), per_b(8, LANES)],
            out_specs=[per_b(8, D_NSA), per_b4(2, KV_W, wbuf)],
            scratch_shapes=[pltpu.VMEM((2, KV_W, nk), F32), pltpu.VMEM((2, KV_W, nk), F32),
                            pltpu.VMEM((2, KV_W, wk), F32), pltpu.SemaphoreType.DMA((2,))]),
        out_shape=[jax.ShapeDtypeStruct((nb_batch, 8, D_NSA), F32),
                   jax.ShapeDtypeStruct((nb_batch, 2, KV_W, wbuf), F32)],
        compiler_params=_cparams(("arbitrary",)),
        name="nsa_sample",
    )(page_table, cache5, q3, newselt, wnewt, state_wint, kcvc, gate3)


def _s5_setup_kernel(are_ref, aim_ref, ldt_ref, bre_ref, bim_ref, cre_ref, cim_ref,
                     ckr_ref, cki_ref, bkr_ref, bki_ref, m_ref, pwr_ref, pwi_ref):
    nk = pwr_ref.shape[1]

    def body(g, carry):
        ar = are_ref[g]
        ai = aim_ref[g]
        dt = jnp.exp(ldt_ref[g])
        mag = jnp.exp(ar * dt)
        abr = mag * jnp.cos(ai * dt)
        abi = mag * jnp.sin(ai * dt)
        den = ar * ar + ai * ai
        nr = abr - 1.0
        e_re = (nr * ar + abi * ai) / den
        e_im = (abi * ar - nr * ai) / den
        bre = bre_ref[g]
        bim = bim_ref[g]
        bbr = e_re * bre - e_im * bim
        bbi = e_re * bim + e_im * bre
        kk = lax.broadcasted_iota(jnp.int32, (nk, S5_STATE), 0).astype(F32)
        pmag = jnp.exp(kk * (ar * dt))
        pwr = pmag * jnp.cos(kk * (ai * dt))
        pwi = pmag * jnp.sin(kk * (ai * dt))
        pwr_ref[g] = pwr
        pwi_ref[g] = pwi
        cre = cre_ref[g]
        cim = cim_ref[g]
        for k in range(S5_CHUNK + 1):
            wr = pwr[k:k + 1, :]
            wi = pwi[k:k + 1, :]
            ckr_ref[g, k * S5_GROUP:(k + 1) * S5_GROUP, :] = cre * wr - cim * wi
            cki_ref[g, k * S5_GROUP:(k + 1) * S5_GROUP, :] = -(cre * wi + cim * wr)
            if k < S5_CHUNK:
                bkr_ref[g, k * S5_GROUP:(k + 1) * S5_GROUP, :] = bbr * wr - bbi * wi
                bki_ref[g, k * S5_GROUP:(k + 1) * S5_GROUP, :] = bbr * wi + bbi * wr
        nl = S5_CHUNK * S5_GROUP
        hp = lax.Precision.HIGHEST
        m_ref[g] = (lax.dot_general(ckr_ref[g, 0:nl, :], bbr, (((1,), (1,)), ((), ())),
                                    precision=hp, preferred_element_type=F32)
                    + lax.dot_general(cki_ref[g, 0:nl, :], bbi, (((1,), (1,)), ((), ())),
                                      precision=hp, preferred_element_type=F32))
        return carry

    lax.fori_loop(0, S5_GROUPS, body, 0)


def _s5_setup(a_re, a_im, log_dt, b_re, b_im, c_re, c_im):
    g, p, c = S5_GROUPS, S5_STATE, S5_GROUP
    nk = 24
    outs = [(DEPTH, g, (S5_CHUNK + 1) * c, p), (DEPTH, g, (S5_CHUNK + 1) * c, p),
            (DEPTH, g, S5_CHUNK * c, p), (DEPTH, g, S5_CHUNK * c, p),
            (DEPTH, g, S5_CHUNK * c, c), (DEPTH, g, nk, p), (DEPTH, g, nk, p)]
    lay = lambda s: pl.BlockSpec((None,) + s[1:], lambda l: (l,) + (0,) * (len(s) - 1))
    ins = [a_re.reshape(DEPTH, g, 1, p), a_im.reshape(DEPTH, g, 1, p),
           jnp.broadcast_to(log_dt[:, :, None, None], (DEPTH, g, 1, p)),
           b_re.transpose(0, 1, 3, 2), b_im.transpose(0, 1, 3, 2), c_re, c_im]
    return pl.pallas_call(
        _s5_setup_kernel,
        grid=(DEPTH,),
        in_specs=[lay(x.shape) for x in ins],
        out_specs=[lay(s) for s in outs],
        out_shape=[jax.ShapeDtypeStruct(s, F32) for s in outs],
        compiler_params=_cparams(("arbitrary",)),
        name="s5_setup",
    )(*ins)


def _s5_chunk_kernel(u_ref, h0_ref, kmat_ref, smat_ref, ymat_ref, a1_ref, a2_ref,
                     y_ref, hl_ref, s_sc, hp_sc, *, gb, nj, nbt):
    for gi in range(gb):
        s_sc[gi] = _dot(u_ref[gi], smat_ref[gi])

    def step(j, hs):
        r0 = pl.multiple_of(j * nbt, nbt)
        new = []
        for gi in range(gb):
            h = hs[gi]
            hp_sc[gi, pl.ds(r0, nbt), :] = h
            new.append(a1_ref[gi] * h + a2_ref[gi] * pltpu.roll(h, S5_STATE, 1)
                       + s_sc[gi, pl.ds(r0, nbt), :])
        return tuple(new)

    hs = lax.fori_loop(0, nj, step, tuple(h0_ref[gi] for gi in range(gb)))
    for gi in range(gb):
        hl_ref[gi] = hs[gi]
        y_ref[gi] = _dot(u_ref[gi], kmat_ref[gi]) + _dot(hp_sc[gi].astype(BF16), ymat_ref[gi])


def _s5_chunk(u_g, h0_g, kmat, smat, ymat, a1, a2, nbt):
    g, rows, w = u_g.shape
    nj = rows // nbt
    gb = 8
    st = 2 * S5_STATE
    blk = lambda d1, d2: pl.BlockSpec((gb, d1, d2), lambda i: (i, 0, 0))
    kern = functools.partial(_s5_chunk_kernel, gb=gb, nj=nj, nbt=nbt)
    return pl.pallas_call(
        kern,
        grid=(g // gb,),
        in_specs=[blk(rows, w), blk(nbt, st), blk(w, w), blk(w, st), blk(st, w), blk(1, st), blk(1, st)],
        out_specs=[blk(rows, w), blk(nbt, st)],
        out_shape=[jax.ShapeDtypeStruct((g, rows, w), F32), jax.ShapeDtypeStruct((g, nbt, st), F32)],
        scratch_shapes=[pltpu.VMEM((gb, rows, st), F32), pltpu.VMEM((gb, rows, st), F32)],
        compiler_params=_cparams(("arbitrary",)),
        name="s5_chunk",
    )(u_g, h0_g, kmat, smat, ymat, a1, a2)


def _s5_rows_kernel(u_ref, w_ref, sw_ref, yw_ref, a1_ref, a2_ref, y_ref, hl_ref,
                    xr_sc, sg_sc, sgs_sc, hp_sc, hcat_sc, *, nbl, nj):
    L = S5_CHUNK
    mr = nbl * nj
    ng = LANES // S5_GROUP
    st = 2 * S5_STATE
    for s in range(L):
        xr_sc[:, (L - 1 - s) * LANES:(L - s) * LANES] = u_ref[pl.ds(s, mr, stride=L), :].astype(BF16)
    sall = _dot(xr_sc[...], sw_ref[...])
    for g in range(ng):
        s_g = sall[:, g * st:(g + 1) * st]
        sg_sc[g] = s_g
        sgs_sc[g] = pltpu.roll(s_g, S5_STATE, 1)

    def step(j, carry):
        hs, hss = carry
        new, news = [], []
        for g in range(ng):
            h, hsw = hs[g], hss[g]
            a1, a2 = a1_ref[g], a2_ref[g]
            hp_sc.at[g][pl.ds(j, nbl, stride=nj), :] = h
            new.append(a1 * h + a2 * hsw + sg_sc.at[g][pl.ds(j, nbl, stride=nj), :])
            news.append(a1 * hsw - a2 * h + sgs_sc.at[g][pl.ds(j, nbl, stride=nj), :])
        return tuple(new), tuple(news)

    zero = tuple(jnp.zeros((nbl, st), F32) for _ in range(ng))
    hs, _ = lax.fori_loop(0, nj, step, (zero, zero))
    for g in range(ng):
        hl_ref[g] = hs[g]
        hcat_sc[:, g * st:(g + 1) * st] = hp_sc[g].astype(BF16)
    for tp in range(L // 2):
        t1 = 2 * tp + 1
        y = (_dot(xr_sc[:, (L - 1 - t1) * LANES:], w_ref[0:(t1 + 1) * LANES, :])
             + _dot(hcat_sc[...], yw_ref[:, (t1 - 1) * LANES:(t1 + 1) * LANES]))
        y_ref[pl.ds(t1 - 1, mr, stride=L), :] = y[:, :LANES]
        y_ref[pl.ds(t1, mr, stride=L), :] = y[:, LANES:]


def _s5_rows(u4, w, sw, yw, a1, a2, nb_batch, t, layer):
    nch, rows, _ = u4.shape
    nsplit = 2
    nbl = nb_batch // nsplit
    nj = t // S5_CHUNK
    rb = nbl * t
    mr = nbl * nj
    ng = LANES // S5_GROUP
    st = 2 * S5_STATE
    kern = functools.partial(_s5_rows_kernel, nbl=nbl, nj=nj)
    per_c = lambda *s: pl.BlockSpec((None, None) + s, lambda c, h: (layer, c) + (0,) * len(s))
    return pl.pallas_call(
        kern,
        grid=(nch, nsplit),
        in_specs=[pl.BlockSpec((None, rb, LANES), lambda c, h: (c, h, 0)),
                  per_c(S5_CHUNK * LANES, 2 * LANES), per_c(S5_CHUNK * LANES, ng * st),
                  per_c(ng * st, S5_CHUNK * LANES), per_c(ng, 1, st), per_c(ng, 1, st)],
        out_specs=[pl.BlockSpec((None, rb, LANES), lambda c, h: (c, h, 0)),
                   pl.BlockSpec((None, ng, None, nbl, st), lambda c, h: (c, 0, h, 0, 0))],
        out_shape=[jax.ShapeDtypeStruct((nch, rows, LANES), F32),
                   jax.ShapeDtypeStruct((nch, ng, nsplit, nbl, st), F32)],
        scratch_shapes=[pltpu.VMEM((mr, S5_CHUNK * LANES), BF16), pltpu.VMEM((ng, mr, st), F32),
                        pltpu.VMEM((ng, mr, st), F32), pltpu.VMEM((ng, mr, st), F32),
                        pltpu.VMEM((mr, ng * st), BF16)],
        compiler_params=_cparams(("arbitrary", "arbitrary")),
        name="s5_rows",
    )(u4, w, sw, yw, a1, a2)


def _post_kernel(x_ref, gmod_ref, yssm_ref, u_ref, zs5_ref, o_ref, znsa_ref, merge_ref,
                 d_ref, gluw_ref, glub_ref, ws5_ref, wnsa_ref, wo_ref, fg_ref, xo_ref, *, final):
    y = jnp.concatenate([yssm_ref[ch] + d_ref[ch] * u_ref[ch] for ch in range(D_S5 // LANES)], axis=1)
    y = 0.5 * y * (1.0 + jnp.tanh(math.sqrt(2.0 / math.pi) * (y + 0.044715 * (y * y * y))))
    y = y * jax.nn.sigmoid(_dot(y.astype(BF16), gluw_ref[...]) + glub_ref[...])
    y = y * _silu(zs5_ref[...].astype(F32))
    b_s5 = _dot(y.astype(BF16), ws5_ref[...])
    b_nsa = _dot((o_ref[...] * _silu(znsa_ref[...].astype(F32))).astype(BF16), wnsa_ref[...])
    m = jax.nn.sigmoid(merge_ref[...].astype(F32))
    mix = m[:, :D_MODEL] * b_s5 + m[:, D_MODEL:] * b_nsa
    x = x_ref[...] + gmod_ref[...] * _dot(mix.astype(BF16), wo_ref[...])
    if final:
        x = x * lax.rsqrt(jnp.mean(x * x, axis=-1, keepdims=True) + RMS_EPS) * fg_ref[...]
    xo_ref[...] = x


def _post(x2d, gmod, yssm, u, zs5, o, znsa, merge, d, gluw, glub, ws5, wnsa, wo, fg,
          *, tm, tiles_per_b, per_row_mod, final):
    rows = x2d.shape[0]
    if per_row_mod:
        mod_spec = pl.BlockSpec((tm, D_MODEL), lambda i: (i, 0))
    else:
        mod_spec = pl.BlockSpec((None, 1, D_MODEL), lambda i: (i // tiles_per_b, 0, 0))
    row = lambda w: pl.BlockSpec((tm, w), lambda i: (i, 0))
    full = lambda a, b: pl.BlockSpec((a, b), lambda i: (0, 0))
    nch = D_S5 // LANES
    chunked = pl.BlockSpec((nch, tm, LANES), lambda i: (0, i, 0))
    return pl.pallas_call(
        functools.partial(_post_kernel, final=final),
        grid=(rows // tm,),
        in_specs=[row(D_MODEL), mod_spec, chunked, chunked, row(512), row(512), row(512), row(2048),
                  pl.BlockSpec((nch, 1, LANES), lambda i: (0, 0, 0)),
                  full(512, 512), full(1, 512), full(512, D_MODEL), full(512, D_MODEL),
                  full(D_MODEL, D_MODEL), full(1, D_MODEL)],
        out_specs=row(D_MODEL),
        out_shape=jax.ShapeDtypeStruct((rows, D_MODEL), F32),
        compiler_params=_cparams(("arbitrary",)),
        name="post",
    )(x2d, gmod, yssm, u, zs5, o, znsa, merge, d, gluw, glub, ws5, wnsa, wo, fg)


def _head_perm():
    idx = [HEAD_DIM * (GROUP_SIZE * g + r) + d
           for r in range(GROUP_SIZE) for g in range(N_KV) for d in range(HEAD_DIM)]
    return np.asarray(idx, np.int32)


def _rope_tables(pos, width=LANES):
    inv = ROPE_THETA ** (-jnp.arange(ROT_HALF, dtype=F32) / ROT_HALF)
    ang = pos.astype(F32)[:, None] * inv[None, :]
    cos, sin = jnp.cos(ang), jnp.sin(ang)
    n = pos.shape[0]
    one = jnp.ones((n, HEAD_DIM - 2 * ROT_HALF), F32)
    zero8 = jnp.zeros((n, ROT_HALF), F32)
    zero = jnp.zeros((n, HEAD_DIM - 2 * ROT_HALF), F32)
    c = jnp.concatenate([cos, cos, one], axis=1)
    sa = jnp.concatenate([-sin, zero8, zero], axis=1)
    sb = jnp.concatenate([zero8, sin, zero], axis=1)
    rep = width // HEAD_DIM
    return tuple(jnp.tile(t, (1, rep)) for t in (c, sa, sb))


def _blockdiag2(w):
    z = jnp.zeros_like(w)
    return jnp.concatenate([jnp.concatenate([w, z], -1), jnp.concatenate([z, w], -1)], -2)


def _s5_matrices(tabs, l, t_eff):
    ckr, cki, bkr, bki, m, pwr, pwi = [t[l] for t in tabs]
    g, c, L = S5_GROUPS, S5_GROUP, S5_CHUNK
    mk = m.reshape(g, L, c, c)
    s_idx = np.arange(L)[:, None]
    t_idx = np.arange(L)[None, :]
    lag = np.clip(t_idx - s_idx, 0, L - 1)
    kfull = mk[:, lag]
    kfull = jnp.where((t_idx >= s_idx)[None, :, :, None, None], kfull, 0.0)
    kmat = kfull.transpose(0, 1, 4, 2, 3).reshape(g, L * c, L * c).astype(BF16)
    ck = jnp.concatenate([ckr, cki], axis=-1)
    ymat = ck[:, c:, :].transpose(0, 2, 1).astype(BF16)
    bk = jnp.concatenate([bkr, bki], axis=-1).reshape(g, L, c, 2 * S5_STATE)
    sm = bk[:, t_eff - 1::-1] if t_eff == L else bk[:, np.arange(t_eff - 1, -1, -1)]
    if t_eff < L:
        sm = jnp.concatenate([sm, jnp.zeros((g, L - t_eff, c, 2 * S5_STATE), F32)], axis=1)
    smat = sm.reshape(g, L * c, 2 * S5_STATE).astype(BF16)
    ar = pwr[:, t_eff][:, None, :]
    ai = pwi[:, t_eff][:, None, :]
    a1 = jnp.concatenate([ar, ar], axis=-1)
    a2 = jnp.concatenate([-ai, ai], axis=-1)
    return kmat, smat, ymat, a1, a2


def _s5_row_weights(tabs):
    ckr, cki, bkr, bki, m, pwr, pwi = tabs
    c, L, st = S5_GROUP, S5_CHUNK, 2 * S5_STATE
    ng = LANES // c
    nch = S5_GROUPS // ng
    eye = jnp.eye(ng, dtype=BF16)
    mk = m.astype(BF16).reshape(DEPTH, nch, ng, L, c, c)
    w = jnp.einsum('dhgloi,gq->dhlgiqo', mk, eye).reshape(DEPTH, nch, L * LANES, LANES)
    w_shift = jnp.concatenate([jnp.zeros_like(w[:, :, :LANES]), w[:, :, :-LANES]], axis=2)
    w2 = jnp.concatenate([w_shift, w], axis=-1)
    bk = jnp.concatenate([bkr, bki], axis=-1).astype(BF16).reshape(DEPTH, nch, ng, L, c, st)
    sw = jnp.einsum('dhgkcp,gq->dhkgcqp', bk, eye).reshape(DEPTH, nch, L * LANES, ng * st)
    ck = jnp.concatenate([ckr, cki], axis=-1).astype(BF16).reshape(DEPTH, nch, ng, L + 1, c, st)[:, :, :, 1:]
    yw = jnp.einsum('dhgtcp,gq->dhgptqc', ck, eye).reshape(DEPTH, nch, ng * st, L * LANES)
    ar = pwr[:, :, L].reshape(DEPTH, nch, ng, 1, S5_STATE)
    ai = pwi[:, :, L].reshape(DEPTH, nch, ng, 1, S5_STATE)
    a1 = jnp.concatenate([ar, ar], axis=-1)
    a2 = jnp.concatenate([-ai, ai], axis=-1)
    return w2, sw, yw, a1, a2


def _all_layer_weights(w_in, cmp_pe, cmp_w1, cmp_w2, s5_glu_w, w_s5_out, w_nsa_out, w_o, perm):
    gate_w = jnp.pad(w_in[:, :, 2304:2328], ((0, 0), (0, 0), (0, LANES - 3 * N_HEADS)))
    wp = jnp.concatenate([w_in[:, :, :1024], w_in[:, :, 1024:1536][:, :, perm], w_in[:, :, 1536:2304],
                          w_in[:, :, 2328:2840][:, :, perm], w_in[:, :, 2840:], gate_w], axis=2).astype(BF16)
    pe2 = jnp.concatenate([cmp_pe, cmp_pe], axis=-1)[:, :, :, None, :]
    w1bd = _blockdiag2(cmp_w1.astype(BF16).reshape(DEPTH, 2, BLOCK, HEAD_DIM, HEAD_DIM))
    w1bd = w1bd.reshape(DEPTH, 2, BLOCK * KV_W, KV_W)
    w2bd = _blockdiag2(cmp_w2.astype(BF16))
    wt = w_in[:, :, 1536:2304].transpose(0, 2, 1).astype(BF16)
    return dict(wp=wp, wt=wt, pe2=pe2, w1bd=w1bd, w2bd=w2bd, gluw=s5_glu_w.astype(BF16),
                ws5=w_s5_out.astype(BF16), wnsa=w_nsa_out[:, perm, :].astype(BF16),
                wo=w_o.astype(BF16))


def _to_groups(u2d, nbt, nj):
    u5 = u2d.reshape(nbt, nj, S5_CHUNK, S5_GROUPS, S5_GROUP)
    return u5.transpose(3, 1, 0, 2, 4).reshape(S5_GROUPS, nj * nbt, S5_CHUNK * S5_GROUP)


def _from_groups(y_g, nbt, nj):
    y5 = y_g.reshape(S5_GROUPS, nj, nbt, S5_CHUNK, S5_GROUP)
    return y5.transpose(2, 1, 3, 0, 4).reshape(nbt * nj * S5_CHUNK, D_S5)


def _state_to_groups(h):
    return h.transpose(2, 0, 1, 3).reshape(S5_GROUPS, h.shape[0], 2 * S5_STATE)


def _state_from_groups(hg):
    g, b, _ = hg.shape
    return hg.reshape(g, b, 2, S5_STATE).transpose(1, 2, 0, 3)


def kernel(x_prompt, x_sample, c_prompt, c_sample, cache_kv, page_table, state_win, state_ssm, ada_w, ada_b, norm_g, w_in, s5_a_re, s5_a_im, s5_log_dt, s5_b_re, s5_b_im, s5_c_re, s5_c_im, s5_d, s5_glu_w, s5_glu_b, cmp_pe, cmp_w1, cmp_w2, w_s5_out, w_nsa_out, w_o, final_g):
    bp, tp, _ = x_prompt.shape
    bs, ts, _ = x_sample.shape
    n_pool, _, page = cache_kv.shape[:3]
    n_pages = page_table.shape[1]
    past = n_pages * page
    wbuf = state_win.shape[2]
    assert tp % S5_CHUNK == 0 and tp % 128 == 0 and ts <= 8 and bs % 8 == 0 and bp % 8 == 0

    perm = _head_perm()
    mod = _ada_mod(jnp.concatenate([c_prompt, c_sample], axis=0), ada_w, ada_b)
    s5tabs = _s5_setup(s5_a_re, s5_a_im, s5_log_dt, s5_b_re, s5_b_im, s5_c_re, s5_c_im)

    tabs_p = _rope_tables(jnp.arange(tp))
    rs = bs * ts
    tabs_s = _rope_tables(jnp.tile(past + jnp.arange(ts), bs))
    nb_p = tp // BLOCK
    nbp_p = (nb_p + LANES - 1) // LANES * LANES
    ctabs_p = _rope_tables(jnp.arange(nbp_p) * BLOCK + (BLOCK - 1))
    nb_s = (past + ts + BLOCK - 1) // BLOCK
    nbp_s = ((nb_s + 7) // 8 * 8 + LANES - 1) // LANES * LANES
    ctabs_s = _rope_tables(jnp.arange(nbp_s) * BLOCK + (BLOCK - 1))

    cache5 = cache_kv.transpose(0, 1, 3, 4, 5, 2).reshape(n_pool, DEPTH, 4, KV_W, page)
    state_wint = state_win.transpose(0, 1, 3, 4, 5, 2).reshape(bs, DEPTH, 2, KV_W, wbuf)
    tabs_pt = tuple(tb.T for tb in tabs_p)
    tm_p = 512 if tp % 512 == 0 else 256
    tq_p = 256
    nj_p = tp // S5_CHUNK
    consts = _attn_constants(tp, tq_p)
    keep = min(WINDOW, tp)

    xp = x_prompt.reshape(bp * tp, D_MODEL)
    xs = x_sample.reshape(rs, D_MODEL)
    kv_p, kv_s, win_p, win_s, ssm_p, ssm_s = [], [], [], [], [], []

    def pad_rows(a3):
        return jnp.pad(a3, ((0, 0), (0, 8 - ts), (0, 0)))

    def to_chunks(a2):
        return a2.reshape(a2.shape[0], D_S5 // LANES, LANES).transpose(1, 0, 2)

    def new_tiles(a3):
        a4 = a3.reshape(bs, ts, 2, KV_W).transpose(0, 2, 3, 1)
        return jnp.pad(a4, ((0, 0), (0, 0), (0, 0), (0, LANES - ts)))

    lw_all = _all_layer_weights(w_in, cmp_pe, cmp_w1, cmp_w2, s5_glu_w, w_s5_out, w_nsa_out, w_o, perm)
    s5w_all = _s5_row_weights(s5tabs)

    for l in range(DEPTH):
        lw = {name: val[l] for name, val in lw_all.items()}
        g_row = norm_g[l][None, :]
        d_row = s5_d[l].reshape(D_S5 // LANES, 1, LANES)
        glub = s5_glu_b[l][None, :]
        mp, ms = mod[l, :bp], mod[l, bp:]
        post_w = (d_row, lw['gluw'], glub, lw['ws5'], lw['wnsa'], lw['wo'], final_g[None, :])
        last = l == DEPTH - 1

        shift, scale, gmod = [mp[:, k * D_MODEL:(k + 1) * D_MODEL][:, None, :] for k in range(3)]
        u, zs5, q, kvt, wint, cmp, znsa, merge, gate = _inproj_prompt(
            xp, shift, scale, g_row, lw['wp'], lw['wt'], tabs_p, tabs_pt, tm=tm_p, nb_batch=bp, t=tp)
        kcvc = _cmp_prompt(cmp, bp, tp, nbp_p, lw['pe2'], lw['w1bd'], lw['w2bd'], ctabs_p)
        o = _nsa_prompt(q, kvt, wint, kcvc, gate, consts, bp, tp, tq_p)
        yssm, hl = _s5_rows(u, *s5w_all, bp, tp, l)
        xp = _post(xp, gmod, yssm, u, zs5, o, znsa, merge, *post_w,
                   tm=tm_p, tiles_per_b=tp // tm_p, per_row_mod=False, final=last)
        kv_p.append(kvt)
        win_p.append(wint[:, :, tp - keep:])
        ssm_p.append(_state_from_groups(hl.reshape(S5_GROUPS, bp, 2 * S5_STATE)))

        shift, scale, gmod = [jnp.repeat(ms[:, k * D_MODEL:(k + 1) * D_MODEL], ts, axis=0) for k in range(3)]
        u, zs5, q, kv, win, znsa, merge, gate = _inproj(
            xs, shift, scale, g_row, lw['wp'], tabs_s, tm=rs, tiles_per_b=1, per_row_mod=True)
        kv3 = kv.reshape(bs, ts, 4 * KV_W)
        kcvc = _cmp_sample(page_table, cache5, pad_rows(kv3[:, :, :2 * KV_W]), l, nbp_s,
                           lw['pe2'], lw['w1bd'], lw['w2bd'], ctabs_s, ts)
        o8, wst = _nsa_sample(page_table, cache5, pad_rows(q.reshape(bs, ts, D_NSA)),
                              new_tiles(kv3[:, :, 2 * KV_W:]), new_tiles(win.reshape(bs, ts, 2 * KV_W)),
                              state_wint, kcvc, pad_rows(gate.reshape(bs, ts, LANES)), l, ts)
        o = o8[:, :ts].reshape(rs, D_NSA)
        mats = _s5_matrices(s5tabs, l, ts)
        u_pad = jnp.pad(u.reshape(bs, ts, D_S5), ((0, 0), (0, S5_CHUNK - ts), (0, 0)))
        y_g, hl = _s5_chunk(_to_groups(u_pad.reshape(bs * S5_CHUNK, D_S5), bs, 1).astype(BF16),
                            _state_to_groups(state_ssm[:, l]), *mats, nbt=bs)
        yssm = _from_groups(y_g, bs, 1).reshape(bs, S5_CHUNK, D_S5)[:, :ts].reshape(rs, D_S5)
        xs = _post(xs, gmod, to_chunks(yssm), to_chunks(u), zs5, o, znsa, merge, *post_w,
                   tm=rs, tiles_per_b=1, per_row_mod=True, final=last)
        kv_s.append(kv.reshape(bs, ts, 4, N_KV, HEAD_DIM))
        win_s.append(wst)
        ssm_s.append(_state_from_groups(hl))

    def from_t(parts, n_streams):
        a = jnp.stack(parts, axis=1)
        nbt, keys = a.shape[0], a.shape[-1]
        return a.reshape(nbt, DEPTH, n_streams, N_KV, HEAD_DIM, keys).transpose(0, 1, 5, 2, 3, 4)

    y_prompt = xp.reshape(bp, tp, D_MODEL)
    y_sample = xs.reshape(bs, ts, D_MODEL)
    return (y_prompt, y_sample, from_t(kv_p, 4), jnp.stack(kv_s, axis=1),
            from_t(win_p, 2), from_t(win_s, 2),
            jnp.stack(ssm_p, axis=1), jnp.stack(ssm_s, axis=1))
```
